```python
import math
import jax, jax.numpy as jnp
from jax import lax
import numpy as np

D_MODEL = 1024
BATCH = 32
SEQ = 256
DEPTH = 2
DEC_BATCH = 8
DEC_SEQ = 1024
PAST_LEN = 512

GRID_W = 64
MIX_W = D_MODEL // 2
CONV_W = MIX_W
CONV_K = 3
HEAD_DIM = 64
ATT_HEADS = MIX_W // HEAD_DIM
ATT_KV_HEADS = 2
ATT_GROUP = ATT_HEADS // ATT_KV_HEADS
ATT_Q_W = ATT_HEADS * HEAD_DIM
ATT_KV_W = ATT_KV_HEADS * HEAD_DIM
WINDOW = 128
ATT_BLOCK = 128
ATT_SCALE = HEAD_DIM ** -0.5
ROPE_THETA = 10000.0
NEG_INF = -1e30
GLA_HEADS = 4
GLA_DK = MIX_W // 2 // GLA_HEADS
GLA_DV = MIX_W // GLA_HEADS
GLA_QK_W = GLA_HEADS * GLA_DK
GLA_V_W = GLA_HEADS * GLA_DV
GLA_GATE_RANK = 16
GLA_GATE_NORM = 16.0
GLA_CHUNK = 64
N_EXPERTS = 32
TOP_K = 4
D_FF = D_MODEL
SWIGLU_LIMIT = 7.0
SWIGLU_ALPHA = 1.702
MOE_BLOCK = 128
DN_ALPHA = (2 * DEPTH) ** 0.25
DN_BETA = (8 * DEPTH) ** -0.25
LN_EPS = 1e-5

IN_SPLITS = (CONV_W, CONV_W, CONV_W,
             ATT_Q_W, ATT_KV_W, ATT_KV_W,
             GLA_QK_W, GLA_QK_W, GLA_V_W, GLA_V_W, GLA_GATE_RANK, GLA_GATE_RANK,
             D_MODEL, D_MODEL, D_MODEL)
IN_COLS = sum(IN_SPLITS)

kernel_name = "hybrid_diffusion_conv_swa_gla_moe_step"


def _split_in(proj):
    outs, o = [], 0
    for w in IN_SPLITS:
        outs.append(proj[..., o:o + w])
        o += w
    return outs


def _layernorm(x, g, b):
    xf = x.astype(jnp.float32)
    mu = jnp.mean(xf, axis=-1, keepdims=True)
    var = jnp.mean(jnp.square(xf - mu), axis=-1, keepdims=True)
    return ((xf - mu) * lax.rsqrt(var + LN_EPS) * g + b).astype(x.dtype)


def _modulation(cond, ada_w, ada_b):
    m = jax.nn.silu(cond) @ ada_w + ada_b
    return jnp.split(m[..., None, :], 6, axis=-1)


def _short_conv(u, b, cg, conv_w):
    v = cg * u
    T = v.shape[1]
    vp = jnp.pad(v, ((0, 0), (1, 1), (0, 0)))
    y = conv_w[0] * vp[:, :T] + conv_w[1] * vp[:, 1:T + 1] + conv_w[2] * vp[:, 2:T + 2]
    return b * y


def _axial_rope(x, rows):
    row = jnp.repeat(jnp.arange(rows), GRID_W)
    col = jnp.tile(jnp.arange(GRID_W), rows)
    half = HEAD_DIM // 2
    nf = half // 2
    inv = ROPE_THETA ** (-jnp.arange(nf, dtype=jnp.float32) / nf)

    def rot(xh, pos):
        ang = pos.astype(jnp.float32)[:, None] * inv
        cos = jnp.cos(ang)[None, :, None, :]
        sin = jnp.sin(ang)[None, :, None, :]
        x1 = xh[..., :nf].astype(jnp.float32)
        x2 = xh[..., nf:].astype(jnp.float32)
        return jnp.concatenate([x1 * cos - x2 * sin, x1 * sin + x2 * cos], axis=-1)

    return jnp.concatenate([rot(x[..., :half], row), rot(x[..., half:], col)], axis=-1).astype(x.dtype)


def _sink_softmax(s, sink):
    sk = jnp.broadcast_to(sink.astype(jnp.float32).reshape(ATT_KV_HEADS, ATT_GROUP, 1, 1), s.shape[:-1] + (1,))
    p = jax.nn.softmax(jnp.concatenate([s, sk], axis=-1), axis=-1)
    return p[..., :-1]


def _context_attention(q, k, v, sink):
    B, T = q.shape[:2]
    nb = T // ATT_BLOCK
    qb = q.reshape(B, nb, ATT_BLOCK, ATT_KV_HEADS, ATT_GROUP, HEAD_DIM).swapaxes(0, 1)

    def one(qi):
        s = jnp.einsum('bqkgd,bckd->bkgqc', qi, k).astype(jnp.float32) * ATT_SCALE
        p = _sink_softmax(s, sink)
        return jnp.einsum('bkgqc,bckd->bqkgd', p.astype(v.dtype), v)

    o = lax.map(one, qb)
    return o.swapaxes(0, 1).reshape(B, T, ATT_Q_W)


def _latent_attention(q, k, v, kc, vc, sink):
    B, T = q.shape[:2]
    nb = T // ATT_BLOCK
    kp = jnp.pad(k, ((0, 0), (ATT_BLOCK, ATT_BLOCK), (0, 0), (0, 0)))
    vp = jnp.pad(v, ((0, 0), (ATT_BLOCK, ATT_BLOCK), (0, 0), (0, 0)))
    qb = q.reshape(B, nb, ATT_BLOCK, ATT_KV_HEADS, ATT_GROUP, HEAD_DIM).swapaxes(0, 1)
    r = jnp.arange(3 * ATT_BLOCK)
    cq = jnp.arange(ATT_BLOCK)

    def one(args):
        i, qi = args
        kw = lax.dynamic_slice_in_dim(kp, i * ATT_BLOCK, 3 * ATT_BLOCK, axis=1)
        vw = lax.dynamic_slice_in_dim(vp, i * ATT_BLOCK, 3 * ATT_BLOCK, axis=1)
        kpos = i * ATT_BLOCK - ATT_BLOCK + r
        qpos = i * ATT_BLOCK + cq
        mask = (jnp.abs(kpos[None, :] - qpos[:, None]) <= WINDOW) & (kpos[None, :] >= 0) & (kpos[None, :] < T)
        s_w = jnp.einsum('bqkgd,brkd->bkgqr', qi, kw).astype(jnp.float32) * ATT_SCALE
        s_w = jnp.where(mask[None, None, None], s_w, NEG_INF)
        s_c = jnp.einsum('bqkgd,bckd->bkgqc', qi, kc).astype(jnp.float32) * ATT_SCALE
        p = _sink_softmax(jnp.concatenate([s_w, s_c], axis=-1), sink)
        p_w = p[..., :3 * ATT_BLOCK].astype(v.dtype)
        p_c = p[..., 3 * ATT_BLOCK:].astype(v.dtype)
        return (jnp.einsum('bkgqr,brkd->bqkgd', p_w, vw)
                + jnp.einsum('bkgqc,bckd->bqkgd', p_c, vc))

    o = lax.map(one, (jnp.arange(nb), qb))
    return o.swapaxes(0, 1).reshape(B, T, ATT_Q_W)


def _gla_scan(q, k, v, log_a, s0):
    B, T, H, DK = q.shape
    n = T // GLA_CHUNK
    cs = lambda a: a.reshape((B, n, GLA_CHUNK) + a.shape[2:])
    qc, kc, vc, ac = cs(q), cs(k), cs(v), cs(log_a)
    dt = q.dtype
    bcum = jnp.cumsum(ac, axis=2)
    blast = bcum[:, :, -1:]
    qe = (qc * jnp.exp(bcum)).astype(dt)
    ke = (kc * jnp.exp(-bcum)).astype(dt)
    kd = (kc * jnp.exp(blast - bcum)).astype(dt)
    causal = jnp.tril(jnp.ones((GLA_CHUNK, GLA_CHUNK), dtype=bool))
    att = jnp.where(causal, jnp.einsum('bnchk,bndhk->bnhcd', qe, ke), 0)
    o_intra = jnp.einsum('bnhcd,bndhv->bnchv', att, vc)
    upd = jnp.einsum('bnchk,bnchv->bnhkv', kd, vc)
    decay = jnp.exp(blast[:, :, 0])

    def step(s, xs):
        d, u = xs
        return (d[..., None] * s + u).astype(s.dtype), s

    s_fin, s_start = lax.scan(step, s0, (jnp.moveaxis(decay, 1, 0).astype(s0.dtype), jnp.moveaxis(upd, 1, 0)))
    s_start = jnp.moveaxis(s_start, 0, 1).astype(dt)
    o_inter = jnp.einsum('bnchk,bnhkv->bnchv', qe, s_start)
    return (o_intra + o_inter).reshape(B, T, H, v.shape[-1]), s_fin


def _gla_branch(q, k, v, r, za_f, za_b, wa2, ba2, norm_g, s0f, s0b):
    B, T = q.shape[:2]
    q = q.reshape(B, T, GLA_HEADS, GLA_DK) * (GLA_DK ** -0.5)
    k = k.reshape(B, T, GLA_HEADS, GLA_DK)
    v = v.reshape(B, T, GLA_HEADS, GLA_DV)
    la_f = (jax.nn.log_sigmoid((za_f @ wa2[0] + ba2[0]).astype(jnp.float32)) / GLA_GATE_NORM).reshape(B, T, GLA_HEADS, GLA_DK)
    la_b = (jax.nn.log_sigmoid((za_b @ wa2[1] + ba2[1]).astype(jnp.float32)) / GLA_GATE_NORM).reshape(B, T, GLA_HEADS, GLA_DK)
    o_f, s_f = _gla_scan(q, k, v, la_f, s0f)
    o_b, s_b = _gla_scan(q[:, ::-1], k[:, ::-1], v[:, ::-1], la_b[:, ::-1], s0b)
    of = (o_f + o_b[:, ::-1]).astype(jnp.float32)
    mu = jnp.mean(of, axis=-1, keepdims=True)
    var = jnp.mean(jnp.square(of - mu), axis=-1, keepdims=True)
    on = ((of - mu) * lax.rsqrt(var + LN_EPS) * norm_g).reshape(B, T, GLA_V_W).astype(r.dtype)
    return on * jax.nn.silu(r), s_f, s_b


def _moe(h, router_w, router_b, w_gu, b_gu, w_dn, b_dn):
    B, T, D = h.shape
    xt = h.reshape(-1, D)
    n_tok = xt.shape[0]
    logits = (xt @ router_w + router_b).astype(jnp.float32)
    top_val, top_idx = lax.top_k(logits, TOP_K)
    gate = jax.nn.softmax(top_val, axis=-1)
    n_assign = n_tok * TOP_K
    flat_e = top_idx.reshape(-1)
    order = jnp.argsort(flat_e)
    e_sorted = flat_e[order]
    tok_sorted = order // TOP_K
    counts = jnp.bincount(flat_e, length=N_EXPERTS)
    padded = (counts + MOE_BLOCK - 1) // MOE_BLOCK * MOE_BLOCK
    pend = jnp.cumsum(padded)
    pstart = pend - padded
    start = jnp.cumsum(counts) - counts
    dest = pstart[e_sorted] + jnp.arange(n_assign) - start[e_sorted]
    n_blocks = -(-n_assign // MOE_BLOCK) + N_EXPERTS
    rows = jnp.zeros((n_blocks * MOE_BLOCK, D), h.dtype).at[dest].set(xt[tok_sorted])
    blk_e = jnp.minimum(jnp.searchsorted(pend, jnp.arange(n_blocks) * MOE_BLOCK, side='right'), N_EXPERTS - 1)

    def expert_block(args):
        xb, e = args
        gu = xb @ w_gu[e] + b_gu[e]
        g = jnp.minimum(gu[:, :D_FF], SWIGLU_LIMIT)
        u = jnp.clip(gu[:, D_FF:], -SWIGLU_LIMIT, SWIGLU_LIMIT)
        act = (u + 1) * (g * jax.nn.sigmoid(SWIGLU_ALPHA * g))
        return act @ w_dn[e] + b_dn[e]

    y_rows = lax.map(expert_block, (rows.reshape(n_blocks, MOE_BLOCK, D), blk_e)).reshape(-1, D)
    w_sorted = gate.reshape(-1)[order].astype(h.dtype)
    out = jax.ops.segment_sum(y_rows[dest] * w_sorted[:, None], tok_sorted, num_segments=n_tok)
    return out.reshape(B, T, D)


def _layer(x, mod, lw, ctx):
    (w_in, conv_w, sink, wa2, ba2, gla_g, w_branch, w_out,
     ln1_g, ln1_b, ln2_g, ln2_b, r_w, r_b, w_gu, b_gu, w_dn, b_dn) = lw
    sh1, sc1, g1, sh2, sc2, g2 = mod
    B, T, _ = x.shape
    h = x * (1 + sc1) + sh1
    (u_a, b_a, c_a, q_b, k_b, v_b, q_c, k_c, v_c, r_c, za_f, za_b,
     gt_a, gt_b, gt_c) = _split_in(h @ w_in)
    y_a = _short_conv(u_a, b_a, c_a, conv_w)
    q_b = q_b.reshape(B, T, ATT_HEADS, HEAD_DIM)
    k_b = k_b.reshape(B, T, ATT_KV_HEADS, HEAD_DIM)
    v_b = v_b.reshape(B, T, ATT_KV_HEADS, HEAD_DIM)
    if ctx is None:
        y_b = _context_attention(q_b, k_b, v_b, sink)
        s0f = jnp.zeros((B, GLA_HEADS, GLA_DK, GLA_DV), x.dtype)
        s0b = jnp.zeros((B, GLA_HEADS, GLA_DK, GLA_DV), x.dtype)
    else:
        kc, vc, s0f, s0b, rows = ctx
        q_b = _axial_rope(q_b, rows)
        k_b = _axial_rope(k_b, rows)
        y_b = _latent_attention(q_b, k_b, v_b, kc, vc, sink)
    y_c, s_f, s_b = _gla_branch(q_c, k_c, v_c, r_c, za_f, za_b, wa2, ba2, gla_g, s0f, s0b)
    merged = (jax.nn.sigmoid(gt_a) * (y_a @ w_branch[0])
              + jax.nn.sigmoid(gt_b) * (y_b @ w_branch[1])
              + jax.nn.sigmoid(gt_c) * (y_c @ w_branch[2]))
    x = _layernorm(DN_ALPHA * x + g1 * (merged @ w_out), ln1_g, ln1_b)
    h2 = x * (1 + sc2) + sh2
    x = _layernorm(DN_ALPHA * x + g2 * _moe(h2, r_w, r_b, w_gu, b_gu, w_dn, b_dn), ln2_g, ln2_b)
    return x, (k_b, v_b, s_f, s_b)


def setup_inputs(seed: int = 0) -> dict:
    key = jax.random.key(seed)
    ks = jax.random.split(key, 32)
    nrm = lambda k, shape, s: jax.random.normal(k, shape, jnp.float32) * s
    L, D = DEPTH, D_MODEL
    return {
        "x_prompt": nrm(ks[0], (BATCH, SEQ, D), 1.0),
        "x_sample": nrm(ks[1], (DEC_BATCH, DEC_SEQ, D), 1.0),
        "cache_k": nrm(ks[2], (DEC_BATCH, DEPTH, PAST_LEN, ATT_KV_HEADS, HEAD_DIM), 1.0),
        "cache_v": nrm(ks[3], (DEC_BATCH, DEPTH, PAST_LEN, ATT_KV_HEADS, HEAD_DIM), 1.0),
        "state_gla": nrm(ks[4], (DEC_BATCH, DEPTH, 2, GLA_HEADS, GLA_DK, GLA_DV), 1.0),
        "c": nrm(ks[5], (DEC_BATCH, D), 1.0),
        "c_ctx": nrm(ks[6], (D,), 1.0),
        "ada_w": nrm(ks[7], (L, D, 6 * D), 0.5 * D ** -0.5),
        "ada_b": nrm(ks[8], (L, 6 * D), 0.02),
        "w_in": nrm(ks[9], (L, D, IN_COLS), D ** -0.5),
        "conv_w": nrm(ks[10], (L, CONV_K, CONV_W), CONV_K ** -0.5),
        "attn_sink": nrm(ks[11], (L, ATT_HEADS), 0.5),
        "gla_wa2": nrm(ks[12], (L, 2, GLA_GATE_RANK, GLA_QK_W), GLA_GATE_RANK ** -0.5),
        "gla_ba2": nrm(ks[13], (L, 2, GLA_QK_W), 0.1),
        "gla_norm_g": 1.0 + nrm(ks[14], (L, GLA_DV), 0.02),
        "w_branch": nrm(ks[15], (L, 3, MIX_W, D), DN_BETA * MIX_W ** -0.5),
        "w_out": nrm(ks[16], (L, D, D), DN_BETA * D ** -0.5),
        "ln1_g": 1.0 + nrm(ks[17], (L, D), 0.02),
        "ln1_b": nrm(ks[18], (L, D), 0.02),
        "ln2_g": 1.0 + nrm(ks[19], (L, D), 0.02),
        "ln2_b": nrm(ks[20], (L, D), 0.02),
        "router_w": nrm(ks[21], (L, D, N_EXPERTS), D ** -0.5),
        "router_b": nrm(ks[22], (L, N_EXPERTS), 0.01),
        "w_gate_up": nrm(ks[23], (L, N_EXPERTS, D, 2 * D_FF), D ** -0.5),
        "b_gate_up": nrm(ks[24], (L, N_EXPERTS, 2 * D_FF), 0.02),
        "w_down": nrm(ks[25], (L, N_EXPERTS, D_FF, D), DN_BETA * D_FF ** -0.5),
        "b_down": nrm(ks[26], (L, N_EXPERTS, D), 0.02),
    }


def reference(x_prompt, x_sample, cache_k, cache_v, state_gla, c, c_ctx, ada_w, ada_b, w_in, conv_w,
              attn_sink, gla_wa2, gla_ba2, gla_norm_g, w_branch, w_out, ln1_g, ln1_b, ln2_g, ln2_b,
              router_w, router_b, w_gate_up, b_gate_up, w_down, b_down):
    rows = x_sample.shape[1] // GRID_W
    y_prompt, y_sample = x_prompt, x_sample
    ks, vs, ss = [], [], []
    for l in range(DEPTH):
        lw = (w_in[l], conv_w[l], attn_sink[l], gla_wa2[l], gla_ba2[l], gla_norm_g[l], w_branch[l], w_out[l],
              ln1_g[l], ln1_b[l], ln2_g[l], ln2_b[l], router_w[l], router_b[l],
              w_gate_up[l], b_gate_up[l], w_down[l], b_down[l])
        y_prompt, (k_l, v_l, sf_l, sb_l) = _layer(y_prompt, _modulation(c_ctx, ada_w[l], ada_b[l]), lw, None)
        ks.append(k_l)
        vs.append(v_l)
        ss.append(jnp.stack([sf_l, sb_l], axis=1))
        ctx = (cache_k[:, l], cache_v[:, l], state_gla[:, l, 0], state_gla[:, l, 1], rows)
        y_sample, _ = _layer(y_sample, _modulation(c, ada_w[l], ada_b[l]), lw, ctx)
    new_k = jnp.stack(ks, axis=1)
    new_v = jnp.stack(vs, axis=1)
    new_gla = jnp.stack(ss, axis=1)
    return (y_prompt, y_sample, new_k, new_v, new_gla)
```

```python
import functools

import jax
import jax.numpy as jnp
from jax import lax
from jax.experimental import pallas as pl
from jax.experimental.pallas import tpu as pltpu

F32 = jnp.float32
BF16 = jnp.bfloat16
HIGHEST = lax.Precision.HIGHEST

D_MODEL = 1024
MIX_W = D_MODEL // 2
CONV_W = MIX_W
HEAD_DIM = 64
ATT_HEADS = MIX_W // HEAD_DIM
ATT_KV_HEADS = 2
ATT_GROUP = ATT_HEADS // ATT_KV_HEADS
ATT_Q_W = ATT_HEADS * HEAD_DIM
ATT_KV_W = ATT_KV_HEADS * HEAD_DIM
WINDOW = 128
ATT_BLOCK = 128
ATT_SCALE = HEAD_DIM ** -0.5
ROPE_THETA = 10000.0
NEG_INF = -1e30
GRID_W = 64
GLA_HEADS = 4
GLA_DK = MIX_W // 2 // GLA_HEADS
GLA_DV = MIX_W // GLA_HEADS
GLA_QK_W = GLA_HEADS * GLA_DK
GLA_V_W = GLA_HEADS * GLA_DV
GLA_GATE_RANK = 16
GLA_GATE_NORM = 16.0
GLA_CHUNK = 64
N_EXPERTS = 32
TOP_K = 4
D_FF = D_MODEL
SWIGLU_LIMIT = 7.0
SWIGLU_ALPHA = 1.702
LN_EPS = 1e-5

LANES = 128
SUBLANES = 8
VMEM_LIMIT_BYTES = 56 * 1024 * 1024

COL_GATES = 0
COL_CONV = 3 * D_MODEL
COL_QB = COL_CONV + 3 * CONV_W
COL_VC = COL_QB + ATT_Q_W
COL_RC = COL_VC + GLA_V_W
COL_QC = COL_RC + GLA_V_W
COL_KC = COL_QC + GLA_QK_W
COL_KB = COL_KC + GLA_QK_W
COL_VB = COL_KB + ATT_KV_W
COL_ZA = COL_VB + ATT_KV_W
P_COLS = COL_ZA + LANES

TOK_TILE = 256
MOE_ROWS = 256
COMB_TILE = 128
FF_CHUNK = 512


def _params(semantics, **kw):
    return pltpu.CompilerParams(dimension_semantics=semantics, vmem_limit_bytes=VMEM_LIMIT_BYTES, **kw)


def _sigmoid(x):
    return 1.0 / (1.0 + jnp.exp(-x))


def _layernorm_rows(z, g, b):
    mu = jnp.mean(z, axis=-1, keepdims=True)
    d = z - mu
    var = jnp.mean(d * d, axis=-1, keepdims=True)
    return d * lax.rsqrt(var + LN_EPS) * g + b


def _ada_body(c_ref, w_ref, b_ref, o_ref):
    c = c_ref[...]
    s = c * _sigmoid(c)
    o_ref[...] = jnp.dot(s, w_ref[...], precision=HIGHEST, preferred_element_type=F32) + b_ref[...]


def _modulation(cond, ada_w, ada_b):
    depth, d, n6 = ada_w.shape
    rows = cond.shape[0]
    tn = 1536
    return pl.pallas_call(
        _ada_body,
        grid=(depth, n6 // tn),
        in_specs=[
            pl.BlockSpec((rows, d), lambda l, j: (0, 0)),
            pl.BlockSpec((None, d, tn), lambda l, j: (l, 0, j)),
            pl.BlockSpec((None, 1, tn), lambda l, j: (l, 0, j)),
        ],
        out_specs=pl.BlockSpec((None, rows, tn), lambda l, j: (l, 0, j)),
        out_shape=jax.ShapeDtypeStruct((depth, rows, n6), F32),
        compiler_params=_params(("arbitrary", "arbitrary")),
        name="modulation",
    )(cond, ada_w, ada_b.reshape(depth, 1, n6))


class _Geom:
    def __init__(self, n_ctx, t_ctx, n_lat, t_lat):
        self.n_ctx, self.t_ctx, self.n_lat, self.t_lat = n_ctx, t_ctx, n_lat, t_lat
        self.tok_ctx = n_ctx * t_ctx
        self.tok_lat = n_lat * t_lat
        self.n_tok = self.tok_ctx + self.tok_lat
        self.ctx_row = n_lat
        assert t_ctx % TOK_TILE == 0 and t_lat % TOK_TILE == 0 and self.tok_ctx % t_lat == 0

    def seq_of_tile(self, i, tile):
        ctx_tiles = self.tok_ctx // tile
        return jnp.where(i < ctx_tiles, self.ctx_row, (i - ctx_tiles) // (self.t_lat // tile))


def _mod_spec(geom, tile, which):
    return pl.BlockSpec((None, 1, D_MODEL), lambda i, *_: (geom.seq_of_tile(i, tile), 0, which))


IN_CHUNK = 1408


def _inproj_body(x_ref, sc_ref, sh_ref, w_ref, o_ref):
    h = (x_ref[...] * (1.0 + sc_ref[...]) + sh_ref[...]).astype(BF16)
    for c0 in range(0, P_COLS, IN_CHUNK):
        o_ref[:, c0:c0 + IN_CHUNK] = jnp.dot(h, w_ref[:, c0:c0 + IN_CHUNK], preferred_element_type=F32)


def _in_proj(geom, x, mod, w_in_bf):
    tm = TOK_TILE
    return pl.pallas_call(
        _inproj_body,
        grid=(geom.n_tok // tm,),
        in_specs=[
            pl.BlockSpec((tm, D_MODEL), lambda i: (i, 0)),
            _mod_spec(geom, tm, 1),
            _mod_spec(geom, tm, 0),
            pl.BlockSpec((D_MODEL, P_COLS), lambda i: (0, 0), pipeline_mode=pl.Buffered(1)),
        ],
        out_specs=pl.BlockSpec((tm, P_COLS), lambda i: (i, 0)),
        out_shape=jax.ShapeDtypeStruct((geom.n_tok, P_COLS), F32),
        compiler_params=_params(("arbitrary",)),
        name="in_proj",
    )(x, mod, mod, w_in_bf)


def _sink_column(sink_ref, g, rows):
    return jnp.concatenate(
        [jnp.full((rows, 1), sink_ref[g * ATT_GROUP + h], F32) for h in range(ATT_GROUP)], axis=0)


def _stack_heads(q, g):
    return jnp.concatenate(
        [q[:, (g * ATT_GROUP + h) * HEAD_DIM:(g * ATT_GROUP + h + 1) * HEAD_DIM] for h in range(ATT_GROUP)],
        axis=0)


def _softmax_pv(score_blocks, value_blocks, sink_col):
    m = sink_col
    for s in score_blocks:
        m = jnp.maximum(m, jnp.max(s, axis=-1, keepdims=True))
    den = jnp.exp(sink_col - m)
    acc = None
    for s, v in zip(score_blocks, value_blocks):
        p = jnp.exp(s - m)
        den = den + jnp.sum(p, axis=-1, keepdims=True)
        pv = jnp.dot(p.astype(BF16), v, preferred_element_type=F32)
        acc = pv if acc is None else acc + pv
    return acc / den


def _nt_dot(a, b):
    return lax.dot_general(a, b, (((1,), (1,)), ((), ())), preferred_element_type=F32)


def _attn_ctx_body(q_ref, k_ref, v_ref, sink_ref, o_ref):
    t = q_ref.shape[0]
    q = q_ref[...] * ATT_SCALE
    k = k_ref[...].astype(BF16)
    v = v_ref[...].astype(BF16)
    outs = []
    for g in range(ATT_KV_HEADS):
        kg = k[:, g * HEAD_DIM:(g + 1) * HEAD_DIM]
        vg = v[:, g * HEAD_DIM:(g + 1) * HEAD_DIM]
        qs = _stack_heads(q, g).astype(BF16)
        o = _softmax_pv([_nt_dot(qs, kg)], [vg], _sink_column(sink_ref, g, t))
        outs += [o[h * t:(h + 1) * t] for h in range(ATT_GROUP)]
    o_ref[...] = jnp.concatenate(outs, axis=1)


def _attn_ctx(geom, p, sink):
    t = geom.t_ctx
    return pl.pallas_call(
        _attn_ctx_body,
        grid=(geom.n_ctx,),
        in_specs=[
            pl.BlockSpec((t, ATT_Q_W), lambda b: (b, COL_QB // ATT_Q_W)),
            pl.BlockSpec((t, ATT_KV_W), lambda b: (b, COL_KB // ATT_KV_W)),
            pl.BlockSpec((t, ATT_KV_W), lambda b: (b, COL_VB // ATT_KV_W)),
            pl.BlockSpec(memory_space=pltpu.SMEM),
        ],
        out_specs=pl.BlockSpec((t, ATT_Q_W), lambda b: (b, 0)),
        out_shape=jax.ShapeDtypeStruct((geom.tok_ctx, ATT_Q_W), F32),
        compiler_params=_params(("arbitrary",)),
        name="attn_ctx",
    )(p, p, p, sink)


def _rope(x, cos, sin_lo, sin_hi):
    nf = HEAD_DIM // 4
    outs = []
    for j in range(x.shape[1] // LANES):
        xj = x[:, j * LANES:(j + 1) * LANES]
        outs.append(xj * cos + pltpu.roll(xj, LANES - nf, 1) * sin_lo + pltpu.roll(xj, nf, 1) * sin_hi)
    return outs[0] if len(outs) == 1 else jnp.concatenate(outs, axis=1)


def _attn_lat_body(q_ref, k_ref, v_ref, ck_ref, cv_ref, qcos_ref, qslo_ref, qshi_ref,
                   kcos_ref, kslo_ref, kshi_ref, sink_ref, o_ref):
    i = pl.program_id(1)
    nb = k_ref.shape[0] // ATT_BLOCK
    blk = ATT_BLOCK
    q = _rope(q_ref[...], qcos_ref[...], qslo_ref[...], qshi_ref[...]) * ATT_SCALE
    q_in_blk = lax.broadcasted_iota(jnp.int32, (ATT_GROUP * blk, blk), 0) % blk
    k_in_blk = lax.broadcasted_iota(jnp.int32, (ATT_GROUP * blk, blk), 1)
    k_win, v_win, masks = [], [], []
    for j in (-1, 0, 1):
        kb = i + j
        kbc = jnp.clip(kb, 0, nb - 1)
        rows = pl.ds(pl.multiple_of(kbc * blk, blk), blk)
        k_win.append(_rope(k_ref[rows, :], kcos_ref[rows, :], kslo_ref[rows, :], kshi_ref[rows, :]).astype(BF16))
        v_win.append(v_ref[rows, :].astype(BF16))
        outside = jnp.where((kb >= 0) & (kb < nb), 0, 4 * WINDOW)
        dist = (j * blk + k_in_blk) - q_in_blk
        masks.append(jnp.abs(dist) + outside <= WINDOW)
    ck = ck_ref[...].astype(BF16)
    cv = cv_ref[...].astype(BF16)
    outs = []
    for g in range(ATT_KV_HEADS):
        hs = slice(g * HEAD_DIM, (g + 1) * HEAD_DIM)
        qs = _stack_heads(q, g).astype(BF16)
        scores = [jnp.where(mk, _nt_dot(qs, kw[:, hs]), NEG_INF) for kw, mk in zip(k_win, masks)]
        scores.append(_nt_dot(qs, ck[:, hs]))
        values = [vw[:, hs] for vw in v_win] + [cv[:, hs]]
        o = _softmax_pv(scores, values, _sink_column(sink_ref, g, blk))
        outs += [o[h * blk:(h + 1) * blk] for h in range(ATT_GROUP)]
    o_ref[...] = jnp.concatenate(outs, axis=1)


def _attn_lat(geom, p, cache_k, cache_v, rope_tabs, sink):
    t, blk = geom.t_lat, ATT_BLOCK
    nb = t // blk
    past = cache_k.shape[1]
    q_row0 = geom.tok_ctx // blk
    s_row0 = geom.tok_ctx // t
    cos, slo, shi = rope_tabs
    q_tab = pl.BlockSpec((blk, LANES), lambda b, i: (i, 0))
    k_tab = pl.BlockSpec((t, LANES), lambda b, i: (0, 0))
    return pl.pallas_call(
        _attn_lat_body,
        grid=(geom.n_lat, nb),
        in_specs=[
            pl.BlockSpec((blk, ATT_Q_W), lambda b, i: (q_row0 + b * nb + i, COL_QB // ATT_Q_W)),
            pl.BlockSpec((t, ATT_KV_W), lambda b, i: (s_row0 + b, COL_KB // ATT_KV_W)),
            pl.BlockSpec((t, ATT_KV_W), lambda b, i: (s_row0 + b, COL_VB // ATT_KV_W)),
            pl.BlockSpec((None, past, ATT_KV_W), lambda b, i: (b, 0, 0)),
            pl.BlockSpec((None, past, ATT_KV_W), lambda b, i: (b, 0, 0)),
            q_tab, q_tab, q_tab, k_tab, k_tab, k_tab,
            pl.BlockSpec(memory_space=pltpu.SMEM),
        ],
        out_specs=pl.BlockSpec((blk, ATT_Q_W), lambda b, i: (b * nb + i, 0)),
        out_shape=jax.ShapeDtypeStruct((geom.tok_lat, ATT_Q_W), F32),
        compiler_params=_params(("arbitrary", "arbitrary")),
        name="attn_lat",
    )(p, p, p, cache_k, cache_v, cos, slo, shi, cos, slo, shi, sink)


def _rope_tables(t_lat):
    half = HEAD_DIM // 2
    nf = half // 2
    tok = jnp.arange(t_lat)
    lane = jnp.arange(LANES) % HEAD_DIM
    pos = jnp.where(lane[None, :] < half, (tok // GRID_W)[:, None], (tok % GRID_W)[:, None]).astype(F32)
    inv = ROPE_THETA ** (-(lane % nf).astype(F32) / nf)
    ang = pos * inv[None, :]
    first = (lane % half) < nf
    cos = jnp.cos(ang)
    sin = jnp.sin(ang)
    sin_lo = jnp.where(first[None, :], -sin, 0.0)
    sin_hi = jnp.where(first[None, :], 0.0, sin)
    return cos, sin_lo, sin_hi


def _gla_body(*refs, has_s0):
    if has_s0:
        (q_ref, k_ref, v_ref, r_ref, za_ref, w2_ref, b2_ref, g_ref, s0_ref,
         y_ref, sfin_ref, la_scr, o_scr, st_scr) = refs
    else:
        (q_ref, k_ref, v_ref, r_ref, za_ref, w2_ref, b2_ref, g_ref,
         y_ref, sfin_ref, la_scr, o_scr, st_scr) = refs
    t = q_ref.shape[0]
    c = GLA_CHUNK
    n = t // c
    z = jnp.dot(za_ref[...].astype(BF16), w2_ref[...], preferred_element_type=F32) + b2_ref[...]
    la_scr[...] = (jnp.minimum(z, 0.0) - jnp.log1p(jnp.exp(-jnp.abs(z)))) * (1.0 / GLA_GATE_NORM)

    ri = lax.broadcasted_iota(jnp.int32, (c, c), 0)
    ci = lax.broadcasted_iota(jnp.int32, (c, c), 1)

    for direction in (0, 1):
        keep = (ri >= ci) if direction == 0 else (ri <= ci)
        cum_mat = keep.astype(F32)
        if has_s0:
            st_scr[...] = s0_ref[direction]
        else:
            st_scr[...] = jnp.zeros_like(st_scr)

        def chunk(step, carry, direction=direction, keep=keep, cum_mat=cum_mat):
            ch = step if direction == 0 else n - 1 - step
            rows = pl.ds(pl.multiple_of(ch * c, c), c)
            la = la_scr[rows, direction * GLA_QK_W:(direction + 1) * GLA_QK_W]
            bc = jnp.dot(cum_mat, la, precision=HIGHEST, preferred_element_type=F32)
            last = bc[c - 1:c, :] if direction == 0 else bc[0:1, :]
            q = q_ref[rows, :] * (GLA_DK ** -0.5)
            k = k_ref[rows, :]
            v = v_ref[rows, :]
            qe = (q * jnp.exp(bc)).astype(BF16)
            ke = (k * jnp.exp(-bc)).astype(BF16)
            kd = (k * jnp.exp(last - bc)).astype(BF16)
            vt = v.T
            st = st_scr[...]
            o_heads, upd_heads = [], []
            for h in range(GLA_HEADS):
                ks = slice(h * GLA_DK, (h + 1) * GLA_DK)
                vs = slice(h * GLA_DV, (h + 1) * GLA_DV)
                att = jnp.where(keep, _nt_dot(qe[:, ks], ke[:, ks]), 0.0)
                o_h = jnp.dot(att.astype(BF16), v[:, vs].astype(BF16), preferred_element_type=F32)
                o_h = o_h + _nt_dot(qe[:, ks], st[:, ks].astype(BF16))
                o_heads.append(o_h)
                upd_heads.append(jnp.dot(vt[vs, :].astype(BF16), kd[:, ks], preferred_element_type=F32))
            o_c = jnp.concatenate(o_heads, axis=1)
            if direction == 0:
                o_scr[rows, :] = o_c
            else:
                o_scr[rows, :] = o_scr[rows, :] + o_c
            st_scr[...] = st * jnp.exp(last) + jnp.concatenate(upd_heads, axis=1)
            return carry

        lax.fori_loop(0, n, chunk, 0)
        sfin_ref[direction] = st_scr[...]

    g = g_ref[...]
    rb = 256
    for r0 in range(0, t, rb):
        of = o_scr[r0:r0 + rb, :]
        normed = []
        for h in range(GLA_HEADS):
            oh = of[:, h * GLA_DV:(h + 1) * GLA_DV]
            mu = jnp.mean(oh, axis=-1, keepdims=True)
            d = oh - mu
            var = jnp.mean(d * d, axis=-1, keepdims=True)
            normed.append(d * lax.rsqrt(var + LN_EPS) * g)
        r = r_ref[r0:r0 + rb, :]
        y_ref[r0:r0 + rb, :] = jnp.concatenate(normed, axis=1) * (r * _sigmoid(r))


def _gla(geom, p, latent, w2, b2, norm_g, s0t):
    if latent:
        t, n_seq, row0 = geom.t_lat, geom.n_lat, geom.tok_ctx // geom.t_lat
    else:
        t, n_seq, row0 = geom.t_ctx, geom.n_ctx, 0
    has_s0 = s0t is not None

    def col(width, off):
        return pl.BlockSpec((t, width), lambda b: (row0 + b, off // width))

    in_specs = [
        col(GLA_QK_W, COL_QC), col(GLA_QK_W, COL_KC), col(GLA_V_W, COL_VC), col(GLA_V_W, COL_RC),
        col(LANES, COL_ZA),
        pl.BlockSpec((LANES, 2 * GLA_QK_W), lambda b: (0, 0)),
        pl.BlockSpec((1, 2 * GLA_QK_W), lambda b: (0, 0)),
        pl.BlockSpec((1, GLA_DV), lambda b: (0, 0)),
    ]
    args = [p, p, p, p, p, w2, b2, norm_g]
    if has_s0:
        in_specs.append(pl.BlockSpec((None, 2, GLA_DV, GLA_QK_W), lambda b: (b, 0, 0, 0)))
        args.append(s0t)
    return pl.pallas_call(
        functools.partial(_gla_body, has_s0=has_s0),
        grid=(n_seq,),
        in_specs=in_specs,
        out_specs=[
            pl.BlockSpec((t, GLA_V_W), lambda b: (b, 0)),
            pl.BlockSpec((None, 2, GLA_DV, GLA_QK_W), lambda b: (b, 0, 0, 0)),
        ],
        out_shape=[
            jax.ShapeDtypeStruct((n_seq * t, GLA_V_W), F32),
            jax.ShapeDtypeStruct((n_seq, 2, GLA_DV, GLA_QK_W), F32),
        ],
        scratch_shapes=[
            pltpu.VMEM((t, 2 * GLA_QK_W), F32),
            pltpu.VMEM((t, GLA_V_W), F32),
            pltpu.VMEM((GLA_DV, GLA_QK_W), F32),
        ],
        compiler_params=_params(("arbitrary",)),
        name="gla_lat" if latent else "gla_ctx",
    )(*args)


def _merge_body(geom, x_ref, gt_ref, cv_ref, cvp_ref, cvn_ref, yb_ref, yc_ref, wbr_ref, wout_ref, cw_ref,
                g1_ref, sc2_ref, sh2_ref, lng_ref, lnb_ref, rwh_ref, rwl_ref, rb_ref,
                x1_ref, h2_ref, tidx_ref, tgate_ref, *, dn_alpha):
    i = pl.program_id(0)
    tm = x_ref.shape[0]
    ctx_tiles = geom.tok_ctx // tm
    per_seq = jnp.where(i < ctx_tiles, geom.t_ctx // tm, geom.t_lat // tm)
    j = jnp.where(i < ctx_tiles, i, i - ctx_tiles) % per_seq
    has_prev = (j > 0).astype(F32)
    has_next = (j < per_seq - 1).astype(F32)

    cv = cv_ref[...]
    u, bg, cg = cv[:, :CONV_W], cv[:, CONV_W:2 * CONV_W], cv[:, 2 * CONV_W:]
    v = cg * u
    pv = cvp_ref[SUBLANES - 1:SUBLANES, :]
    nv = cvn_ref[0:1, :]
    v_before = has_prev * (pv[:, 2 * CONV_W:] * pv[:, :CONV_W])
    v_after = has_next * (nv[:, 2 * CONV_W:] * nv[:, :CONV_W])
    row = lax.broadcasted_iota(jnp.int32, (tm, 1), 0)
    v_prev = jnp.where(row == 0, v_before, pltpu.roll(v, 1, 0))
    v_next = jnp.where(row == tm - 1, v_after, pltpu.roll(v, tm - 1, 0))
    cw = cw_ref[...]
    y_a = bg * (cw[0:1, :] * v_prev + cw[1:2, :] * v + cw[2:3, :] * v_next)

    gt = gt_ref[...]
    merged = (_sigmoid(gt[:, :D_MODEL]) * jnp.dot(y_a.astype(BF16), wbr_ref[0], preferred_element_type=F32)
              + _sigmoid(gt[:, D_MODEL:2 * D_MODEL])
              * jnp.dot(yb_ref[...].astype(BF16), wbr_ref[1], preferred_element_type=F32)
              + _sigmoid(gt[:, 2 * D_MODEL:])
              * jnp.dot(yc_ref[...].astype(BF16), wbr_ref[2], preferred_element_type=F32))
    mix = jnp.dot(merged.astype(BF16), wout_ref[...], preferred_element_type=F32)
    x1 = _layernorm_rows(dn_alpha * x_ref[...] + g1_ref[...] * mix, lng_ref[...], lnb_ref[...])
    x1_ref[...] = x1
    h2 = x1 * (1.0 + sc2_ref[...]) + sh2_ref[...]
    h2_ref[...] = h2

    h_hi = h2.astype(BF16)
    h_lo = (h2 - h_hi.astype(F32)).astype(BF16)
    logits = (jnp.dot(h_hi, rwh_ref[...], preferred_element_type=F32)
              + jnp.dot(h_lo, rwh_ref[...], preferred_element_type=F32)
              + jnp.dot(h_hi, rwl_ref[...], preferred_element_type=F32)) + rb_ref[...]
    lane = lax.broadcasted_iota(jnp.int32, logits.shape, 1).astype(F32)
    vals, top_v, top_i = logits, [], []
    for _ in range(TOP_K):
        m = jnp.max(vals, axis=-1, keepdims=True)
        am = jnp.min(jnp.where(vals == m, lane, float(N_EXPERTS)), axis=-1, keepdims=True)
        top_v.append(m)
        top_i.append(am)
        vals = jnp.where(lane == am, -jnp.inf, vals)
    ex = [jnp.exp(tv - top_v[0]) for tv in top_v]
    den = ex[0] + ex[1] + ex[2] + ex[3]
    slot = lax.broadcasted_iota(jnp.int32, (tm, TOP_K), 1)
    gates, idxs = ex[TOP_K - 1], top_i[TOP_K - 1]
    for k in range(TOP_K - 2, -1, -1):
        gates = jnp.where(slot == k, ex[k], gates)
        idxs = jnp.where(slot == k, top_i[k], idxs)
    tgate_ref[...] = gates / den
    tidx_ref[...] = idxs.astype(jnp.int32)


def _merge(geom, x, p, yb, yc, mod, lw, dn_alpha):
    tm = TOK_TILE
    n = geom.n_tok
    halo_per_tile = tm // SUBLANES
    n_halo = n // SUBLANES
    conv_blk = COL_CONV // (3 * CONV_W)
    const2 = lambda i: (0, 0)
    return pl.pallas_call(
        functools.partial(_merge_body, geom, dn_alpha=dn_alpha),
        grid=(n // tm,),
        in_specs=[
            pl.BlockSpec((tm, D_MODEL), lambda i: (i, 0)),
            pl.BlockSpec((tm, 3 * D_MODEL), lambda i: (i, 0)),
            pl.BlockSpec((tm, 3 * CONV_W), lambda i: (i, conv_blk)),
            pl.BlockSpec((SUBLANES, 3 * CONV_W), lambda i: (jnp.maximum(i * halo_per_tile - 1, 0), conv_blk)),
            pl.BlockSpec((SUBLANES, 3 * CONV_W),
                         lambda i: (jnp.minimum((i + 1) * halo_per_tile, n_halo - 1), conv_blk)),
            pl.BlockSpec((tm, MIX_W), lambda i: (i, 0)),
            pl.BlockSpec((tm, MIX_W), lambda i: (i, 0)),
            pl.BlockSpec((3, MIX_W, D_MODEL), lambda i: (0, 0, 0)),
            pl.BlockSpec((D_MODEL, D_MODEL), const2),
            pl.BlockSpec((SUBLANES, CONV_W), const2),
            _mod_spec(geom, tm, 2), _mod_spec(geom, tm, 4), _mod_spec(geom, tm, 3),
            pl.BlockSpec((1, D_MODEL), const2), pl.BlockSpec((1, D_MODEL), const2),
            pl.BlockSpec((D_MODEL, N_EXPERTS), const2), pl.BlockSpec((D_MODEL, N_EXPERTS), const2),
            pl.BlockSpec((1, N_EXPERTS), const2),
        ],
        out_specs=[
            pl.BlockSpec((tm, D_MODEL), lambda i: (i, 0)),
            pl.BlockSpec((tm, D_MODEL), lambda i: (i, 0)),
            pl.BlockSpec((tm, TOP_K), lambda i: (i, 0)),
            pl.BlockSpec((tm, TOP_K), lambda i: (i, 0)),
        ],
        out_shape=[
            jax.ShapeDtypeStruct((n, D_MODEL), F32),
            jax.ShapeDtypeStruct((n, D_MODEL), F32),
            jax.ShapeDtypeStruct((n, TOP_K), jnp.int32),
            jax.ShapeDtypeStruct((n, TOP_K), F32),
        ],
        compiler_params=_params(("arbitrary",)),
        name="merge",
    )(x, p, p, p, p, yb, yc, lw["w_branch"], lw["w_out"], lw["conv_w"], mod, mod, mod,
      lw["ln1_g"], lw["ln1_b"], lw["rw_hi"], lw["rw_lo"], lw["r_b"])


def _route(top_idx, bm):
    n_tok = top_idx.shape[0]
    n_assign = n_tok * TOP_K
    flat_e = top_idx.reshape(-1)
    onehot = (flat_e[:, None] == jnp.arange(N_EXPERTS, dtype=jnp.int32)[None, :]).astype(jnp.int32)
    csum = jnp.cumsum(onehot, axis=0)
    rank = jnp.sum(csum * onehot, axis=1) - 1
    counts = csum[-1]
    padded = (counts + bm - 1) // bm * bm
    pend = jnp.cumsum(padded)
    pstart = pend - padded
    dest = (pstart[flat_e] + rank).astype(jnp.int32)
    n_blocks = n_assign // bm + N_EXPERTS
    src_tok = jnp.zeros((n_blocks * bm,), jnp.int32).at[dest].set(jnp.arange(n_assign, dtype=jnp.int32) // TOP_K)
    blk_e = jnp.minimum(jnp.searchsorted(pend, jnp.arange(n_blocks, dtype=jnp.int32) * bm, side="right"),
                        N_EXPERTS - 1).astype(jnp.int32)
    n_used = (pend[-1] // bm).astype(jnp.int32).reshape(1)
    return dest, src_tok, blk_e, n_used


def _row_gather_start(src_hbm, row, dst, sem):
    pltpu.make_async_copy(src_hbm.at[pl.ds(row, 1), :], dst, sem).start()


def _expert_body(blk_e_ref, n_used_ref, tok_ref, x_hbm, wgu_ref, bgu_ref, wdn_ref, bdn_ref,
                 y_ref, xbuf, sem, wgu_bf, wdn_bf):
    i = pl.program_id(0)
    bm = y_ref.shape[0]
    n_used = n_used_ref[0]

    def issue(blk, slot):
        base = blk * bm

        def one(r, carry):
            _row_gather_start(x_hbm, tok_ref[base + r], xbuf.at[slot, pl.ds(r, 1), :], sem.at[slot])
            return carry

        lax.fori_loop(0, bm, one, 0, unroll=8)

    @pl.when(i == 0)
    def _():
        issue(0, 0)

    @pl.when(i + 1 < n_used)
    def _():
        issue(i + 1, (i + 1) % 2)

    @pl.when(i < n_used)
    def _():
        slot = i % 2
        pltpu.make_async_copy(x_hbm.at[pl.ds(0, bm), :], xbuf.at[slot], sem.at[slot]).wait()
        e = blk_e_ref[i]
        e_before = blk_e_ref[jnp.maximum(i - 1, 0)]

        @pl.when((i == 0) | (e != e_before))
        def _():
            wgu_bf[...] = wgu_ref[...].astype(BF16)
            wdn_bf[...] = wdn_ref[...].astype(BF16)

        x = xbuf[slot].astype(BF16)
        acc = jnp.zeros((bm, D_MODEL), F32)
        for f0 in range(0, D_FF, FF_CHUNK):
            g = jnp.dot(x, wgu_bf[:, f0:f0 + FF_CHUNK], preferred_element_type=F32) + bgu_ref[:, f0:f0 + FF_CHUNK]
            u = (jnp.dot(x, wgu_bf[:, D_FF + f0:D_FF + f0 + FF_CHUNK], preferred_element_type=F32)
                 + bgu_ref[:, D_FF + f0:D_FF + f0 + FF_CHUNK])
            g = jnp.minimum(g, SWIGLU_LIMIT)
            u = jnp.clip(u, -SWIGLU_LIMIT, SWIGLU_LIMIT)
            act = (u + 1.0) * (g * _sigmoid(SWIGLU_ALPHA * g))
            acc = acc + jnp.dot(act.astype(BF16), wdn_bf[f0:f0 + FF_CHUNK, :], preferred_element_type=F32)
        y_ref[...] = acc + bdn_ref[...]

    @pl.when(i >= n_used)
    def _():
        y_ref[...] = jnp.zeros_like(y_ref)


def _experts(h2, layer, src_tok, blk_e, n_used, w_gate_up, b_gate_up, w_down, b_down):
    bm = MOE_ROWS
    n_blocks = blk_e.shape[0]
    grid_spec = pltpu.PrefetchScalarGridSpec(
        num_scalar_prefetch=3,
        grid=(n_blocks,),
        in_specs=[
            pl.BlockSpec(memory_space=pl.ANY),
            pl.BlockSpec((None, None, D_MODEL, 2 * D_FF), lambda i, be, nu, tk: (layer, be[i], 0, 0)),
            pl.BlockSpec((None, None, 1, 2 * D_FF), lambda i, be, nu, tk: (layer, be[i], 0, 0)),
            pl.BlockSpec((None, None, D_FF, D_MODEL), lambda i, be, nu, tk: (layer, be[i], 0, 0)),
            pl.BlockSpec((None, None, 1, D_MODEL), lambda i, be, nu, tk: (layer, be[i], 0, 0)),
        ],
        out_specs=pl.BlockSpec((bm, D_MODEL), lambda i, be, nu, tk: (i, 0)),
        scratch_shapes=[
            pltpu.VMEM((2, bm, D_MODEL), F32),
            pltpu.SemaphoreType.DMA((2,)),
            pltpu.VMEM((D_MODEL, 2 * D_FF), BF16),
            pltpu.VMEM((D_FF, D_MODEL), BF16),
        ],
    )
    depth = w_gate_up.shape[0]
    return pl.pallas_call(
        _expert_body,
        grid_spec=grid_spec,
        out_shape=jax.ShapeDtypeStruct((n_blocks * bm, D_MODEL), F32),
        compiler_params=_params(("arbitrary",), disable_bounds_checks=True),
        name="experts",
    )(blk_e, n_used, src_tok, h2, w_gate_up, b_gate_up.reshape(depth, N_EXPERTS, 1, 2 * D_FF),
      w_down, b_down.reshape(depth, N_EXPERTS, 1, D_MODEL))


def _combine_body(geom, dest_ref, y_hbm, x1_ref, gate_ref, g2_ref, lng_ref, lnb_ref, o_ref, buf, sem,
                  *, dn_alpha):
    i = pl.program_id(0)
    n_steps = pl.num_programs(0)
    tc = o_ref.shape[0]

    def issue(step, slot):
        base = step * tc * TOP_K

        def one(r, carry):
            for k in range(TOP_K):
                _row_gather_start(y_hbm, dest_ref[base + r * TOP_K + k], buf.at[slot, k, pl.ds(r, 1), :],
                                  sem.at[slot])
            return carry

        lax.fori_loop(0, tc, one, 0, unroll=4)

    @pl.when(i == 0)
    def _():
        issue(0, 0)

    @pl.when(i + 1 < n_steps)
    def _():
        issue(i + 1, (i + 1) % 2)

    slot = i % 2
    for k in range(TOP_K):
        pltpu.make_async_copy(y_hbm.at[pl.ds(0, tc), :], buf.at[slot, k], sem.at[slot]).wait()
    gate = gate_ref[...]
    moe = gate[:, 0:1] * buf[slot, 0]
    for k in range(1, TOP_K):
        moe = moe + gate[:, k:k + 1] * buf[slot, k]
    z = dn_alpha * x1_ref[...] + g2_ref[...] * moe
    o_ref[...] = _layernorm_rows(z, lng_ref[...], lnb_ref[...])


def _combine(geom, y_rows, dest, x1, top_gate, mod, ln_g, ln_b, dn_alpha):
    tc = COMB_TILE
    n = geom.n_tok
    grid_spec = pltpu.PrefetchScalarGridSpec(
        num_scalar_prefetch=1,
        grid=(n // tc,),
        in_specs=[
            pl.BlockSpec(memory_space=pl.ANY),
            pl.BlockSpec((tc, D_MODEL), lambda i, d: (i, 0)),
            pl.BlockSpec((tc, TOP_K), lambda i, d: (i, 0)),
            _mod_spec(geom, tc, 5),
            pl.BlockSpec((1, D_MODEL), lambda i, d: (0, 0)),
            pl.BlockSpec((1, D_MODEL), lambda i, d: (0, 0)),
        ],
        out_specs=pl.BlockSpec((tc, D_MODEL), lambda i, d: (i, 0)),
        scratch_shapes=[
            pltpu.VMEM((2, TOP_K, tc, D_MODEL), F32),
            pltpu.SemaphoreType.DMA((2,)),
        ],
    )
    return pl.pallas_call(
        functools.partial(_combine_body, geom, dn_alpha=dn_alpha),
        grid_spec=grid_spec,
        out_shape=jax.ShapeDtypeStruct((n, D_MODEL), F32),
        compiler_params=_params(("arbitrary",), disable_bounds_checks=True),
        name="combine",
    )(dest, y_rows, x1, top_gate, mod, ln_g, ln_b)


def _rearrange_w_in(w_in):
    splits = (CONV_W, CONV_W, CONV_W, ATT_Q_W, ATT_KV_W, ATT_KV_W, GLA_QK_W, GLA_QK_W, GLA_V_W, GLA_V_W,
              GLA_GATE_RANK, GLA_GATE_RANK, D_MODEL, D_MODEL, D_MODEL)
    parts, o = [], 0
    for w in splits:
        parts.append(w_in[..., o:o + w])
        o += w
    u_a, b_a, c_a, q_b, k_b, v_b, q_c, k_c, v_c, r_c, za_f, za_b, gt_a, gt_b, gt_c = parts
    pad = jnp.zeros(w_in.shape[:-1] + (LANES - 2 * GLA_GATE_RANK,), w_in.dtype)
    return jnp.concatenate([gt_a, gt_b, gt_c, u_a, b_a, c_a, q_b, v_c, r_c, q_c, k_c, k_b, v_b, za_f, za_b, pad],
                           axis=-1).astype(BF16)


def kernel(x_prompt, x_sample, cache_k, cache_v, state_gla, c, c_ctx, ada_w, ada_b, w_in, conv_w, attn_sink,
           gla_wa2, gla_ba2, gla_norm_g, w_branch, w_out, ln1_g, ln1_b, ln2_g, ln2_b, router_w, router_b,
           w_gate_up, b_gate_up, w_down, b_down):
    depth = w_in.shape[0]
    n_ctx, t_ctx, _ = x_prompt.shape
    n_lat, t_lat, _ = x_sample.shape
    geom = _Geom(n_ctx, t_ctx, n_lat, t_lat)
    dn_alpha = (2 * depth) ** 0.25

    mod_rows = -(-(n_lat + 1) // SUBLANES) * SUBLANES
    cond = jnp.zeros((mod_rows, D_MODEL), F32).at[:n_lat].set(c).at[n_lat].set(c_ctx)
    mods = _modulation(cond, ada_w, ada_b).reshape(depth, mod_rows, 1, 6 * D_MODEL)

    w_in_bf = _rearrange_w_in(w_in)
    w_branch_bf = w_branch.astype(BF16)
    w_out_bf = w_out.astype(BF16)
    conv_w_pad = jnp.zeros((depth, SUBLANES, CONV_W), F32).at[:, :conv_w.shape[1]].set(conv_w)
    rw_hi = router_w.astype(BF16)
    rw_lo = (router_w - rw_hi.astype(F32)).astype(BF16)
    w2 = jnp.zeros((depth, LANES, 2 * GLA_QK_W), F32)
    w2 = w2.at[:, :GLA_GATE_RANK, :GLA_QK_W].set(gla_wa2[:, 0])
    w2 = w2.at[:, GLA_GATE_RANK:2 * GLA_GATE_RANK, GLA_QK_W:].set(gla_wa2[:, 1]).astype(BF16)
    b2 = gla_ba2.reshape(depth, 1, 2 * GLA_QK_W)
    rope_tabs = _rope_tables(t_lat)
    s0t = jnp.transpose(state_gla, (0, 1, 2, 5, 3, 4)).reshape(n_lat, depth, 2, GLA_DV, GLA_QK_W)
    past = cache_k.shape[2]
    ck = cache_k.reshape(n_lat, depth, past, ATT_KV_W)
    cv = cache_v.reshape(n_lat, depth, past, ATT_KV_W)

    x = jnp.concatenate([x_prompt.reshape(-1, D_MODEL), x_sample.reshape(-1, D_MODEL)], axis=0)
    new_k, new_v, new_s = [], [], []
    for l in range(depth):
        mod = mods[l]
        lw = dict(w_branch=w_branch_bf[l], w_out=w_out_bf[l], conv_w=conv_w_pad[l],
                  ln1_g=ln1_g[l].reshape(1, -1), ln1_b=ln1_b[l].reshape(1, -1),
                  rw_hi=rw_hi[l], rw_lo=rw_lo[l], r_b=router_b[l].reshape(1, -1))
        p = _in_proj(geom, x, mod, w_in_bf[l])
        new_k.append(p[:geom.tok_ctx, COL_KB:COL_KB + ATT_KV_W].reshape(n_ctx, t_ctx, ATT_KV_HEADS, HEAD_DIM))
        new_v.append(p[:geom.tok_ctx, COL_VB:COL_VB + ATT_KV_W].reshape(n_ctx, t_ctx, ATT_KV_HEADS, HEAD_DIM))
        yb = jnp.concatenate([_attn_ctx(geom, p, attn_sink[l]),
                              _attn_lat(geom, p, ck[:, l], cv[:, l], rope_tabs, attn_sink[l])], axis=0)
        g_norm = gla_norm_g[l].reshape(1, GLA_DV)
        yc_ctx, s_ctx = _gla(geom, p, False, w2[l], b2[l], g_norm, None)
        yc_lat, _ = _gla(geom, p, True, w2[l], b2[l], g_norm, s0t[:, l])
        yc = jnp.concatenate([yc_ctx, yc_lat], axis=0)
        new_s.append(jnp.transpose(s_ctx.reshape(n_ctx, 2, GLA_DV, GLA_HEADS, GLA_DK), (0, 1, 3, 4, 2)))
        x1, h2, top_idx, top_gate = _merge(geom, x, p, yb, yc, mod, lw, dn_alpha)
        dest, src_tok, blk_e, n_used = _route(top_idx, MOE_ROWS)
        y_rows = _experts(h2, l, src_tok, blk_e, n_used, w_gate_up, b_gate_up, w_down, b_down)
        x = _combine(geom, y_rows, dest, x1, top_gate, mod, ln2_g[l].reshape(1, -1), ln2_b[l].reshape(1, -1),
                     dn_alpha)
    y_prompt = x[:geom.tok_ctx].reshape(n_ctx, t_ctx, D_MODEL)
    y_sample = x[geom.tok_ctx:].reshape(n_lat, t_lat, D_MODEL)
    return (y_prompt, y_sample, jnp.stack(new_k, axis=1), jnp.stack(new_v, axis=1), jnp.stack(new_s, axis=1))
```

```python
import functools

import jax
import jax.numpy as jnp
from jax import lax
from jax.experimental import pallas as pl
from jax.experimental.pallas import tpu as pltpu

F32 = jnp.float32
BF16 = jnp.bfloat16
HIGHEST = lax.Precision.HIGHEST

D_MODEL = 1024
MIX_W = D_MODEL // 2
CONV_W = MIX_W
HEAD_DIM = 64
ATT_HEADS = MIX_W // HEAD_DIM
ATT_KV_HEADS = 2
ATT_GROUP = ATT_HEADS // ATT_KV_HEADS
ATT_Q_W = ATT_HEADS * HEAD_DIM
ATT_KV_W = ATT_KV_HEADS * HEAD_DIM
WINDOW = 128
ATT_BLOCK = 128
ATT_SCALE = HEAD_DIM ** -0.5
ROPE_THETA = 10000.0
NEG_INF = -1e30
GRID_W = 64
GLA_HEADS = 4
GLA_DK = MIX_W // 2 // GLA_HEADS
GLA_DV = MIX_W // GLA_HEADS
GLA_QK_W = GLA_HEADS * GLA_DK
GLA_V_W = GLA_HEADS * GLA_DV
GLA_GATE_RANK = 16
GLA_GATE_NORM = 16.0
GLA_CHUNK = 64
N_EXPERTS = 32
TOP_K = 4
D_FF = D_MODEL
SWIGLU_LIMIT = 7.0
SWIGLU_ALPHA = 1.702
LN_EPS = 1e-5

LANES = 128
SUBLANES = 8
VMEM_LIMIT_BYTES = 56 * 1024 * 1024

COL_GATES = 0
COL_CONV = 3 * D_MODEL
COL_QB = COL_CONV + 3 * CONV_W
COL_VC = COL_QB + ATT_Q_W
COL_RC = COL_VC + GLA_V_W
COL_QC = COL_RC + GLA_V_W
COL_KC = COL_QC + GLA_QK_W
COL_KB = COL_KC + GLA_QK_W
COL_VB = COL_KB + ATT_KV_W
COL_ZA = COL_VB + ATT_KV_W
P_COLS = COL_ZA + LANES

TOK_TILE = 256
MOE_ROWS = 256
COMB_TILE = 256
DISPATCH_CHUNK = 1024
FF_CHUNK = 512

ROW_TILE = D_MODEL // LANES


def _params(semantics, **kw):
    return pltpu.CompilerParams(dimension_semantics=semantics, vmem_limit_bytes=VMEM_LIMIT_BYTES, **kw)


def _sigmoid(x):
    return 1.0 / (1.0 + jnp.exp(-x))


def _layernorm_rows(z, g, b):
    mu = jnp.mean(z, axis=-1, keepdims=True)
    d = z - mu
    var = jnp.mean(d * d, axis=-1, keepdims=True)
    return d * lax.rsqrt(var + LN_EPS) * g + b


def _store_row_tiles(ref, x):
    rows = x.shape[0]
    for j in range(ROW_TILE):
        ref[pl.ds(j, rows, stride=ROW_TILE), :] = x[:, j * LANES:(j + 1) * LANES]


def _load_row_tiles(ref, rows):
    return jnp.concatenate([ref[pl.ds(j, rows, stride=ROW_TILE), :] for j in range(ROW_TILE)], axis=1)


def _ada_body(c_ref, w_ref, b_ref, o_ref):
    c = c_ref[...]
    s = c * _sigmoid(c)
    o_ref[...] = jnp.dot(s, w_ref[...], precision=HIGHEST, preferred_element_type=F32) + b_ref[...]


def _modulation(cond, ada_w, ada_b):
    depth, d, n6 = ada_w.shape
    rows = cond.shape[0]
    tn = 1536
    return pl.pallas_call(
        _ada_body,
        grid=(depth, n6 // tn),
        in_specs=[
            pl.BlockSpec((rows, d), lambda l, j: (0, 0)),
            pl.BlockSpec((None, d, tn), lambda l, j: (l, 0, j)),
            pl.BlockSpec((None, 1, tn), lambda l, j: (l, 0, j)),
        ],
        out_specs=pl.BlockSpec((None, rows, tn), lambda l, j: (l, 0, j)),
        out_shape=jax.ShapeDtypeStruct((depth, rows, n6), F32),
        compiler_params=_params(("arbitrary", "arbitrary")),
        name="modulation",
    )(cond, ada_w, ada_b.reshape(depth, 1, n6))


class _Geom:
    def __init__(self, n_ctx, t_ctx, n_lat, t_lat):
        self.n_ctx, self.t_ctx, self.n_lat, self.t_lat = n_ctx, t_ctx, n_lat, t_lat
        self.tok_ctx = n_ctx * t_ctx
        self.tok_lat = n_lat * t_lat
        self.n_tok = self.tok_ctx + self.tok_lat
        self.ctx_row = n_lat
        assert t_ctx % TOK_TILE == 0 and t_lat % TOK_TILE == 0 and self.tok_ctx % t_lat == 0

    def seq_of_tile(self, i, tile):
        ctx_tiles = self.tok_ctx // tile
        return jnp.where(i < ctx_tiles, self.ctx_row, (i - ctx_tiles) // (self.t_lat // tile))


def _mod_spec(geom, tile, which):
    return pl.BlockSpec((None, 1, D_MODEL), lambda i, *_: (geom.seq_of_tile(i, tile), 0, which))


IN_CHUNK = 1408


def _inproj_body(x_ref, sc_ref, sh_ref, w_ref, o_ref):
    h = (x_ref[...] * (1.0 + sc_ref[...]) + sh_ref[...]).astype(BF16)
    for c0 in range(0, P_COLS, IN_CHUNK):
        o_ref[:, c0:c0 + IN_CHUNK] = jnp.dot(h, w_ref[:, c0:c0 + IN_CHUNK], preferred_element_type=F32)


def _in_proj(geom, x, mod, w_in_bf):
    tm = TOK_TILE
    return pl.pallas_call(
        _inproj_body,
        grid=(geom.n_tok // tm,),
        in_specs=[
            pl.BlockSpec((tm, D_MODEL), lambda i: (i, 0)),
            _mod_spec(geom, tm, 1),
            _mod_spec(geom, tm, 0),
            pl.BlockSpec((D_MODEL, P_COLS), lambda i: (0, 0), pipeline_mode=pl.Buffered(1)),
        ],
        out_specs=pl.BlockSpec((tm, P_COLS), lambda i: (i, 0)),
        out_shape=jax.ShapeDtypeStruct((geom.n_tok, P_COLS), F32),
        compiler_params=_params(("arbitrary",)),
        name="in_proj",
    )(x, mod, mod, w_in_bf)


def _sink_column(sink_ref, g, rows):
    return jnp.concatenate(
        [jnp.full((rows, 1), sink_ref[g * ATT_GROUP + h], F32) for h in range(ATT_GROUP)], axis=0)


def _stack_heads(q, g):
    return jnp.concatenate(
        [q[:, (g * ATT_GROUP + h) * HEAD_DIM:(g * ATT_GROUP + h + 1) * HEAD_DIM] for h in range(ATT_GROUP)],
        axis=0)


def _softmax_pv(score_blocks, value_blocks, sink_col):
    m = sink_col
    for s in score_blocks:
        m = jnp.maximum(m, jnp.max(s, axis=-1, keepdims=True))
    den = jnp.exp(sink_col - m)
    acc = None
    for s, v in zip(score_blocks, value_blocks):
        p = jnp.exp(s - m)
        den = den + jnp.sum(p, axis=-1, keepdims=True)
        pv = jnp.dot(p.astype(BF16), v, preferred_element_type=F32)
        acc = pv if acc is None else acc + pv
    return acc / den


def _nt_dot(a, b):
    return lax.dot_general(a, b, (((1,), (1,)), ((), ())), preferred_element_type=F32)


def _attn_ctx_body(q_ref, k_ref, v_ref, sink_ref, o_ref, *, n_seq):
    t = q_ref.shape[0]

    @pl.when(pl.program_id(0) < n_seq)
    def _():
        q = q_ref[...] * ATT_SCALE
        k = k_ref[...].astype(BF16)
        v = v_ref[...].astype(BF16)
        outs = []
        for g in range(ATT_KV_HEADS):
            kg = k[:, g * HEAD_DIM:(g + 1) * HEAD_DIM]
            vg = v[:, g * HEAD_DIM:(g + 1) * HEAD_DIM]
            qs = _stack_heads(q, g).astype(BF16)
            o = _softmax_pv([_nt_dot(qs, kg)], [vg], _sink_column(sink_ref, g, t))
            outs += [o[h * t:(h + 1) * t] for h in range(ATT_GROUP)]
        o_ref[...] = jnp.concatenate(outs, axis=1)

    @pl.when(pl.program_id(0) >= n_seq)
    def _():
        o_ref[...] = jnp.zeros_like(o_ref)


def _attn_ctx(geom, p, sink):
    t = geom.t_ctx
    last = geom.n_ctx - 1
    return pl.pallas_call(
        functools.partial(_attn_ctx_body, n_seq=geom.n_ctx),
        grid=(geom.n_tok // t,),
        in_specs=[
            pl.BlockSpec((t, ATT_Q_W), lambda b: (jnp.minimum(b, last), COL_QB // ATT_Q_W)),
            pl.BlockSpec((t, ATT_KV_W), lambda b: (jnp.minimum(b, last), COL_KB // ATT_KV_W)),
            pl.BlockSpec((t, ATT_KV_W), lambda b: (jnp.minimum(b, last), COL_VB // ATT_KV_W)),
            pl.BlockSpec(memory_space=pltpu.SMEM),
        ],
        out_specs=pl.BlockSpec((t, ATT_Q_W), lambda b: (b, 0)),
        out_shape=jax.ShapeDtypeStruct((geom.n_tok, ATT_Q_W), F32),
        compiler_params=_params(("arbitrary",)),
        name="attn_ctx",
    )(p, p, p, sink)


def _rope(x, cos, sin_lo, sin_hi):
    nf = HEAD_DIM // 4
    outs = []
    for j in range(x.shape[1] // LANES):
        xj = x[:, j * LANES:(j + 1) * LANES]
        outs.append(xj * cos + pltpu.roll(xj, LANES - nf, 1) * sin_lo + pltpu.roll(xj, nf, 1) * sin_hi)
    return outs[0] if len(outs) == 1 else jnp.concatenate(outs, axis=1)


def _attn_lat_body(q_ref, k_ref, v_ref, ck_ref, cv_ref, qcos_ref, qslo_ref, qshi_ref,
                   kcos_ref, kslo_ref, kshi_ref, sink_ref, yb_hbm, o_ref):
    del yb_hbm
    i = pl.program_id(1)
    nb = k_ref.shape[0] // ATT_BLOCK
    blk = ATT_BLOCK
    q = _rope(q_ref[...], qcos_ref[...], qslo_ref[...], qshi_ref[...]) * ATT_SCALE
    q_in_blk = lax.broadcasted_iota(jnp.int32, (ATT_GROUP * blk, blk), 0) % blk
    k_in_blk = lax.broadcasted_iota(jnp.int32, (ATT_GROUP * blk, blk), 1)
    k_win, v_win, masks = [], [], []
    for j in (-1, 0, 1):
        kb = i + j
        kbc = jnp.clip(kb, 0, nb - 1)
        rows = pl.ds(pl.multiple_of(kbc * blk, blk), blk)
        k_win.append(_rope(k_ref[rows, :], kcos_ref[rows, :], kslo_ref[rows, :], kshi_ref[rows, :]).astype(BF16))
        v_win.append(v_ref[rows, :].astype(BF16))
        outside = jnp.where((kb >= 0) & (kb < nb), 0, 4 * WINDOW)
        dist = (j * blk + k_in_blk) - q_in_blk
        masks.append(jnp.abs(dist) + outside <= WINDOW)
    ck = ck_ref[...].astype(BF16)
    cv = cv_ref[...].astype(BF16)
    outs = []
    for g in range(ATT_KV_HEADS):
        hs = slice(g * HEAD_DIM, (g + 1) * HEAD_DIM)
        qs = _stack_heads(q, g).astype(BF16)
        scores = [jnp.where(mk, _nt_dot(qs, kw[:, hs]), NEG_INF) for kw, mk in zip(k_win, masks)]
        scores.append(_nt_dot(qs, ck[:, hs]))
        values = [vw[:, hs] for vw in v_win] + [cv[:, hs]]
        o = _softmax_pv(scores, values, _sink_column(sink_ref, g, blk))
        outs += [o[h * blk:(h + 1) * blk] for h in range(ATT_GROUP)]
    o_ref[...] = jnp.concatenate(outs, axis=1)


def _attn_lat(geom, p, cache_k, cache_v, rope_tabs, sink, yb):
    t, blk = geom.t_lat, ATT_BLOCK
    nb = t // blk
    past = cache_k.shape[1]
    q_row0 = geom.tok_ctx // blk
    s_row0 = geom.tok_ctx // t
    cos, slo, shi = rope_tabs
    q_tab = pl.BlockSpec((blk, LANES), lambda b, i: (i, 0))
    k_tab = pl.BlockSpec((t, LANES), lambda b, i: (0, 0))
    return pl.pallas_call(
        _attn_lat_body,
        grid=(geom.n_lat, nb),
        in_specs=[
            pl.BlockSpec((blk, ATT_Q_W), lambda b, i: (q_row0 + b * nb + i, COL_QB // ATT_Q_W)),
            pl.BlockSpec((t, ATT_KV_W), lambda b, i: (s_row0 + b, COL_KB // ATT_KV_W)),
            pl.BlockSpec((t, ATT_KV_W), lambda b, i: (s_row0 + b, COL_VB // ATT_KV_W)),
            pl.BlockSpec((None, past, ATT_KV_W), lambda b, i: (b, 0, 0)),
            pl.BlockSpec((None, past, ATT_KV_W), lambda b, i: (b, 0, 0)),
            q_tab, q_tab, q_tab, k_tab, k_tab, k_tab,
            pl.BlockSpec(memory_space=pltpu.SMEM),
            pl.BlockSpec(memory_space=pl.ANY),
        ],
        out_specs=pl.BlockSpec((blk, ATT_Q_W), lambda b, i: (q_row0 + b * nb + i, 0)),
        out_shape=jax.ShapeDtypeStruct((geom.n_tok, ATT_Q_W), F32),
        input_output_aliases={12: 0},
        compiler_params=_params(("arbitrary", "arbitrary")),
        name="attn_lat",
    )(p, p, p, cache_k, cache_v, cos, slo, shi, cos, slo, shi, sink, yb)


def _rope_tables(t_lat):
    half = HEAD_DIM // 2
    nf = half // 2
    tok = jnp.arange(t_lat)
    lane = jnp.arange(LANES) % HEAD_DIM
    pos = jnp.where(lane[None, :] < half, (tok // GRID_W)[:, None], (tok % GRID_W)[:, None]).astype(F32)
    inv = ROPE_THETA ** (-(lane % nf).astype(F32) / nf)
    ang = pos * inv[None, :]
    first = (lane % half) < nf
    cos = jnp.cos(ang)
    sin = jnp.sin(ang)
    sin_lo = jnp.where(first[None, :], -sin, 0.0)
    sin_hi = jnp.where(first[None, :], 0.0, sin)
    return cos, sin_lo, sin_hi


def _gla_body(*refs, has_s0, n_seq):
    y_ref = refs[10 if has_s0 else 8]

    @pl.when(pl.program_id(0) < n_seq)
    def _():
        _gla_sequence(*refs, has_s0=has_s0)

    @pl.when(pl.program_id(0) >= n_seq)
    def _():
        y_ref[...] = jnp.zeros_like(y_ref)


def _gla_sequence(*refs, has_s0):
    if has_s0:
        (q_ref, k_ref, v_ref, r_ref, za_ref, w2_ref, b2_ref, g_ref, s0_ref, _yc_hbm,
         y_ref, sfin_ref, la_scr, o_scr, st_scr) = refs
    else:
        (q_ref, k_ref, v_ref, r_ref, za_ref, w2_ref, b2_ref, g_ref,
         y_ref, sfin_ref, la_scr, o_scr, st_scr) = refs
    t = q_ref.shape[0]
    c = GLA_CHUNK
    n = t // c
    z = jnp.dot(za_ref[...].astype(BF16), w2_ref[...], preferred_element_type=F32) + b2_ref[...]
    la_scr[...] = (jnp.minimum(z, 0.0) - jnp.log1p(jnp.exp(-jnp.abs(z)))) * (1.0 / GLA_GATE_NORM)

    ri = lax.broadcasted_iota(jnp.int32, (c, c), 0)
    ci = lax.broadcasted_iota(jnp.int32, (c, c), 1)

    for direction in (0, 1):
        keep = (ri >= ci) if direction == 0 else (ri <= ci)
        cum_mat = keep.astype(F32)
        if has_s0:
            st_scr[...] = s0_ref[direction]
        else:
            st_scr[...] = jnp.zeros_like(st_scr)

        def chunk(step, carry, direction=direction, keep=keep, cum_mat=cum_mat):
            ch = step if direction == 0 else n - 1 - step
            rows = pl.ds(pl.multiple_of(ch * c, c), c)
            la = la_scr[rows, direction * GLA_QK_W:(direction + 1) * GLA_QK_W]
            bc = jnp.dot(cum_mat, la, precision=HIGHEST, preferred_element_type=F32)
            last = bc[c - 1:c, :] if direction == 0 else bc[0:1, :]
            q = q_ref[rows, :] * (GLA_DK ** -0.5)
            k = k_ref[rows, :]
            v = v_ref[rows, :]
            qe = (q * jnp.exp(bc)).astype(BF16)
            ke = (k * jnp.exp(-bc)).astype(BF16)
            kd = (k * jnp.exp(last - bc)).astype(BF16)
            vt = v.T
            st = st_scr[...]
            o_heads, upd_heads = [], []
            for h in range(GLA_HEADS):
                ks = slice(h * GLA_DK, (h + 1) * GLA_DK)
                vs = slice(h * GLA_DV, (h + 1) * GLA_DV)
                att = jnp.where(keep, _nt_dot(qe[:, ks], ke[:, ks]), 0.0)
                o_h = jnp.dot(att.astype(BF16), v[:, vs].astype(BF16), preferred_element_type=F32)
                o_h = o_h + _nt_dot(qe[:, ks], st[:, ks].astype(BF16))
                o_heads.append(o_h)
                upd_heads.append(jnp.dot(vt[vs, :].astype(BF16), kd[:, ks], preferred_element_type=F32))
            o_c = jnp.concatenate(o_heads, axis=1)
            if direction == 0:
                o_scr[rows, :] = o_c
            else:
                o_scr[rows, :] = o_scr[rows, :] + o_c
            st_scr[...] = st * jnp.exp(last) + jnp.concatenate(upd_heads, axis=1)
            return carry

        lax.fori_loop(0, n, chunk, 0)
        sfin_ref[direction] = st_scr[...]

    g = g_ref[...]
    rb = 256
    for r0 in range(0, t, rb):
        of = o_scr[r0:r0 + rb, :]
        normed = []
        for h in range(GLA_HEADS):
            oh = of[:, h * GLA_DV:(h + 1) * GLA_DV]
            mu = jnp.mean(oh, axis=-1, keepdims=True)
            d = oh - mu
            var = jnp.mean(d * d, axis=-1, keepdims=True)
            normed.append(d * lax.rsqrt(var + LN_EPS) * g)
        r = r_ref[r0:r0 + rb, :]
        y_ref[r0:r0 + rb, :] = jnp.concatenate(normed, axis=1) * (r * _sigmoid(r))


def _gla(geom, p, latent, w2, b2, norm_g, s0t=None, yc=None):
    if latent:
        t, n_seq, row0 = geom.t_lat, geom.n_lat, geom.tok_ctx // geom.t_lat
    else:
        t, n_seq, row0 = geom.t_ctx, geom.n_ctx, 0
    has_s0 = s0t is not None
    steps = n_seq if latent else geom.n_tok // t
    last = n_seq - 1

    def col(width, off):
        return pl.BlockSpec((t, width), lambda b: (row0 + jnp.minimum(b, last), off // width))

    in_specs = [
        col(GLA_QK_W, COL_QC), col(GLA_QK_W, COL_KC), col(GLA_V_W, COL_VC), col(GLA_V_W, COL_RC),
        col(LANES, COL_ZA),
        pl.BlockSpec((LANES, 2 * GLA_QK_W), lambda b: (0, 0)),
        pl.BlockSpec((1, 2 * GLA_QK_W), lambda b: (0, 0)),
        pl.BlockSpec((1, GLA_DV), lambda b: (0, 0)),
    ]
    args = [p, p, p, p, p, w2, b2, norm_g]
    if has_s0:
        in_specs += [pl.BlockSpec((None, 2, GLA_DV, GLA_QK_W), lambda b: (b, 0, 0, 0)),
                     pl.BlockSpec(memory_space=pl.ANY)]
        args += [s0t, yc]
    return pl.pallas_call(
        functools.partial(_gla_body, has_s0=has_s0, n_seq=n_seq),
        grid=(steps,),
        in_specs=in_specs,
        out_specs=[
            pl.BlockSpec((t, GLA_V_W), lambda b: (row0 + b, 0)),
            pl.BlockSpec((None, 2, GLA_DV, GLA_QK_W), lambda b: (jnp.minimum(b, last), 0, 0, 0)),
        ],
        out_shape=[
            jax.ShapeDtypeStruct((geom.n_tok, GLA_V_W), F32),
            jax.ShapeDtypeStruct((n_seq, 2, GLA_DV, GLA_QK_W), F32),
        ],
        scratch_shapes=[
            pltpu.VMEM((t, 2 * GLA_QK_W), F32),
            pltpu.VMEM((t, GLA_V_W), F32),
            pltpu.VMEM((GLA_DV, GLA_QK_W), F32),
        ],
        input_output_aliases={len(args) - 1: 0} if has_s0 else {},
        compiler_params=_params(("arbitrary",)),
        name="gla_lat" if latent else "gla_ctx",
    )(*args)


def _merge_body(geom, x_ref, gt_ref, cv_ref, cvp_ref, cvn_ref, yb_ref, yc_ref, wbr_ref, wout_ref, cw_ref,
                g1_ref, sc2_ref, sh2_ref, lng_ref, lnb_ref, rwh_ref, rwl_ref, rb_ref,
                x1_ref, h2_ref, tidx_ref, tgate_ref, trank_ref, tcount_ref, *, dn_alpha):
    i = pl.program_id(0)
    tm = x_ref.shape[0]
    ctx_tiles = geom.tok_ctx // tm
    per_seq = jnp.where(i < ctx_tiles, geom.t_ctx // tm, geom.t_lat // tm)
    j = jnp.where(i < ctx_tiles, i, i - ctx_tiles) % per_seq
    has_prev = (j > 0).astype(F32)
    has_next = (j < per_seq - 1).astype(F32)

    cv = cv_ref[...]
    u, bg, cg = cv[:, :CONV_W], cv[:, CONV_W:2 * CONV_W], cv[:, 2 * CONV_W:]
    v = cg * u
    pv = cvp_ref[SUBLANES - 1:SUBLANES, :]
    nv = cvn_ref[0:1, :]
    v_before = has_prev * (pv[:, 2 * CONV_W:] * pv[:, :CONV_W])
    v_after = has_next * (nv[:, 2 * CONV_W:] * nv[:, :CONV_W])
    row = lax.broadcasted_iota(jnp.int32, (tm, 1), 0)
    v_prev = jnp.where(row == 0, v_before, pltpu.roll(v, 1, 0))
    v_next = jnp.where(row == tm - 1, v_after, pltpu.roll(v, tm - 1, 0))
    cw = cw_ref[...]
    y_a = bg * (cw[0:1, :] * v_prev + cw[1:2, :] * v + cw[2:3, :] * v_next)

    gt = gt_ref[...]
    merged = (_sigmoid(gt[:, :D_MODEL]) * jnp.dot(y_a.astype(BF16), wbr_ref[0], preferred_element_type=F32)
              + _sigmoid(gt[:, D_MODEL:2 * D_MODEL])
              * jnp.dot(yb_ref[...].astype(BF16), wbr_ref[1], preferred_element_type=F32)
              + _sigmoid(gt[:, 2 * D_MODEL:])
              * jnp.dot(yc_ref[...].astype(BF16), wbr_ref[2], preferred_element_type=F32))
    mix = jnp.dot(merged.astype(BF16), wout_ref[...], preferred_element_type=F32)
    x1 = _layernorm_rows(dn_alpha * x_ref[...] + g1_ref[...] * mix, lng_ref[...], lnb_ref[...])
    x1_ref[...] = x1
    h2 = x1 * (1.0 + sc2_ref[...]) + sh2_ref[...]
    _store_row_tiles(h2_ref, h2)

    h_hi = h2.astype(BF16)
    h_lo = (h2 - h_hi.astype(F32)).astype(BF16)
    logits = (jnp.dot(h_hi, rwh_ref[...], preferred_element_type=F32)
              + jnp.dot(h_lo, rwh_ref[...], preferred_element_type=F32)
              + jnp.dot(h_hi, rwl_ref[...], preferred_element_type=F32)) + rb_ref[...]
    lane = lax.broadcasted_iota(jnp.int32, logits.shape, 1).astype(F32)
    vals, top_v, top_i = logits, [], []
    for _ in range(TOP_K):
        m = jnp.max(vals, axis=-1, keepdims=True)
        am = jnp.min(jnp.where(vals == m, lane, float(N_EXPERTS)), axis=-1, keepdims=True)
        top_v.append(m)
        top_i.append(am)
        vals = jnp.where(lane == am, -jnp.inf, vals)
    ex = [jnp.exp(tv - top_v[0]) for tv in top_v]
    den = ex[0] + ex[1] + ex[2] + ex[3]
    slot = lax.broadcasted_iota(jnp.int32, (tm, TOP_K), 1)
    gates, idxs = ex[TOP_K - 1], top_i[TOP_K - 1]
    for k in range(TOP_K - 2, -1, -1):
        gates = jnp.where(slot == k, ex[k], gates)
        idxs = jnp.where(slot == k, top_i[k], idxs)
    tgate_ref[...] = gates / den
    tidx_ref[...] = idxs.astype(jnp.int32)

    onehots = [(lane == ti).astype(F32) for ti in top_i]
    member = onehots[0] + onehots[1] + onehots[2] + onehots[3]
    earlier = (lax.broadcasted_iota(jnp.int32, (tm, tm), 0) > lax.broadcasted_iota(jnp.int32, (tm, tm), 1))
    before = jnp.dot(earlier.astype(BF16), member.astype(BF16), preferred_element_type=F32)
    ranks = jnp.sum(onehots[TOP_K - 1] * before, axis=-1, keepdims=True)
    for k in range(TOP_K - 2, -1, -1):
        ranks = jnp.where(slot == k, jnp.sum(onehots[k] * before, axis=-1, keepdims=True), ranks)
    trank_ref[...] = ranks.astype(jnp.int32)
    tcount_ref[...] = (before[tm - 1:tm, :] + member[tm - 1:tm, :]).astype(jnp.int32)


def _merge(geom, x, p, yb, yc, mod, lw, dn_alpha):
    tm = TOK_TILE
    n = geom.n_tok
    halo_per_tile = tm // SUBLANES
    n_halo = n // SUBLANES
    conv_blk = COL_CONV // (3 * CONV_W)
    const2 = lambda i: (0, 0)
    return pl.pallas_call(
        functools.partial(_merge_body, geom, dn_alpha=dn_alpha),
        grid=(n // tm,),
        in_specs=[
            pl.BlockSpec((tm, D_MODEL), lambda i: (i, 0)),
            pl.BlockSpec((tm, 3 * D_MODEL), lambda i: (i, 0)),
            pl.BlockSpec((tm, 3 * CONV_W), lambda i: (i, conv_blk)),
            pl.BlockSpec((SUBLANES, 3 * CONV_W), lambda i: (jnp.maximum(i * halo_per_tile - 1, 0), conv_blk)),
            pl.BlockSpec((SUBLANES, 3 * CONV_W),
                         lambda i: (jnp.minimum((i + 1) * halo_per_tile, n_halo - 1), conv_blk)),
            pl.BlockSpec((tm, MIX_W), lambda i: (i, 0)),
            pl.BlockSpec((tm, MIX_W), lambda i: (i, 0)),
            pl.BlockSpec((3, MIX_W, D_MODEL), lambda i: (0, 0, 0)),
            pl.BlockSpec((D_MODEL, D_MODEL), const2),
            pl.BlockSpec((SUBLANES, CONV_W), const2),
            _mod_spec(geom, tm, 2), _mod_spec(geom, tm, 4), _mod_spec(geom, tm, 3),
            pl.BlockSpec((1, D_MODEL), const2), pl.BlockSpec((1, D_MODEL), const2),
            pl.BlockSpec((D_MODEL, N_EXPERTS), const2), pl.BlockSpec((D_MODEL, N_EXPERTS), const2),
            pl.BlockSpec((1, N_EXPERTS), const2),
        ],
        out_specs=[
            pl.BlockSpec((tm, D_MODEL), lambda i: (i, 0)),
            pl.BlockSpec((tm * ROW_TILE, LANES), lambda i: (i, 0)),
            pl.BlockSpec((tm, TOP_K), lambda i: (i, 0)),
            pl.BlockSpec((tm, TOP_K), lambda i: (i, 0)),
            pl.BlockSpec((tm, TOP_K), lambda i: (i, 0)),
            pl.BlockSpec((None, 1, N_EXPERTS), lambda i: (i, 0, 0)),
        ],
        out_shape=[
            jax.ShapeDtypeStruct((n, D_MODEL), F32),
            jax.ShapeDtypeStruct((n * ROW_TILE, LANES), F32),
            jax.ShapeDtypeStruct((n, TOP_K), jnp.int32),
            jax.ShapeDtypeStruct((n, TOP_K), F32),
            jax.ShapeDtypeStruct((n, TOP_K), jnp.int32),
            jax.ShapeDtypeStruct((n // tm, 1, N_EXPERTS), jnp.int32),
        ],
        compiler_params=_params(("arbitrary",)),
        name="merge",
    )(x, p, p, p, p, yb, yc, lw["w_branch"], lw["w_out"], lw["conv_w"], mod, mod, mod,
      lw["ln1_g"], lw["ln1_b"], lw["rw_hi"], lw["rw_lo"], lw["r_b"])


def _route(top_idx, rank, tile_counts, bm, tile):
    n_tok = top_idx.shape[0]
    n_tiles = n_tok // tile
    tcnt = tile_counts.reshape(n_tiles, N_EXPERTS)
    counts = jnp.sum(tcnt, axis=0)
    padded = (counts + bm - 1) // bm * bm
    pend = jnp.cumsum(padded)
    pstart = pend - padded
    tile_base = pstart[None, :] + jnp.cumsum(tcnt, axis=0) - tcnt
    tok_base = jnp.broadcast_to(tile_base[:, None, :], (n_tiles, tile, N_EXPERTS)).reshape(n_tok, 1, N_EXPERTS)
    onehot = top_idx[:, :, None] == jnp.arange(N_EXPERTS, dtype=jnp.int32)[None, None, :]
    dest = (jnp.sum(jnp.where(onehot, tok_base, 0), axis=-1) + rank).reshape(-1).astype(jnp.int32)
    n_blocks = n_tok * TOP_K // bm + N_EXPERTS
    blk_start = jnp.arange(n_blocks, dtype=jnp.int32) * bm
    blk_e = jnp.minimum(jnp.sum((pend[None, :] <= blk_start[:, None]).astype(jnp.int32), axis=1), N_EXPERTS - 1)
    n_used = (pend[-1] // bm).astype(jnp.int32).reshape(1)
    return dest, blk_e.astype(jnp.int32), n_used, (pstart + counts).astype(jnp.int32), (padded - counts).astype(
        jnp.int32)


def _row_tile(ref, row):
    return ref.at[pl.ds(pl.multiple_of(row * ROW_TILE, ROW_TILE), ROW_TILE), :]


def _dispatch_body(dest_ref, pad_start_ref, pad_n_ref, n_used_ref, h_hbm, xs_hbm, zbuf, sem, zsem, bsem,
                   *, chunk, bm):
    i = pl.program_id(0)
    n_steps = pl.num_programs(0)
    n_blocks = xs_hbm.shape[0] // (bm * ROW_TILE)

    def drain(slot):
        rows = chunk * TOP_K * ROW_TILE
        pltpu.make_async_copy(h_hbm.at[pl.ds(0, rows), :], xs_hbm.at[pl.ds(0, rows), :], sem.at[slot]).wait()

    @pl.when(i == 0)
    def _():
        zbuf[...] = jnp.zeros_like(zbuf)
        zrow = zbuf.at[pl.ds(0, ROW_TILE), :]

        def per_expert(e, total):
            start, cnt = pad_start_ref[e], pad_n_ref[e]

            def one(j, carry):
                pltpu.make_async_copy(zrow, _row_tile(xs_hbm, start + j), zsem).start()
                return carry

            lax.fori_loop(0, cnt, one, 0)
            return total + cnt

        total = lax.fori_loop(0, N_EXPERTS, per_expert, 0)

        def block_copy(b):
            return pltpu.make_async_copy(zbuf, xs_hbm.at[pl.ds(pl.multiple_of(b * bm * ROW_TILE, bm * ROW_TILE),
                                                                bm * ROW_TILE), :], bsem)

        def start_block(b, carry):
            block_copy(b).start()
            return carry

        def wait_block(b, carry):
            block_copy(b).wait()
            return carry

        lax.fori_loop(n_used_ref[0], n_blocks, start_block, 0)

        def wait_one(j, carry):
            pltpu.make_async_copy(zrow, _row_tile(xs_hbm, 0), zsem).wait()
            return carry

        lax.fori_loop(0, total, wait_one, 0)
        lax.fori_loop(n_used_ref[0], n_blocks, wait_block, 0)

    def one_token(r, carry):
        t = i * chunk + r
        for k in range(TOP_K):
            pltpu.make_async_copy(_row_tile(h_hbm, t), _row_tile(xs_hbm, dest_ref[t * TOP_K + k]),
                                  sem.at[i % 2]).start()
        return carry

    lax.fori_loop(0, chunk, one_token, 0, unroll=4)

    @pl.when(i > 0)
    def _():
        drain((i + 1) % 2)

    @pl.when(i == n_steps - 1)
    def _():
        drain(i % 2)


def _dispatch(h2_tiles, dest, pad_start, pad_n, n_used, n_blocks):
    chunk, bm = DISPATCH_CHUNK, MOE_ROWS
    n_tok = h2_tiles.shape[0] // ROW_TILE
    grid_spec = pltpu.PrefetchScalarGridSpec(
        num_scalar_prefetch=4,
        grid=(n_tok // chunk,),
        in_specs=[pl.BlockSpec(memory_space=pl.ANY)],
        out_specs=pl.BlockSpec(memory_space=pl.ANY),
        scratch_shapes=[
            pltpu.VMEM((bm * ROW_TILE, LANES), F32),
            pltpu.SemaphoreType.DMA((2,)),
            pltpu.SemaphoreType.DMA,
            pltpu.SemaphoreType.DMA,
        ],
    )
    return pl.pallas_call(
        functools.partial(_dispatch_body, chunk=chunk, bm=bm),
        grid_spec=grid_spec,
        out_shape=jax.ShapeDtypeStruct((n_blocks * bm * ROW_TILE, LANES), F32),
        compiler_params=_params(("arbitrary",), disable_bounds_checks=True),
        name="dispatch",
    )(dest, pad_start, pad_n, n_used, h2_tiles)


def _expert_body(blk_e_ref, n_used_ref, xs_ref, wgu_ref, bgu_ref, wdn_ref, bdn_ref, y_ref, wgu_bf, wdn_bf):
    i = pl.program_id(0)
    bm = y_ref.shape[0] // ROW_TILE
    n_used = n_used_ref[0]

    @pl.when(i < n_used)
    def _():
        e = blk_e_ref[i]
        e_before = blk_e_ref[jnp.maximum(i - 1, 0)]

        @pl.when((i == 0) | (e != e_before))
        def _():
            wgu_bf[...] = wgu_ref[...].astype(BF16)
            wdn_bf[...] = wdn_ref[...].astype(BF16)

        x = _load_row_tiles(xs_ref, bm).astype(BF16)
        acc = jnp.zeros((bm, D_MODEL), F32)
        for f0 in range(0, D_FF, FF_CHUNK):
            g = jnp.dot(x, wgu_bf[:, f0:f0 + FF_CHUNK], preferred_element_type=F32) + bgu_ref[:, f0:f0 + FF_CHUNK]
            u = (jnp.dot(x, wgu_bf[:, D_FF + f0:D_FF + f0 + FF_CHUNK], preferred_element_type=F32)
                 + bgu_ref[:, D_FF + f0:D_FF + f0 + FF_CHUNK])
            g = jnp.minimum(g, SWIGLU_LIMIT)
            u = jnp.clip(u, -SWIGLU_LIMIT, SWIGLU_LIMIT)
            act = (u + 1.0) * (g * _sigmoid(SWIGLU_ALPHA * g))
            acc = acc + jnp.dot(act.astype(BF16), wdn_bf[f0:f0 + FF_CHUNK, :], preferred_element_type=F32)
        _store_row_tiles(y_ref, acc + bdn_ref[...])

    @pl.when(i >= n_used)
    def _():
        y_ref[...] = jnp.zeros_like(y_ref)


def _experts(xs, layer, blk_e, n_used, w_gate_up, b_gate_up, w_down, b_down):
    bm = MOE_ROWS
    n_blocks = blk_e.shape[0]
    grid_spec = pltpu.PrefetchScalarGridSpec(
        num_scalar_prefetch=2,
        grid=(n_blocks,),
        in_specs=[
            pl.BlockSpec((bm * ROW_TILE, LANES), lambda i, be, nu: (jnp.minimum(i, nu[0] - 1), 0)),
            pl.BlockSpec((None, None, D_MODEL, 2 * D_FF), lambda i, be, nu: (layer, be[i], 0, 0)),
            pl.BlockSpec((None, None, 1, 2 * D_FF), lambda i, be, nu: (layer, be[i], 0, 0)),
            pl.BlockSpec((None, None, D_FF, D_MODEL), lambda i, be, nu: (layer, be[i], 0, 0)),
            pl.BlockSpec((None, None, 1, D_MODEL), lambda i, be, nu: (layer, be[i], 0, 0)),
        ],
        out_specs=pl.BlockSpec((bm * ROW_TILE, LANES), lambda i, be, nu: (i, 0)),
        scratch_shapes=[
            pltpu.VMEM((D_MODEL, 2 * D_FF), BF16),
            pltpu.VMEM((D_FF, D_MODEL), BF16),
        ],
    )
    depth = w_gate_up.shape[0]
    return pl.pallas_call(
        _expert_body,
        grid_spec=grid_spec,
        out_shape=jax.ShapeDtypeStruct((n_blocks * bm * ROW_TILE, LANES), F32),
        compiler_params=_params(("arbitrary",)),
        name="experts",
    )(blk_e, n_used, xs, w_gate_up, b_gate_up.reshape(depth, N_EXPERTS, 1, 2 * D_FF),
      w_down, b_down.reshape(depth, N_EXPERTS, 1, D_MODEL))


def _combine_body(geom, dest_ref, y_hbm, x1_ref, gate_ref, g2_ref, lng_ref, lnb_ref, o_ref, buf, sem,
                  *, dn_alpha):
    i = pl.program_id(0)
    n_steps = pl.num_programs(0)
    tc = o_ref.shape[0]

    def issue(step, slot):
        base = step * tc * TOP_K

        def one(r, carry):
            for k in range(TOP_K):
                pltpu.make_async_copy(_row_tile(y_hbm, dest_ref[base + r * TOP_K + k]),
                                      _row_tile(buf.at[slot, k], r), sem.at[slot]).start(priority=k % 2)
            return carry

        lax.fori_loop(0, tc, one, 0, unroll=4)

    @pl.when(i == 0)
    def _():
        issue(0, 0)

    @pl.when(i + 1 < n_steps)
    def _():
        issue(i + 1, (i + 1) % 2)

    slot = i % 2
    for k in range(TOP_K):
        pltpu.make_async_copy(y_hbm.at[pl.ds(0, tc * ROW_TILE), :], buf.at[slot, k], sem.at[slot]).wait()
    gate = gate_ref[...]
    moe = gate[:, 0:1] * _load_row_tiles(buf.at[slot, 0], tc)
    for k in range(1, TOP_K):
        moe = moe + gate[:, k:k + 1] * _load_row_tiles(buf.at[slot, k], tc)
    z = dn_alpha * x1_ref[...] + g2_ref[...] * moe
    o_ref[...] = _layernorm_rows(z, lng_ref[...], lnb_ref[...])


def _combine(geom, y_rows, dest, x1, top_gate, mod, ln_g, ln_b, dn_alpha):
    tc = COMB_TILE
    n = geom.n_tok
    grid_spec = pltpu.PrefetchScalarGridSpec(
        num_scalar_prefetch=1,
        grid=(n // tc,),
        in_specs=[
            pl.BlockSpec(memory_space=pl.ANY),
            pl.BlockSpec((tc, D_MODEL), lambda i, d: (i, 0)),
            pl.BlockSpec((tc, TOP_K), lambda i, d: (i, 0)),
            _mod_spec(geom, tc, 5),
            pl.BlockSpec((1, D_MODEL), lambda i, d: (0, 0)),
            pl.BlockSpec((1, D_MODEL), lambda i, d: (0, 0)),
        ],
        out_specs=pl.BlockSpec((tc, D_MODEL), lambda i, d: (i, 0)),
        scratch_shapes=[
            pltpu.VMEM((2, TOP_K, tc * ROW_TILE, LANES), F32),
            pltpu.SemaphoreType.DMA((2,)),
        ],
    )
    return pl.pallas_call(
        functools.partial(_combine_body, geom, dn_alpha=dn_alpha),
        grid_spec=grid_spec,
        out_shape=jax.ShapeDtypeStruct((n, D_MODEL), F32),
        compiler_params=_params(("arbitrary",), disable_bounds_checks=True),
        name="combine",
    )(dest, y_rows, x1, top_gate, mod, ln_g, ln_b)


def _rearrange_w_in(w_in):
    splits = (CONV_W, CONV_W, CONV_W, ATT_Q_W, ATT_KV_W, ATT_KV_W, GLA_QK_W, GLA_QK_W, GLA_V_W, GLA_V_W,
              GLA_GATE_RANK, GLA_GATE_RANK, D_MODEL, D_MODEL, D_MODEL)
    parts, o = [], 0
    for w in splits:
        parts.append(w_in[..., o:o + w])
        o += w
    u_a, b_a, c_a, q_b, k_b, v_b, q_c, k_c, v_c, r_c, za_f, za_b, gt_a, gt_b, gt_c = parts
    pad = jnp.zeros(w_in.shape[:-1] + (LANES - 2 * GLA_GATE_RANK,), w_in.dtype)
    return jnp.concatenate([gt_a, gt_b, gt_c, u_a, b_a, c_a, q_b, v_c, r_c, q_c, k_c, k_b, v_b, za_f, za_b, pad],
                           axis=-1).astype(BF16)


def kernel(x_prompt, x_sample, cache_k, cache_v, state_gla, c, c_ctx, ada_w, ada_b, w_in, conv_w, attn_sink,
           gla_wa2, gla_ba2, gla_norm_g, w_branch, w_out, ln1_g, ln1_b, ln2_g, ln2_b, router_w, router_b,
           w_gate_up, b_gate_up, w_down, b_down):
    depth = w_in.shape[0]
    n_ctx, t_ctx, _ = x_prompt.shape
    n_lat, t_lat, _ = x_sample.shape
    geom = _Geom(n_ctx, t_ctx, n_lat, t_lat)
    dn_alpha = (2 * depth) ** 0.25

    mod_rows = -(-(n_lat + 1) // SUBLANES) * SUBLANES
    cond = jnp.zeros((mod_rows, D_MODEL), F32).at[:n_lat].set(c).at[n_lat].set(c_ctx)
    mods = _modulation(cond, ada_w, ada_b).reshape(depth, mod_rows, 1, 6 * D_MODEL)

    w_in_bf = _rearrange_w_in(w_in)
    w_branch_bf = w_branch.astype(BF16)
    w_out_bf = w_out.astype(BF16)
    conv_w_pad = jnp.zeros((depth, SUBLANES, CONV_W), F32).at[:, :conv_w.shape[1]].set(conv_w)
    rw_hi = router_w.astype(BF16)
    rw_lo = (router_w - rw_hi.astype(F32)).astype(BF16)
    w2 = jnp.zeros((depth, LANES, 2 * GLA_QK_W), F32)
    w2 = w2.at[:, :GLA_GATE_RANK, :GLA_QK_W].set(gla_wa2[:, 0])
    w2 = w2.at[:, GLA_GATE_RANK:2 * GLA_GATE_RANK, GLA_QK_W:].set(gla_wa2[:, 1]).astype(BF16)
    b2 = gla_ba2.reshape(depth, 1, 2 * GLA_QK_W)
    rope_tabs = _rope_tables(t_lat)
    s0t = jnp.transpose(state_gla, (0, 1, 2, 5, 3, 4)).reshape(n_lat, depth, 2, GLA_DV, GLA_QK_W)
    past = cache_k.shape[2]
    ck = cache_k.reshape(n_lat, depth, past, ATT_KV_W)
    cv = cache_v.reshape(n_lat, depth, past, ATT_KV_W)

    x = jnp.concatenate([x_prompt.reshape(-1, D_MODEL), x_sample.reshape(-1, D_MODEL)], axis=0)
    new_k, new_v, new_s = [], [], []
    for l in range(depth):
        mod = mods[l]
        lw = dict(w_branch=w_branch_bf[l], w_out=w_out_bf[l], conv_w=conv_w_pad[l],
                  ln1_g=ln1_g[l].reshape(1, -1), ln1_b=ln1_b[l].reshape(1, -1),
                  rw_hi=rw_hi[l], rw_lo=rw_lo[l], r_b=router_b[l].reshape(1, -1))
        p = _in_proj(geom, x, mod, w_in_bf[l])
        new_k.append(p[:geom.tok_ctx, COL_KB:COL_KB + ATT_KV_W].reshape(n_ctx, t_ctx, ATT_KV_HEADS, HEAD_DIM))
        new_v.append(p[:geom.tok_ctx, COL_VB:COL_VB + ATT_KV_W].reshape(n_ctx, t_ctx, ATT_KV_HEADS, HEAD_DIM))
        yb = _attn_lat(geom, p, ck[:, l], cv[:, l], rope_tabs, attn_sink[l], _attn_ctx(geom, p, attn_sink[l]))
        g_norm = gla_norm_g[l].reshape(1, GLA_DV)
        yc, s_ctx = _gla(geom, p, False, w2[l], b2[l], g_norm)
        yc, _ = _gla(geom, p, True, w2[l], b2[l], g_norm, s0t[:, l], yc)
        new_s.append(jnp.transpose(s_ctx.reshape(n_ctx, 2, GLA_DV, GLA_HEADS, GLA_DK), (0, 1, 3, 4, 2)))
        x1, h2_tiles, top_idx, top_gate, rank, tile_counts = _merge(geom, x, p, yb, yc, mod, lw, dn_alpha)
        dest, blk_e, n_used, pad_start, pad_n = _route(top_idx, rank, tile_counts, MOE_ROWS, TOK_TILE)
        xs = _dispatch(h2_tiles, dest, pad_start, pad_n, n_used, blk_e.shape[0])
        y_rows = _experts(xs, l, blk_e, n_used, w_gate_up, b_gate_up, w_down, b_down)
        x = _combine(geom, y_rows, dest, x1, top_gate, mod, ln2_g[l].reshape(1, -1), ln2_b[l].reshape(1, -1),
                     dn_alpha)
    y_prompt = x[:geom.tok_ctx].reshape(n_ctx, t_ctx, D_MODEL)
    y_sample = x[geom.tok_ctx:].reshape(n_lat, t_lat, D_MODEL)
    return (y_prompt, y_sample, jnp.stack(new_k, axis=1), jnp.stack(new_v, axis=1), jnp.stack(new_s, axis=1))
```

```python
import functools

import jax
import jax.numpy as jnp
from jax import lax
from jax.experimental import pallas as pl
from jax.experimental.pallas import tpu as pltpu

F32 = jnp.float32
BF16 = jnp.bfloat16
HIGHEST = lax.Precision.HIGHEST

D_MODEL = 1024
MIX_W = D_MODEL // 2
CONV_W = MIX_W
HEAD_DIM = 64
ATT_HEADS = MIX_W // HEAD_DIM
ATT_KV_HEADS = 2
ATT_GROUP = ATT_HEADS // ATT_KV_HEADS
ATT_Q_W = ATT_HEADS * HEAD_DIM
ATT_KV_W = ATT_KV_HEADS * HEAD_DIM
WINDOW = 128
ATT_BLOCK = 128
ATT_SCALE = HEAD_DIM ** -0.5
ROPE_THETA = 10000.0
NEG_INF = -1e30
GRID_W = 64
GLA_HEADS = 4
GLA_DK = MIX_W // 2 // GLA_HEADS
GLA_DV = MIX_W // GLA_HEADS
GLA_QK_W = GLA_HEADS * GLA_DK
GLA_V_W = GLA_HEADS * GLA_DV
GLA_GATE_RANK = 16
GLA_GATE_NORM = 16.0
GLA_CHUNK = 64
N_EXPERTS = 32
TOP_K = 4
D_FF = D_MODEL
SWIGLU_LIMIT = 7.0
SWIGLU_ALPHA = 1.702
LN_EPS = 1e-5

LANES = 128
SUBLANES = 8
VMEM_LIMIT_BYTES = 56 * 1024 * 1024

COL_GATES = 0
COL_CONV = 3 * D_MODEL
COL_QB = COL_CONV + 3 * CONV_W
COL_VC = COL_QB + ATT_Q_W
COL_RC = COL_VC + GLA_V_W
COL_QC = COL_RC + GLA_V_W
COL_KC = COL_QC + GLA_QK_W
COL_KB = COL_KC + GLA_QK_W
COL_VB = COL_KB + ATT_KV_W
COL_ZA = COL_VB + ATT_KV_W
P_COLS = COL_ZA + LANES

TOK_TILE = 256
MOE_ROWS = 256
COMB_TILE = 256
DISPATCH_CHUNK = 1024
FF_CHUNK = 512

ROW_TILE = D_MODEL // LANES


def _params(semantics, **kw):
    return pltpu.CompilerParams(dimension_semantics=semantics, vmem_limit_bytes=VMEM_LIMIT_BYTES, **kw)


def _sigmoid(x):
    return 1.0 / (1.0 + jnp.exp(-x))


def _layernorm_rows(z, g, b):
    mu = jnp.mean(z, axis=-1, keepdims=True)
    d = z - mu
    var = jnp.mean(d * d, axis=-1, keepdims=True)
    return d * lax.rsqrt(var + LN_EPS) * g + b


def _store_row_tiles(ref, x):
    rows = x.shape[0]
    for j in range(ROW_TILE):
        ref[pl.ds(j, rows, stride=ROW_TILE), :] = x[:, j * LANES:(j + 1) * LANES]


def _load_row_tiles(ref, rows):
    return jnp.concatenate([ref[pl.ds(j, rows, stride=ROW_TILE), :] for j in range(ROW_TILE)], axis=1)


def _ada_body(c_ref, w_ref, b_ref, o_ref):
    c = c_ref[...]
    s = c * _sigmoid(c)
    o_ref[...] = jnp.dot(s, w_ref[...], precision=HIGHEST, preferred_element_type=F32) + b_ref[...]


def _modulation(cond, ada_w, ada_b):
    depth, d, n6 = ada_w.shape
    rows = cond.shape[0]
    tn = 1536
    return pl.pallas_call(
        _ada_body,
        grid=(depth, n6 // tn),
        in_specs=[
            pl.BlockSpec((rows, d), lambda l, j: (0, 0)),
            pl.BlockSpec((None, d, tn), lambda l, j: (l, 0, j)),
            pl.BlockSpec((None, 1, tn), lambda l, j: (l, 0, j)),
        ],
        out_specs=pl.BlockSpec((None, rows, tn), lambda l, j: (l, 0, j)),
        out_shape=jax.ShapeDtypeStruct((depth, rows, n6), F32),
        compiler_params=_params(("arbitrary", "arbitrary")),
        name="modulation",
    )(cond, ada_w, ada_b.reshape(depth, 1, n6))


class _Geom:
    def __init__(self, n_ctx, t_ctx, n_lat, t_lat):
        self.n_ctx, self.t_ctx, self.n_lat, self.t_lat = n_ctx, t_ctx, n_lat, t_lat
        self.tok_ctx = n_ctx * t_ctx
        self.tok_lat = n_lat * t_lat
        self.n_tok = self.tok_ctx + self.tok_lat
        self.ctx_row = n_lat
        assert t_ctx % TOK_TILE == 0 and t_lat % TOK_TILE == 0 and self.tok_ctx % t_lat == 0

    def seq_of_tile(self, i, tile):
        ctx_tiles = self.tok_ctx // tile
        return jnp.where(i < ctx_tiles, self.ctx_row, (i - ctx_tiles) // (self.t_lat // tile))


def _mod_spec(geom, tile, which):
    return pl.BlockSpec((None, 1, D_MODEL), lambda i, *_: (geom.seq_of_tile(i, tile), 0, which))


IN_CHUNK = 1408


def _inproj_body(x_ref, sc_ref, sh_ref, w_ref, o_ref):
    h = (x_ref[...] * (1.0 + sc_ref[...]) + sh_ref[...]).astype(BF16)
    for c0 in range(0, P_COLS, IN_CHUNK):
        o_ref[:, c0:c0 + IN_CHUNK] = jnp.dot(h, w_ref[:, c0:c0 + IN_CHUNK], preferred_element_type=F32)


def _in_proj(geom, x, mod, w_in_bf):
    tm = TOK_TILE
    return pl.pallas_call(
        _inproj_body,
        grid=(geom.n_tok // tm,),
        in_specs=[
            pl.BlockSpec((tm, D_MODEL), lambda i: (i, 0)),
            _mod_spec(geom, tm, 1),
            _mod_spec(geom, tm, 0),
            pl.BlockSpec((D_MODEL, P_COLS), lambda i: (0, 0), pipeline_mode=pl.Buffered(1)),
        ],
        out_specs=pl.BlockSpec((tm, P_COLS), lambda i: (i, 0)),
        out_shape=jax.ShapeDtypeStruct((geom.n_tok, P_COLS), F32),
        compiler_params=_params(("arbitrary",)),
        name="in_proj",
    )(x, mod, mod, w_in_bf)


def _sink_column(sink_ref, g, rows):
    return jnp.concatenate(
        [jnp.full((rows, 1), sink_ref[g * ATT_GROUP + h], F32) for h in range(ATT_GROUP)], axis=0)


def _stack_heads(q, g):
    return jnp.concatenate(
        [q[:, (g * ATT_GROUP + h) * HEAD_DIM:(g * ATT_GROUP + h + 1) * HEAD_DIM] for h in range(ATT_GROUP)],
        axis=0)


def _softmax_pv(score_blocks, value_blocks, sink_col):
    m = sink_col
    for s in score_blocks:
        m = jnp.maximum(m, jnp.max(s, axis=-1, keepdims=True))
    den = jnp.exp(sink_col - m)
    acc = None
    for s, v in zip(score_blocks, value_blocks):
        p = jnp.exp(s - m)
        den = den + jnp.sum(p, axis=-1, keepdims=True)
        pv = jnp.dot(p.astype(BF16), v, preferred_element_type=F32)
        acc = pv if acc is None else acc + pv
    return acc / den


def _nt_dot(a, b):
    return lax.dot_general(a, b, (((1,), (1,)), ((), ())), preferred_element_type=F32)


def _attn_ctx_body(q_ref, k_ref, v_ref, sink_ref, o_ref, *, n_seq):
    t = q_ref.shape[0]

    @pl.when(pl.program_id(0) < n_seq)
    def _():
        q = q_ref[...] * ATT_SCALE
        k = k_ref[...].astype(BF16)
        v = v_ref[...].astype(BF16)
        outs = []
        for g in range(ATT_KV_HEADS):
            kg = k[:, g * HEAD_DIM:(g + 1) * HEAD_DIM]
            vg = v[:, g * HEAD_DIM:(g + 1) * HEAD_DIM]
            qs = _stack_heads(q, g).astype(BF16)
            o = _softmax_pv([_nt_dot(qs, kg)], [vg], _sink_column(sink_ref, g, t))
            outs += [o[h * t:(h + 1) * t] for h in range(ATT_GROUP)]
        o_ref[...] = jnp.concatenate(outs, axis=1)

    @pl.when(pl.program_id(0) >= n_seq)
    def _():
        o_ref[...] = jnp.zeros_like(o_ref)


def _attn_ctx(geom, p, sink):
    t = geom.t_ctx
    last = geom.n_ctx - 1
    return pl.pallas_call(
        functools.partial(_attn_ctx_body, n_seq=geom.n_ctx),
        grid=(geom.n_tok // t,),
        in_specs=[
            pl.BlockSpec((t, ATT_Q_W), lambda b: (jnp.minimum(b, last), COL_QB // ATT_Q_W)),
            pl.BlockSpec((t, ATT_KV_W), lambda b: (jnp.minimum(b, last), COL_KB // ATT_KV_W)),
            pl.BlockSpec((t, ATT_KV_W), lambda b: (jnp.minimum(b, last), COL_VB // ATT_KV_W)),
            pl.BlockSpec(memory_space=pltpu.SMEM),
        ],
        out_specs=pl.BlockSpec((t, ATT_Q_W), lambda b: (b, 0)),
        out_shape=jax.ShapeDtypeStruct((geom.n_tok, ATT_Q_W), F32),
        compiler_params=_params(("arbitrary",)),
        name="attn_ctx",
    )(p, p, p, sink)


def _rope(x, cos, sin_lo, sin_hi):
    nf = HEAD_DIM // 4
    outs = []
    for j in range(x.shape[1] // LANES):
        xj = x[:, j * LANES:(j + 1) * LANES]
        outs.append(xj * cos + pltpu.roll(xj, LANES - nf, 1) * sin_lo + pltpu.roll(xj, nf, 1) * sin_hi)
    return outs[0] if len(outs) == 1 else jnp.concatenate(outs, axis=1)


def _attn_lat_body(q_ref, k_ref, v_ref, ck_ref, cv_ref, qcos_ref, qslo_ref, qshi_ref,
                   kcos_ref, kslo_ref, kshi_ref, sink_ref, yb_hbm, o_ref):
    del yb_hbm
    i = pl.program_id(1)
    nb = k_ref.shape[0] // ATT_BLOCK
    blk = ATT_BLOCK
    q = _rope(q_ref[...], qcos_ref[...], qslo_ref[...], qshi_ref[...]) * ATT_SCALE
    q_in_blk = lax.broadcasted_iota(jnp.int32, (ATT_GROUP * blk, blk), 0) % blk
    k_in_blk = lax.broadcasted_iota(jnp.int32, (ATT_GROUP * blk, blk), 1)
    k_win, v_win, masks = [], [], []
    for j in (-1, 0, 1):
        kb = i + j
        kbc = jnp.clip(kb, 0, nb - 1)
        rows = pl.ds(pl.multiple_of(kbc * blk, blk), blk)
        k_win.append(_rope(k_ref[rows, :], kcos_ref[rows, :], kslo_ref[rows, :], kshi_ref[rows, :]).astype(BF16))
        v_win.append(v_ref[rows, :].astype(BF16))
        outside = jnp.where((kb >= 0) & (kb < nb), 0, 4 * WINDOW)
        dist = (j * blk + k_in_blk) - q_in_blk
        masks.append(jnp.abs(dist) + outside <= WINDOW)
    ck = ck_ref[...].astype(BF16)
    cv = cv_ref[...].astype(BF16)
    outs = []
    for g in range(ATT_KV_HEADS):
        hs = slice(g * HEAD_DIM, (g + 1) * HEAD_DIM)
        qs = _stack_heads(q, g).astype(BF16)
        scores = [jnp.where(mk, _nt_dot(qs, kw[:, hs]), NEG_INF) for kw, mk in zip(k_win, masks)]
        scores.append(_nt_dot(qs, ck[:, hs]))
        values = [vw[:, hs] for vw in v_win] + [cv[:, hs]]
        o = _softmax_pv(scores, values, _sink_column(sink_ref, g, blk))
        outs += [o[h * blk:(h + 1) * blk] for h in range(ATT_GROUP)]
    o_ref[...] = jnp.concatenate(outs, axis=1)


def _attn_lat(geom, p, cache_k, cache_v, rope_tabs, sink, yb):
    t, blk = geom.t_lat, ATT_BLOCK
    nb = t // blk
    past = cache_k.shape[1]
    q_row0 = geom.tok_ctx // blk
    s_row0 = geom.tok_ctx // t
    cos, slo, shi = rope_tabs
    q_tab = pl.BlockSpec((blk, LANES), lambda b, i: (i, 0))
    k_tab = pl.BlockSpec((t, LANES), lambda b, i: (0, 0))
    return pl.pallas_call(
        _attn_lat_body,
        grid=(geom.n_lat, nb),
        in_specs=[
            pl.BlockSpec((blk, ATT_Q_W), lambda b, i: (q_row0 + b * nb + i, COL_QB // ATT_Q_W)),
            pl.BlockSpec((t, ATT_KV_W), lambda b, i: (s_row0 + b, COL_KB // ATT_KV_W)),
            pl.BlockSpec((t, ATT_KV_W), lambda b, i: (s_row0 + b, COL_VB // ATT_KV_W)),
            pl.BlockSpec((None, past, ATT_KV_W), lambda b, i: (b, 0, 0)),
            pl.BlockSpec((None, past, ATT_KV_W), lambda b, i: (b, 0, 0)),
            q_tab, q_tab, q_tab, k_tab, k_tab, k_tab,
            pl.BlockSpec(memory_space=pltpu.SMEM),
            pl.BlockSpec(memory_space=pl.ANY),
        ],
        out_specs=pl.BlockSpec((blk, ATT_Q_W), lambda b, i: (q_row0 + b * nb + i, 0)),
        out_shape=jax.ShapeDtypeStruct((geom.n_tok, ATT_Q_W), F32),
        input_output_aliases={12: 0},
        compiler_params=_params(("arbitrary", "arbitrary")),
        name="attn_lat",
    )(p, p, p, cache_k, cache_v, cos, slo, shi, cos, slo, shi, sink, yb)


def _rope_tables(t_lat):
    half = HEAD_DIM // 2
    nf = half // 2
    tok = jnp.arange(t_lat)
    lane = jnp.arange(LANES) % HEAD_DIM
    pos = jnp.where(lane[None, :] < half, (tok // GRID_W)[:, None], (tok % GRID_W)[:, None]).astype(F32)
    inv = ROPE_THETA ** (-(lane % nf).astype(F32) / nf)
    ang = pos * inv[None, :]
    first = (lane % half) < nf
    cos = jnp.cos(ang)
    sin = jnp.sin(ang)
    sin_lo = jnp.where(first[None, :], -sin, 0.0)
    sin_hi = jnp.where(first[None, :], 0.0, sin)
    return cos, sin_lo, sin_hi


def _gla_body(*refs, has_s0, n_seq):
    y_ref = refs[10 if has_s0 else 8]

    @pl.when(pl.program_id(0) < n_seq)
    def _():
        _gla_sequence(*refs, has_s0=has_s0)

    @pl.when(pl.program_id(0) >= n_seq)
    def _():
        y_ref[...] = jnp.zeros_like(y_ref)


def _gla_sequence(*refs, has_s0):
    if has_s0:
        (q_ref, k_ref, v_ref, r_ref, za_ref, w2_ref, b2_ref, g_ref, s0_ref, _yc_hbm,
         y_ref, sfin_ref, bc_scr, o_scr, qe_scr, upd_scr, st_scr, last_scr) = refs
    else:
        (q_ref, k_ref, v_ref, r_ref, za_ref, w2_ref, b2_ref, g_ref,
         y_ref, sfin_ref, bc_scr, o_scr, qe_scr, upd_scr, st_scr, last_scr) = refs
    t = q_ref.shape[0]
    c, nh = GLA_CHUNK, GLA_HEADS
    n = t // c
    rb = nh * c

    def iota(shape, axis):
        return lax.broadcasted_iota(jnp.int32, shape, axis)

    z = jnp.dot(za_ref[...].astype(BF16), w2_ref[...], preferred_element_type=F32) + b2_ref[...]
    la = (jnp.minimum(z, 0.0) - jnp.log1p(jnp.exp(-jnp.abs(z)))) * (1.0 / GLA_GATE_NORM)
    same_chunk = (iota((rb, rb), 0) // c) == (iota((rb, rb), 1) // c)
    cum = (jnp.where(same_chunk & (iota((rb, rb), 0) >= iota((rb, rb), 1)), 1.0, 0.0).astype(BF16),
           jnp.where(same_chunk & (iota((rb, rb), 0) <= iota((rb, rb), 1)), 1.0, 0.0).astype(BF16))
    for r0 in range(0, t, rb):
        for d in (0, 1):
            rest = la[r0:r0 + rb, d * GLA_QK_W:(d + 1) * GLA_QK_W]
            bc = None
            for _ in range(3):
                term = rest.astype(BF16)
                rest = rest - term.astype(F32)
                part = jnp.dot(cum[d], term, preferred_element_type=F32)
                bc = part if bc is None else bc + part
            bc_scr[r0:r0 + rb, d * GLA_QK_W:(d + 1) * GLA_QK_W] = bc

    in_chunk = iota((c, nh * c), 1) % c
    keeps = (iota((c, nh * c), 0) >= in_chunk, iota((c, nh * c), 0) <= in_chunk)
    own_k = (iota((nh * c, GLA_QK_W), 0) // c) == (iota((nh * c, GLA_QK_W), 1) // GLA_DK)
    own_v = (iota((nh * c, GLA_V_W), 0) // c) == (iota((nh * c, GLA_V_W), 1) // GLA_DV)
    own_s = (iota((nh * GLA_DV, GLA_QK_W), 0) // GLA_DV) == (iota((nh * GLA_DV, GLA_QK_W), 1) // GLA_DK)
    lane_head = iota((GLA_DV, GLA_QK_W), 1) // GLA_DK

    def stack(x):
        return jnp.concatenate([x] * nh, axis=0)

    def chunk_rows(ch):
        return pl.ds(pl.multiple_of(ch * c, c), c)

    def local(ch, carry):
        rows = chunk_rows(ch)
        q = q_ref[rows, :] * (GLA_DK ** -0.5)
        k = k_ref[rows, :]
        v = v_ref[rows, :]
        v_bd = jnp.where(own_v, stack(v), 0.0).astype(BF16)
        vt = v.T.astype(BF16)
        o_sum = None
        for d in (0, 1):
            bc = bc_scr[rows, d * GLA_QK_W:(d + 1) * GLA_QK_W]
            last = bc[c - 1:c, :] if d == 0 else bc[0:1, :]
            qe = (q * jnp.exp(bc)).astype(BF16)
            kd = (k * jnp.exp(last - bc)).astype(BF16)
            k_bd = jnp.where(own_k, stack(k * jnp.exp(-bc)), 0.0).astype(BF16)
            qe_scr[d, rows, :] = qe
            last_scr[d, ch] = jnp.broadcast_to(last, (SUBLANES, GLA_QK_W))
            att = jnp.where(keeps[d], _nt_dot(qe, k_bd), 0.0)
            o_d = jnp.dot(att.astype(BF16), v_bd, preferred_element_type=F32)
            u = jnp.dot(vt, kd, preferred_element_type=F32)
            upd = jnp.where(lane_head == 0, u[:GLA_DV], 0.0)
            for h in range(1, nh):
                upd = jnp.where(lane_head == h, u[h * GLA_DV:(h + 1) * GLA_DV], upd)
            upd_scr[d, ch] = upd
            o_sum = o_d if o_sum is None else o_sum + o_d
        o_scr[rows, :] = o_sum
        return carry

    lax.fori_loop(0, n, local, 0, unroll=2)

    for d in (0, 1):
        def scan(step, st, d=d):
            ch = step if d == 0 else n - 1 - step
            st_scr[d, ch] = st
            return st * jnp.exp(last_scr[d, ch][0:1, :]) + upd_scr[d, ch]

        st0 = s0_ref[d] if has_s0 else jnp.zeros((GLA_DV, GLA_QK_W), F32)
        sfin_ref[d] = lax.fori_loop(0, n, scan, st0)

    def cross(ch, carry):
        rows = chunk_rows(ch)
        acc = o_scr[rows, :]
        for d in (0, 1):
            s_bd = jnp.where(own_s, stack(st_scr[d, ch]), 0.0).astype(BF16)
            acc = acc + _nt_dot(qe_scr[d, rows, :], s_bd)
        o_scr[rows, :] = acc
        return carry

    lax.fori_loop(0, n, cross, 0, unroll=2)

    g = g_ref[...]
    rb = 256
    for r0 in range(0, t, rb):
        of = o_scr[r0:r0 + rb, :]
        normed = []
        for h in range(GLA_HEADS):
            oh = of[:, h * GLA_DV:(h + 1) * GLA_DV]
            mu = jnp.mean(oh, axis=-1, keepdims=True)
            d = oh - mu
            var = jnp.mean(d * d, axis=-1, keepdims=True)
            normed.append(d * lax.rsqrt(var + LN_EPS) * g)
        r = r_ref[r0:r0 + rb, :]
        y_ref[r0:r0 + rb, :] = jnp.concatenate(normed, axis=1) * (r * _sigmoid(r))


def _gla(geom, p, latent, w2, b2, norm_g, s0t=None, yc=None):
    if latent:
        t, n_seq, row0 = geom.t_lat, geom.n_lat, geom.tok_ctx // geom.t_lat
    else:
        t, n_seq, row0 = geom.t_ctx, geom.n_ctx, 0
    has_s0 = s0t is not None
    steps = n_seq if latent else geom.n_tok // t
    last = n_seq - 1

    def col(width, off):
        return pl.BlockSpec((t, width), lambda b: (row0 + jnp.minimum(b, last), off // width))

    in_specs = [
        col(GLA_QK_W, COL_QC), col(GLA_QK_W, COL_KC), col(GLA_V_W, COL_VC), col(GLA_V_W, COL_RC),
        col(LANES, COL_ZA),
        pl.BlockSpec((LANES, 2 * GLA_QK_W), lambda b: (0, 0)),
        pl.BlockSpec((1, 2 * GLA_QK_W), lambda b: (0, 0)),
        pl.BlockSpec((1, GLA_DV), lambda b: (0, 0)),
    ]
    args = [p, p, p, p, p, w2, b2, norm_g]
    if has_s0:
        in_specs += [pl.BlockSpec((None, 2, GLA_DV, GLA_QK_W), lambda b: (b, 0, 0, 0)),
                     pl.BlockSpec(memory_space=pl.ANY)]
        args += [s0t, yc]
    return pl.pallas_call(
        functools.partial(_gla_body, has_s0=has_s0, n_seq=n_seq),
        grid=(steps,),
        in_specs=in_specs,
        out_specs=[
            pl.BlockSpec((t, GLA_V_W), lambda b: (row0 + b, 0)),
            pl.BlockSpec((None, 2, GLA_DV, GLA_QK_W), lambda b: (jnp.minimum(b, last), 0, 0, 0)),
        ],
        out_shape=[
            jax.ShapeDtypeStruct((geom.n_tok, GLA_V_W), F32),
            jax.ShapeDtypeStruct((n_seq, 2, GLA_DV, GLA_QK_W), F32),
        ],
        scratch_shapes=[
            pltpu.VMEM((t, 2 * GLA_QK_W), F32),
            pltpu.VMEM((t, GLA_V_W), F32),
            pltpu.VMEM((2, t, GLA_QK_W), BF16),
            pltpu.VMEM((2, t // GLA_CHUNK, GLA_DV, GLA_QK_W), F32),
            pltpu.VMEM((2, t // GLA_CHUNK, GLA_DV, GLA_QK_W), F32),
            pltpu.VMEM((2, t // GLA_CHUNK, SUBLANES, GLA_QK_W), F32),
        ],
        input_output_aliases={len(args) - 1: 0} if has_s0 else {},
        compiler_params=_params(("arbitrary",)),
        name="gla_lat" if latent else "gla_ctx",
    )(*args)


def _merge_body(geom, x_ref, gt_ref, cv_ref, cvp_ref, cvn_ref, yb_ref, yc_ref, wbr_ref, wout_ref, cw_ref,
                g1_ref, sc2_ref, sh2_ref, lng_ref, lnb_ref, rwh_ref, rwl_ref, rb_ref,
                x1_ref, h2_ref, tidx_ref, tgate_ref, trank_ref, tcount_ref, *, dn_alpha):
    i = pl.program_id(0)
    tm = x_ref.shape[0]
    ctx_tiles = geom.tok_ctx // tm
    per_seq = jnp.where(i < ctx_tiles, geom.t_ctx // tm, geom.t_lat // tm)
    j = jnp.where(i < ctx_tiles, i, i - ctx_tiles) % per_seq
    has_prev = (j > 0).astype(F32)
    has_next = (j < per_seq - 1).astype(F32)

    cv = cv_ref[...]
    u, bg, cg = cv[:, :CONV_W], cv[:, CONV_W:2 * CONV_W], cv[:, 2 * CONV_W:]
    v = cg * u
    pv = cvp_ref[SUBLANES - 1:SUBLANES, :]
    nv = cvn_ref[0:1, :]
    v_before = has_prev * (pv[:, 2 * CONV_W:] * pv[:, :CONV_W])
    v_after = has_next * (nv[:, 2 * CONV_W:] * nv[:, :CONV_W])
    row = lax.broadcasted_iota(jnp.int32, (tm, 1), 0)
    v_prev = jnp.where(row == 0, v_before, pltpu.roll(v, 1, 0))
    v_next = jnp.where(row == tm - 1, v_after, pltpu.roll(v, tm - 1, 0))
    cw = cw_ref[...]
    y_a = bg * (cw[0:1, :] * v_prev + cw[1:2, :] * v + cw[2:3, :] * v_next)

    gt = gt_ref[...]
    merged = (_sigmoid(gt[:, :D_MODEL]) * jnp.dot(y_a.astype(BF16), wbr_ref[0], preferred_element_type=F32)
              + _sigmoid(gt[:, D_MODEL:2 * D_MODEL])
              * jnp.dot(yb_ref[...].astype(BF16), wbr_ref[1], preferred_element_type=F32)
              + _sigmoid(gt[:, 2 * D_MODEL:])
              * jnp.dot(yc_ref[...].astype(BF16), wbr_ref[2], preferred_element_type=F32))
    mix = jnp.dot(merged.astype(BF16), wout_ref[...], preferred_element_type=F32)
    x1 = _layernorm_rows(dn_alpha * x_ref[...] + g1_ref[...] * mix, lng_ref[...], lnb_ref[...])
    x1_ref[...] = x1
    h2 = x1 * (1.0 + sc2_ref[...]) + sh2_ref[...]
    _store_row_tiles(h2_ref, h2)

    h_hi = h2.astype(BF16)
    h_lo = (h2 - h_hi.astype(F32)).astype(BF16)
    logits = (jnp.dot(h_hi, rwh_ref[...], preferred_element_type=F32)
              + jnp.dot(h_lo, rwh_ref[...], preferred_element_type=F32)
              + jnp.dot(h_hi, rwl_ref[...], preferred_element_type=F32)) + rb_ref[...]
    lane = lax.broadcasted_iota(jnp.int32, logits.shape, 1).astype(F32)
    vals, top_v, top_i = logits, [], []
    for _ in range(TOP_K):
        m = jnp.max(vals, axis=-1, keepdims=True)
        am = jnp.min(jnp.where(vals == m, lane, float(N_EXPERTS)), axis=-1, keepdims=True)
        top_v.append(m)
        top_i.append(am)
        vals = jnp.where(lane == am, -jnp.inf, vals)
    ex = [jnp.exp(tv - top_v[0]) for tv in top_v]
    den = ex[0] + ex[1] + ex[2] + ex[3]
    slot = lax.broadcasted_iota(jnp.int32, (tm, TOP_K), 1)
    gates, idxs = ex[TOP_K - 1], top_i[TOP_K - 1]
    for k in range(TOP_K - 2, -1, -1):
        gates = jnp.where(slot == k, ex[k], gates)
        idxs = jnp.where(slot == k, top_i[k], idxs)
    tgate_ref[...] = gates / den
    tidx_ref[...] = idxs.astype(jnp.int32)

    onehots = [(lane == ti).astype(F32) for ti in top_i]
    member = onehots[0] + onehots[1] + onehots[2] + onehots[3]
    earlier = (lax.broadcasted_iota(jnp.int32, (tm, tm), 0) > lax.broadcasted_iota(jnp.int32, (tm, tm), 1))
    before = jnp.dot(earlier.astype(BF16), member.astype(BF16), preferred_element_type=F32)
    ranks = jnp.sum(onehots[TOP_K - 1] * before, axis=-1, keepdims=True)
    for k in range(TOP_K - 2, -1, -1):
        ranks = jnp.where(slot == k, jnp.sum(onehots[k] * before, axis=-1, keepdims=True), ranks)
    trank_ref[...] = ranks.astype(jnp.int32)
    tcount_ref[...] = (before[tm - 1:tm, :] + member[tm - 1:tm, :]).astype(jnp.int32)


def _merge(geom, x, p, yb, yc, mod, lw, dn_alpha):
    tm = TOK_TILE
    n = geom.n_tok
    halo_per_tile = tm // SUBLANES
    n_halo = n // SUBLANES
    conv_blk = COL_CONV // (3 * CONV_W)
    const2 = lambda i: (0, 0)
    return pl.pallas_call(
        functools.partial(_merge_body, geom, dn_alpha=dn_alpha),
        grid=(n // tm,),
        in_specs=[
            pl.BlockSpec((tm, D_MODEL), lambda i: (i, 0)),
            pl.BlockSpec((tm, 3 * D_MODEL), lambda i: (i, 0)),
            pl.BlockSpec((tm, 3 * CONV_W), lambda i: (i, conv_blk)),
            pl.BlockSpec((SUBLANES, 3 * CONV_W), lambda i: (jnp.maximum(i * halo_per_tile - 1, 0), conv_blk)),
            pl.BlockSpec((SUBLANES, 3 * CONV_W),
                         lambda i: (jnp.minimum((i + 1) * halo_per_tile, n_halo - 1), conv_blk)),
            pl.BlockSpec((tm, MIX_W), lambda i: (i, 0)),
            pl.BlockSpec((tm, MIX_W), lambda i: (i, 0)),
            pl.BlockSpec((3, MIX_W, D_MODEL), lambda i: (0, 0, 0)),
            pl.BlockSpec((D_MODEL, D_MODEL), const2),
            pl.BlockSpec((SUBLANES, CONV_W), const2),
            _mod_spec(geom, tm, 2), _mod_spec(geom, tm, 4), _mod_spec(geom, tm, 3),
            pl.BlockSpec((1, D_MODEL), const2), pl.BlockSpec((1, D_MODEL), const2),
            pl.BlockSpec((D_MODEL, N_EXPERTS), const2), pl.BlockSpec((D_MODEL, N_EXPERTS), const2),
            pl.BlockSpec((1, N_EXPERTS), const2),
        ],
        out_specs=[
            pl.BlockSpec((tm, D_MODEL), lambda i: (i, 0)),
            pl.BlockSpec((tm * ROW_TILE, LANES), lambda i: (i, 0)),
            pl.BlockSpec((tm, TOP_K), lambda i: (i, 0)),
            pl.BlockSpec((tm, TOP_K), lambda i: (i, 0)),
            pl.BlockSpec((tm, TOP_K), lambda i: (i, 0)),
            pl.BlockSpec((None, 1, N_EXPERTS), lambda i: (i, 0, 0)),
        ],
        out_shape=[
            jax.ShapeDtypeStruct((n, D_MODEL), F32),
            jax.ShapeDtypeStruct((n * ROW_TILE, LANES), F32),
            jax.ShapeDtypeStruct((n, TOP_K), jnp.int32),
            jax.ShapeDtypeStruct((n, TOP_K), F32),
            jax.ShapeDtypeStruct((n, TOP_K), jnp.int32),
            jax.ShapeDtypeStruct((n // tm, 1, N_EXPERTS), jnp.int32),
        ],
        compiler_params=_params(("arbitrary",)),
        name="merge",
    )(x, p, p, p, p, yb, yc, lw["w_branch"], lw["w_out"], lw["conv_w"], mod, mod, mod,
      lw["ln1_g"], lw["ln1_b"], lw["rw_hi"], lw["rw_lo"], lw["r_b"])


def _route(top_idx, rank, tile_counts, bm, tile):
    n_tok = top_idx.shape[0]
    n_tiles = n_tok // tile
    tcnt = tile_counts.reshape(n_tiles, N_EXPERTS)
    counts = jnp.sum(tcnt, axis=0)
    padded = (counts + bm - 1) // bm * bm
    pend = jnp.cumsum(padded)
    pstart = pend - padded
    tile_base = pstart[None, :] + jnp.cumsum(tcnt, axis=0) - tcnt
    tok_base = jnp.broadcast_to(tile_base[:, None, :], (n_tiles, tile, N_EXPERTS)).reshape(n_tok, 1, N_EXPERTS)
    onehot = top_idx[:, :, None] == jnp.arange(N_EXPERTS, dtype=jnp.int32)[None, None, :]
    dest = (jnp.sum(jnp.where(onehot, tok_base, 0), axis=-1) + rank).reshape(-1).astype(jnp.int32)
    n_blocks = n_tok * TOP_K // bm + N_EXPERTS
    blk_start = jnp.arange(n_blocks, dtype=jnp.int32) * bm
    blk_e = jnp.minimum(jnp.sum((pend[None, :] <= blk_start[:, None]).astype(jnp.int32), axis=1), N_EXPERTS - 1)
    n_used = (pend[-1] // bm).astype(jnp.int32).reshape(1)
    return dest, blk_e.astype(jnp.int32), n_used, (pstart + counts).astype(jnp.int32), (padded - counts).astype(
        jnp.int32)


def _row_tile(ref, row):
    return ref.at[pl.ds(pl.multiple_of(row * ROW_TILE, ROW_TILE), ROW_TILE), :]


def _dispatch_body(dest_ref, pad_start_ref, pad_n_ref, n_used_ref, h_ref, xs_hbm, zbuf, sem, zsem, bsem,
                   *, chunk, bm):
    i = pl.program_id(0)
    n_blocks = xs_hbm.shape[0] // (bm * ROW_TILE)

    @pl.when(i == 0)
    def _():
        zbuf[...] = jnp.zeros_like(zbuf)
        zrow = zbuf.at[pl.ds(0, ROW_TILE), :]

        def per_expert(e, total):
            start, cnt = pad_start_ref[e], pad_n_ref[e]

            def one(j, carry):
                pltpu.make_async_copy(zrow, _row_tile(xs_hbm, start + j), zsem).start()
                return carry

            lax.fori_loop(0, cnt, one, 0)
            return total + cnt

        total = lax.fori_loop(0, N_EXPERTS, per_expert, 0)

        def block_copy(b):
            return pltpu.make_async_copy(zbuf, xs_hbm.at[pl.ds(pl.multiple_of(b * bm * ROW_TILE, bm * ROW_TILE),
                                                                bm * ROW_TILE), :], bsem)

        def start_block(b, carry):
            block_copy(b).start()
            return carry

        def wait_block(b, carry):
            block_copy(b).wait()
            return carry

        lax.fori_loop(n_used_ref[0], n_blocks, start_block, 0)

        def wait_one(j, carry):
            pltpu.make_async_copy(zrow, _row_tile(xs_hbm, 0), zsem).wait()
            return carry

        lax.fori_loop(0, total, wait_one, 0)
        lax.fori_loop(n_used_ref[0], n_blocks, wait_block, 0)

    def one_token(r, carry):
        t = i * chunk + r
        for k in range(TOP_K):
            pltpu.make_async_copy(_row_tile(h_ref, r), _row_tile(xs_hbm, dest_ref[t * TOP_K + k]),
                                  sem).start(priority=k % 2)
        return carry

    lax.fori_loop(0, chunk, one_token, 0, unroll=4)
    rows = chunk * TOP_K * ROW_TILE
    pltpu.make_async_copy(xs_hbm.at[pl.ds(0, rows), :], xs_hbm.at[pl.ds(0, rows), :], sem).wait()


def _dispatch(h2_tiles, dest, pad_start, pad_n, n_used, n_blocks):
    chunk, bm = DISPATCH_CHUNK, MOE_ROWS
    n_tok = h2_tiles.shape[0] // ROW_TILE
    grid_spec = pltpu.PrefetchScalarGridSpec(
        num_scalar_prefetch=4,
        grid=(n_tok // chunk,),
        in_specs=[pl.BlockSpec((chunk * ROW_TILE, LANES), lambda i, *_: (i, 0))],
        out_specs=pl.BlockSpec(memory_space=pl.ANY),
        scratch_shapes=[
            pltpu.VMEM((bm * ROW_TILE, LANES), F32),
            pltpu.SemaphoreType.DMA,
            pltpu.SemaphoreType.DMA,
            pltpu.SemaphoreType.DMA,
        ],
    )
    return pl.pallas_call(
        functools.partial(_dispatch_body, chunk=chunk, bm=bm),
        grid_spec=grid_spec,
        out_shape=jax.ShapeDtypeStruct((n_blocks * bm * ROW_TILE, LANES), F32),
        compiler_params=_params(("arbitrary",), disable_bounds_checks=True),
        name="dispatch",
    )(dest, pad_start, pad_n, n_used, h2_tiles)


def _expert_body(blk_e_ref, n_used_ref, first_ref, next_e_ref, parity_ref, xs_ref, wgu_hbm, bgu_ref, wdn_hbm,
                 bdn_ref, y_ref, wgu_f32, wdn_f32, wsem, wgu_bf, wdn_bf, *, layer):
    i = pl.program_id(0)
    bm = y_ref.shape[0] // ROW_TILE
    n_used = n_used_ref[0]

    def weight_copies(e, s):
        return (pltpu.make_async_copy(wgu_hbm.at[layer, e], wgu_f32.at[s], wsem.at[s]),
                pltpu.make_async_copy(wdn_hbm.at[layer, e], wdn_f32.at[s], wsem.at[s]))

    @pl.when(i == 0)
    def _():
        for cp in weight_copies(blk_e_ref[0], 0):
            cp.start()

    @pl.when(i < n_used)
    def _():
        @pl.when(first_ref[i] == 1)
        def _():
            s = parity_ref[i]
            for cp in weight_copies(blk_e_ref[i], s):
                cp.wait()

            @pl.when(next_e_ref[i] >= 0)
            def _():
                for cp in weight_copies(next_e_ref[i], 1 - s):
                    cp.start()

            wgu_bf[...] = wgu_f32[s].astype(BF16)
            wdn_bf[...] = wdn_f32[s].astype(BF16)

        x = _load_row_tiles(xs_ref, bm).astype(BF16)
        acc = jnp.zeros((bm, D_MODEL), F32)
        for f0 in range(0, D_FF, FF_CHUNK):
            g = jnp.dot(x, wgu_bf[:, f0:f0 + FF_CHUNK], preferred_element_type=F32) + bgu_ref[:, f0:f0 + FF_CHUNK]
            u = (jnp.dot(x, wgu_bf[:, D_FF + f0:D_FF + f0 + FF_CHUNK], preferred_element_type=F32)
                 + bgu_ref[:, D_FF + f0:D_FF + f0 + FF_CHUNK])
            g = jnp.minimum(g, SWIGLU_LIMIT)
            u = jnp.clip(u, -SWIGLU_LIMIT, SWIGLU_LIMIT)
            act = (u + 1.0) * (g * _sigmoid(SWIGLU_ALPHA * g))
            acc = acc + jnp.dot(act.astype(BF16), wdn_bf[f0:f0 + FF_CHUNK, :], preferred_element_type=F32)
        _store_row_tiles(y_ref, acc + bdn_ref[...])

    @pl.when(i >= n_used)
    def _():
        y_ref[...] = jnp.zeros_like(y_ref)


def _expert_runs(blk_e, n_used):
    n_blocks = blk_e.shape[0]
    idx = jnp.arange(n_blocks, dtype=jnp.int32)
    before = jnp.concatenate([jnp.full((1,), -1, jnp.int32), blk_e[:-1]])
    first = (blk_e != before) & (idx < n_used[0])
    parity = (jnp.cumsum(first.astype(jnp.int32)) - 1) % 2
    starts = jnp.where(first, idx, n_blocks)
    later = jnp.concatenate([lax.cummin(starts, reverse=True)[1:], jnp.full((1,), n_blocks, jnp.int32)])
    next_e = jnp.where(later < n_blocks, blk_e[jnp.minimum(later, n_blocks - 1)], -1)
    return first.astype(jnp.int32), next_e.astype(jnp.int32), parity.astype(jnp.int32)


def _experts(xs, layer, blk_e, n_used, w_gate_up, b_gate_up, w_down, b_down):
    bm = MOE_ROWS
    n_blocks = blk_e.shape[0]
    first, next_e, parity = _expert_runs(blk_e, n_used)
    grid_spec = pltpu.PrefetchScalarGridSpec(
        num_scalar_prefetch=5,
        grid=(n_blocks,),
        in_specs=[
            pl.BlockSpec((bm * ROW_TILE, LANES), lambda i, be, nu, *_: (jnp.minimum(i, nu[0] - 1), 0)),
            pl.BlockSpec(memory_space=pl.ANY),
            pl.BlockSpec((None, None, 1, 2 * D_FF), lambda i, be, *_: (layer, be[i], 0, 0)),
            pl.BlockSpec(memory_space=pl.ANY),
            pl.BlockSpec((None, None, 1, D_MODEL), lambda i, be, *_: (layer, be[i], 0, 0)),
        ],
        out_specs=pl.BlockSpec((bm * ROW_TILE, LANES), lambda i, *_: (i, 0)),
        scratch_shapes=[
            pltpu.VMEM((2, D_MODEL, 2 * D_FF), F32),
            pltpu.VMEM((2, D_FF, D_MODEL), F32),
            pltpu.SemaphoreType.DMA((2,)),
            pltpu.VMEM((D_MODEL, 2 * D_FF), BF16),
            pltpu.VMEM((D_FF, D_MODEL), BF16),
        ],
    )
    depth = w_gate_up.shape[0]
    return pl.pallas_call(
        functools.partial(_expert_body, layer=layer),
        grid_spec=grid_spec,
        out_shape=jax.ShapeDtypeStruct((n_blocks * bm * ROW_TILE, LANES), F32),
        compiler_params=_params(("arbitrary",)),
        name="experts",
    )(blk_e, n_used, first, next_e, parity, xs, w_gate_up, b_gate_up.reshape(depth, N_EXPERTS, 1, 2 * D_FF),
      w_down, b_down.reshape(depth, N_EXPERTS, 1, D_MODEL))


def _combine_body(geom, dest_ref, y_hbm, x1_ref, gate_ref, g2_ref, lng_ref, lnb_ref, o_ref, buf, sem,
                  *, dn_alpha):
    i = pl.program_id(0)
    n_steps = pl.num_programs(0)
    tc = o_ref.shape[0]

    def issue(step, slot):
        base = step * tc * TOP_K

        def one(r, carry):
            for k in range(TOP_K):
                pltpu.make_async_copy(_row_tile(y_hbm, dest_ref[base + r * TOP_K + k]),
                                      _row_tile(buf.at[slot, k], r), sem.at[slot]).start(priority=k % 2)
            return carry

        lax.fori_loop(0, tc, one, 0, unroll=4)

    @pl.when(i == 0)
    def _():
        issue(0, 0)

    @pl.when(i + 1 < n_steps)
    def _():
        issue(i + 1, (i + 1) % 2)

    slot = i % 2
    for k in range(TOP_K):
        pltpu.make_async_copy(y_hbm.at[pl.ds(0, tc * ROW_TILE), :], buf.at[slot, k], sem.at[slot]).wait()
    gate = gate_ref[...]
    moe = gate[:, 0:1] * _load_row_tiles(buf.at[slot, 0], tc)
    for k in range(1, TOP_K):
        moe = moe + gate[:, k:k + 1] * _load_row_tiles(buf.at[slot, k], tc)
    z = dn_alpha * x1_ref[...] + g2_ref[...] * moe
    o_ref[...] = _layernorm_rows(z, lng_ref[...], lnb_ref[...])


def _combine(geom, y_rows, dest, x1, top_gate, mod, ln_g, ln_b, dn_alpha):
    tc = COMB_TILE
    n = geom.n_tok
    grid_spec = pltpu.PrefetchScalarGridSpec(
        num_scalar_prefetch=1,
        grid=(n // tc,),
        in_specs=[
            pl.BlockSpec(memory_space=pl.ANY),
            pl.BlockSpec((tc, D_MODEL), lambda i, d: (i, 0)),
            pl.BlockSpec((tc, TOP_K), lambda i, d: (i, 0)),
            _mod_spec(geom, tc, 5),
            pl.BlockSpec((1, D_MODEL), lambda i, d: (0, 0)),
            pl.BlockSpec((1, D_MODEL), lambda i, d: (0, 0)),
        ],
        out_specs=pl.BlockSpec((tc, D_MODEL), lambda i, d: (i, 0)),
        scratch_shapes=[
            pltpu.VMEM((2, TOP_K, tc * ROW_TILE, LANES), F32),
            pltpu.SemaphoreType.DMA((2,)),
        ],
    )
    return pl.pallas_call(
        functools.partial(_combine_body, geom, dn_alpha=dn_alpha),
        grid_spec=grid_spec,
        out_shape=jax.ShapeDtypeStruct((n, D_MODEL), F32),
        compiler_params=_params(("arbitrary",), disable_bounds_checks=True),
        name="combine",
    )(dest, y_rows, x1, top_gate, mod, ln_g, ln_b)


def _rearrange_w_in(w_in):
    splits = (CONV_W, CONV_W, CONV_W, ATT_Q_W, ATT_KV_W, ATT_KV_W, GLA_QK_W, GLA_QK_W, GLA_V_W, GLA_V_W,
              GLA_GATE_RANK, GLA_GATE_RANK, D_MODEL, D_MODEL, D_MODEL)
    parts, o = [], 0
    for w in splits:
        parts.append(w_in[..., o:o + w])
        o += w
    u_a, b_a, c_a, q_b, k_b, v_b, q_c, k_c, v_c, r_c, za_f, za_b, gt_a, gt_b, gt_c = parts
    pad = jnp.zeros(w_in.shape[:-1] + (LANES - 2 * GLA_GATE_RANK,), w_in.dtype)
    return jnp.concatenate([gt_a, gt_b, gt_c, u_a, b_a, c_a, q_b, v_c, r_c, q_c, k_c, k_b, v_b, za_f, za_b, pad],
                           axis=-1).astype(BF16)


def kernel(x_prompt, x_sample, cache_k, cache_v, state_gla, c, c_ctx, ada_w, ada_b, w_in, conv_w, attn_sink,
           gla_wa2, gla_ba2, gla_norm_g, w_branch, w_out, ln1_g, ln1_b, ln2_g, ln2_b, router_w, router_b,
           w_gate_up, b_gate_up, w_down, b_down):
    depth = w_in.shape[0]
    n_ctx, t_ctx, _ = x_prompt.shape
    n_lat, t_lat, _ = x_sample.shape
    geom = _Geom(n_ctx, t_ctx, n_lat, t_lat)
    dn_alpha = (2 * depth) ** 0.25

    mod_rows = -(-(n_lat + 1) // SUBLANES) * SUBLANES
    cond = jnp.zeros((mod_rows, D_MODEL), F32).at[:n_lat].set(c).at[n_lat].set(c_ctx)
    mods = _modulation(cond, ada_w, ada_b).reshape(depth, mod_rows, 1, 6 * D_MODEL)

    w_in_bf = _rearrange_w_in(w_in)
    w_branch_bf = w_branch.astype(BF16)
    w_out_bf = w_out.astype(BF16)
    conv_w_pad = jnp.zeros((depth, SUBLANES, CONV_W), F32).at[:, :conv_w.shape[1]].set(conv_w)
    rw_hi = router_w.astype(BF16)
    rw_lo = (router_w - rw_hi.astype(F32)).astype(BF16)
    w2 = jnp.zeros((depth, LANES, 2 * GLA_QK_W), F32)
    w2 = w2.at[:, :GLA_GATE_RANK, :GLA_QK_W].set(gla_wa2[:, 0])
    w2 = w2.at[:, GLA_GATE_RANK:2 * GLA_GATE_RANK, GLA_QK_W:].set(gla_wa2[:, 1]).astype(BF16)
    b2 = gla_ba2.reshape(depth, 1, 2 * GLA_QK_W)
    rope_tabs = _rope_tables(t_lat)
    s0t = jnp.transpose(state_gla, (0, 1, 2, 5, 3, 4)).reshape(n_lat, depth, 2, GLA_DV, GLA_QK_W)
    past = cache_k.shape[2]
    ck = cache_k.reshape(n_lat, depth, past, ATT_KV_W)
    cv = cache_v.reshape(n_lat, depth, past, ATT_KV_W)

    x = jnp.concatenate([x_prompt.reshape(-1, D_MODEL), x_sample.reshape(-1, D_MODEL)], axis=0)
    new_k, new_v, new_s = [], [], []
    for l in range(depth):
        mod = mods[l]
        lw = dict(w_branch=w_branch_bf[l], w_out=w_out_bf[l], conv_w=conv_w_pad[l],
                  ln1_g=ln1_g[l].reshape(1, -1), ln1_b=ln1_b[l].reshape(1, -1),
                  rw_hi=rw_hi[l], rw_lo=rw_lo[l], r_b=router_b[l].reshape(1, -1))
        p = _in_proj(geom, x, mod, w_in_bf[l])
        new_k.append(p[:geom.tok_ctx, COL_KB:COL_KB + ATT_KV_W].reshape(n_ctx, t_ctx, ATT_KV_HEADS, HEAD_DIM))
        new_v.append(p[:geom.tok_ctx, COL_VB:COL_VB + ATT_KV_W].reshape(n_ctx, t_ctx, ATT_KV_HEADS, HEAD_DIM))
        yb = _attn_lat(geom, p, ck[:, l], cv[:, l], rope_tabs, attn_sink[l], _attn_ctx(geom, p, attn_sink[l]))
        g_norm = gla_norm_g[l].reshape(1, GLA_DV)
        yc, s_ctx = _gla(geom, p, False, w2[l], b2[l], g_norm)
        yc, _ = _gla(geom, p, True, w2[l], b2[l], g_norm, s0t[:, l], yc)
        new_s.append(jnp.transpose(s_ctx.reshape(n_ctx, 2, GLA_DV, GLA_HEADS, GLA_DK), (0, 1, 3, 4, 2)))
        x1, h2_tiles, top_idx, top_gate, rank, tile_counts = _merge(geom, x, p, yb, yc, mod, lw, dn_alpha)
        dest, blk_e, n_used, pad_start, pad_n = _route(top_idx, rank, tile_counts, MOE_ROWS, TOK_TILE)
        xs = _dispatch(h2_tiles, dest, pad_start, pad_n, n_used, blk_e.shape[0])
        y_rows = _experts(xs, l, blk_e, n_used, w_gate_up, b_gate_up, w_down, b_down)
        x = _combine(geom, y_rows, dest, x1, top_gate, mod, ln2_g[l].reshape(1, -1), ln2_b[l].reshape(1, -1),
                     dn_alpha)
    y_prompt = x[:geom.tok_ctx].reshape(n_ctx, t_ctx, D_MODEL)
    y_sample = x[geom.tok_ctx:].reshape(n_lat, t_lat, D_MODEL)
    return (y_prompt, y_sample, jnp.stack(new_k, axis=1), jnp.stack(new_v, axis=1), jnp.stack(new_s, axis=1))
```

```python
import functools

import jax
import jax.numpy as jnp
from jax import lax
from jax.experimental import pallas as pl
from jax.experimental.pallas import tpu as pltpu

F32 = jnp.float32
BF16 = jnp.bfloat16
HIGHEST = lax.Precision.HIGHEST

D_MODEL = 1024
MIX_W = D_MODEL // 2
CONV_W = MIX_W
HEAD_DIM = 64
ATT_HEADS = MIX_W // HEAD_DIM
ATT_KV_HEADS = 2
ATT_GROUP = ATT_HEADS // ATT_KV_HEADS
ATT_Q_W = ATT_HEADS * HEAD_DIM
ATT_KV_W = ATT_KV_HEADS * HEAD_DIM
WINDOW = 128
ATT_BLOCK = 128
ATT_SCALE = HEAD_DIM ** -0.5
ROPE_THETA = 10000.0
NEG_INF = -1e30
GRID_W = 64
GLA_HEADS = 4
GLA_DK = MIX_W // 2 // GLA_HEADS
GLA_DV = MIX_W // GLA_HEADS
GLA_QK_W = GLA_HEADS * GLA_DK
GLA_V_W = GLA_HEADS * GLA_DV
GLA_GATE_RANK = 16
GLA_GATE_NORM = 16.0
GLA_CHUNK = 64
N_EXPERTS = 32
TOP_K = 4
D_FF = D_MODEL
SWIGLU_LIMIT = 7.0
SWIGLU_ALPHA = 1.702
LN_EPS = 1e-5

LANES = 128
SUBLANES = 8
BF16_SUBLANES = 16
VMEM_LIMIT_BYTES = 56 * 1024 * 1024

COL_GATES = 0
COL_CONV = 3 * D_MODEL
COL_QB = COL_CONV + 3 * CONV_W
COL_VC = COL_QB + ATT_Q_W
COL_RC = COL_VC + GLA_V_W
COL_QC = COL_RC + GLA_V_W
COL_KC = COL_QC + GLA_QK_W
COL_KB = COL_KC + GLA_QK_W
COL_VB = COL_KB + ATT_KV_W
COL_ZA = COL_VB + ATT_KV_W
P_COLS = COL_ZA + LANES

TOK_TILE = 256
MOE_ROWS = 256
COMB_TILE = 256
DISPATCH_CHUNK = 1024
FF_CHUNK = 1024

ROW_TILE = D_MODEL // LANES


def _params(semantics, **kw):
    return pltpu.CompilerParams(dimension_semantics=semantics, vmem_limit_bytes=VMEM_LIMIT_BYTES, **kw)


def _sigmoid(x):
    return 1.0 / (1.0 + jnp.exp(-x))


def _layernorm_rows(z, g, b):
    mu = jnp.mean(z, axis=-1, keepdims=True)
    d = z - mu
    var = jnp.mean(d * d, axis=-1, keepdims=True)
    return d * lax.rsqrt(var + LN_EPS) * g + b


def _store_row_tiles(ref, x):
    rows = x.shape[0]
    for j in range(ROW_TILE):
        ref[pl.ds(j, rows, stride=ROW_TILE), :] = x[:, j * LANES:(j + 1) * LANES]


def _load_row_tiles(ref, rows):
    return jnp.concatenate([ref[pl.ds(j, rows, stride=ROW_TILE), :] for j in range(ROW_TILE)], axis=1)


def _ada_body(c_ref, w_ref, b_ref, o_ref):
    c = c_ref[...]
    s = c * _sigmoid(c)
    o_ref[...] = jnp.dot(s, w_ref[...], precision=HIGHEST, preferred_element_type=F32) + b_ref[...]


def _modulation(cond, ada_w, ada_b):
    depth, d, n6 = ada_w.shape
    rows = cond.shape[0]
    tn = 1536
    return pl.pallas_call(
        _ada_body,
        grid=(depth, n6 // tn),
        in_specs=[
            pl.BlockSpec((rows, d), lambda l, j: (0, 0)),
            pl.BlockSpec((None, d, tn), lambda l, j: (l, 0, j)),
            pl.BlockSpec((None, 1, tn), lambda l, j: (l, 0, j)),
        ],
        out_specs=pl.BlockSpec((None, rows, tn), lambda l, j: (l, 0, j)),
        out_shape=jax.ShapeDtypeStruct((depth, rows, n6), F32),
        compiler_params=_params(("arbitrary", "arbitrary")),
        name="modulation",
    )(cond, ada_w, ada_b.reshape(depth, 1, n6))


class _Geom:
    def __init__(self, n_ctx, t_ctx, n_lat, t_lat):
        self.n_ctx, self.t_ctx, self.n_lat, self.t_lat = n_ctx, t_ctx, n_lat, t_lat
        self.tok_ctx = n_ctx * t_ctx
        self.tok_lat = n_lat * t_lat
        self.n_tok = self.tok_ctx + self.tok_lat
        self.ctx_row = n_lat
        assert t_ctx % TOK_TILE == 0 and t_lat % TOK_TILE == 0 and self.tok_ctx % t_lat == 0

    def seq_of_tile(self, i, tile):
        ctx_tiles = self.tok_ctx // tile
        return jnp.where(i < ctx_tiles, self.ctx_row, (i - ctx_tiles) // (self.t_lat // tile))


def _mod_spec(geom, tile, which):
    return pl.BlockSpec((None, 1, D_MODEL), lambda i, *_: (geom.seq_of_tile(i, tile), 0, which))


IN_CHUNK = 1408


def _inproj_body(x_ref, sc_ref, sh_ref, w_ref, o_ref):
    h = (x_ref[...] * (1.0 + sc_ref[...]) + sh_ref[...]).astype(BF16)
    for c0 in range(0, P_COLS, IN_CHUNK):
        o_ref[:, c0:c0 + IN_CHUNK] = jnp.dot(h, w_ref[:, c0:c0 + IN_CHUNK],
                                             preferred_element_type=F32).astype(o_ref.dtype)


def _in_proj(geom, x, mod, w_in_bf):
    tm = TOK_TILE
    return pl.pallas_call(
        _inproj_body,
        grid=(geom.n_tok // tm,),
        in_specs=[
            pl.BlockSpec((tm, D_MODEL), lambda i: (i, 0)),
            _mod_spec(geom, tm, 1),
            _mod_spec(geom, tm, 0),
            pl.BlockSpec((D_MODEL, P_COLS), lambda i: (0, 0), pipeline_mode=pl.Buffered(1)),
        ],
        out_specs=pl.BlockSpec((tm, P_COLS), lambda i: (i, 0)),
        out_shape=jax.ShapeDtypeStruct((geom.n_tok, P_COLS), BF16),
        compiler_params=_params(("arbitrary",)),
        name="in_proj",
    )(x, mod, mod, w_in_bf)


def _sink_column(sink_ref, g, rows):
    return jnp.concatenate(
        [jnp.full((rows, 1), sink_ref[g * ATT_GROUP + h], F32) for h in range(ATT_GROUP)], axis=0)


def _stack_heads(q, g):
    return jnp.concatenate(
        [q[:, (g * ATT_GROUP + h) * HEAD_DIM:(g * ATT_GROUP + h + 1) * HEAD_DIM] for h in range(ATT_GROUP)],
        axis=0)


def _softmax_pv(score_blocks, value_blocks, sink_col):
    m = sink_col
    for s in score_blocks:
        m = jnp.maximum(m, jnp.max(s, axis=-1, keepdims=True))
    den = jnp.exp(sink_col - m)
    acc = None
    for s, v in zip(score_blocks, value_blocks):
        p = jnp.exp(s - m)
        den = den + jnp.sum(p, axis=-1, keepdims=True)
        pv = jnp.dot(p.astype(BF16), v, preferred_element_type=F32)
        acc = pv if acc is None else acc + pv
    return acc / den


def _nt_dot(a, b):
    return lax.dot_general(a, b, (((1,), (1,)), ((), ())), preferred_element_type=F32)


def _attn_ctx_body(q_ref, k_ref, v_ref, sink_ref, o_ref, *, n_seq):
    t = q_ref.shape[0]

    @pl.when(pl.program_id(0) < n_seq)
    def _():
        q = q_ref[...].astype(F32) * ATT_SCALE
        k = k_ref[...]
        v = v_ref[...]
        outs = []
        for g in range(ATT_KV_HEADS):
            kg = k[:, g * HEAD_DIM:(g + 1) * HEAD_DIM]
            vg = v[:, g * HEAD_DIM:(g + 1) * HEAD_DIM]
            qs = _stack_heads(q, g).astype(BF16)
            o = _softmax_pv([_nt_dot(qs, kg)], [vg], _sink_column(sink_ref, g, t))
            outs += [o[h * t:(h + 1) * t] for h in range(ATT_GROUP)]
        o_ref[...] = jnp.concatenate(outs, axis=1)

    @pl.when(pl.program_id(0) >= n_seq)
    def _():
        o_ref[...] = jnp.zeros_like(o_ref)


def _attn_ctx(geom, p, sink):
    t = geom.t_ctx
    last = geom.n_ctx - 1
    return pl.pallas_call(
        functools.partial(_attn_ctx_body, n_seq=geom.n_ctx),
        grid=(geom.n_tok // t,),
        in_specs=[
            pl.BlockSpec((t, ATT_Q_W), lambda b: (jnp.minimum(b, last), COL_QB // ATT_Q_W)),
            pl.BlockSpec((t, ATT_KV_W), lambda b: (jnp.minimum(b, last), COL_KB // ATT_KV_W)),
            pl.BlockSpec((t, ATT_KV_W), lambda b: (jnp.minimum(b, last), COL_VB // ATT_KV_W)),
            pl.BlockSpec(memory_space=pltpu.SMEM),
        ],
        out_specs=pl.BlockSpec((t, ATT_Q_W), lambda b: (b, 0)),
        out_shape=jax.ShapeDtypeStruct((geom.n_tok, ATT_Q_W), F32),
        compiler_params=_params(("arbitrary",)),
        name="attn_ctx",
    )(p, p, p, sink)


def _rope(x, cos, sin_lo, sin_hi):
    nf = HEAD_DIM // 4
    outs = []
    for j in range(x.shape[1] // LANES):
        xj = x[:, j * LANES:(j + 1) * LANES]
        outs.append(xj * cos + pltpu.roll(xj, LANES - nf, 1) * sin_lo + pltpu.roll(xj, nf, 1) * sin_hi)
    return outs[0] if len(outs) == 1 else jnp.concatenate(outs, axis=1)


def _attn_lat_body(q_ref, k_ref, v_ref, ck_ref, cv_ref, qcos_ref, qslo_ref, qshi_ref,
                   kcos_ref, kslo_ref, kshi_ref, sink_ref, yb_hbm, o_ref):
    del yb_hbm
    i = pl.program_id(1)
    nb = k_ref.shape[0] // ATT_BLOCK
    blk = ATT_BLOCK
    q = _rope(q_ref[...].astype(F32), qcos_ref[...], qslo_ref[...], qshi_ref[...]) * ATT_SCALE
    q_in_blk = lax.broadcasted_iota(jnp.int32, (ATT_GROUP * blk, blk), 0) % blk
    k_in_blk = lax.broadcasted_iota(jnp.int32, (ATT_GROUP * blk, blk), 1)
    k_win, v_win, masks = [], [], []
    for j in (-1, 0, 1):
        kb = i + j
        kbc = jnp.clip(kb, 0, nb - 1)
        rows = pl.ds(pl.multiple_of(kbc * blk, blk), blk)
        k_win.append(_rope(k_ref[rows, :].astype(F32), kcos_ref[rows, :], kslo_ref[rows, :],
                           kshi_ref[rows, :]).astype(BF16))
        v_win.append(v_ref[rows, :])
        outside = jnp.where((kb >= 0) & (kb < nb), 0, 4 * WINDOW)
        dist = (j * blk + k_in_blk) - q_in_blk
        masks.append(jnp.abs(dist) + outside <= WINDOW)
    ck = ck_ref[...].astype(BF16)
    cv = cv_ref[...].astype(BF16)
    outs = []
    for g in range(ATT_KV_HEADS):
        hs = slice(g * HEAD_DIM, (g + 1) * HEAD_DIM)
        qs = _stack_heads(q, g).astype(BF16)
        scores = [jnp.where(mk, _nt_dot(qs, kw[:, hs]), NEG_INF) for kw, mk in zip(k_win, masks)]
        scores.append(_nt_dot(qs, ck[:, hs]))
        values = [vw[:, hs] for vw in v_win] + [cv[:, hs]]
        o = _softmax_pv(scores, values, _sink_column(sink_ref, g, blk))
        outs += [o[h * blk:(h + 1) * blk] for h in range(ATT_GROUP)]
    o_ref[...] = jnp.concatenate(outs, axis=1)


def _attn_lat(geom, p, cache_k, cache_v, rope_tabs, sink, yb):
    t, blk = geom.t_lat, ATT_BLOCK
    nb = t // blk
    past = cache_k.shape[1]
    q_row0 = geom.tok_ctx // blk
    s_row0 = geom.tok_ctx // t
    cos, slo, shi = rope_tabs
    q_tab = pl.BlockSpec((blk, LANES), lambda b, i: (i, 0))
    k_tab = pl.BlockSpec((t, LANES), lambda b, i: (0, 0))
    return pl.pallas_call(
        _attn_lat_body,
        grid=(geom.n_lat, nb),
        in_specs=[
            pl.BlockSpec((blk, ATT_Q_W), lambda b, i: (q_row0 + b * nb + i, COL_QB // ATT_Q_W)),
            pl.BlockSpec((t, ATT_KV_W), lambda b, i: (s_row0 + b, COL_KB // ATT_KV_W)),
            pl.BlockSpec((t, ATT_KV_W), lambda b, i: (s_row0 + b, COL_VB // ATT_KV_W)),
            pl.BlockSpec((None, past, ATT_KV_W), lambda b, i: (b, 0, 0)),
            pl.BlockSpec((None, past, ATT_KV_W), lambda b, i: (b, 0, 0)),
            q_tab, q_tab, q_tab, k_tab, k_tab, k_tab,
            pl.BlockSpec(memory_space=pltpu.SMEM),
            pl.BlockSpec(memory_space=pl.ANY),
        ],
        out_specs=pl.BlockSpec((blk, ATT_Q_W), lambda b, i: (q_row0 + b * nb + i, 0)),
        out_shape=jax.ShapeDtypeStruct((geom.n_tok, ATT_Q_W), F32),
        input_output_aliases={12: 0},
        compiler_params=_params(("arbitrary", "arbitrary")),
        name="attn_lat",
    )(p, p, p, cache_k, cache_v, cos, slo, shi, cos, slo, shi, sink, yb)


def _rope_tables(t_lat):
    half = HEAD_DIM // 2
    nf = half // 2
    tok = jnp.arange(t_lat)
    lane = jnp.arange(LANES) % HEAD_DIM
    pos = jnp.where(lane[None, :] < half, (tok // GRID_W)[:, None], (tok % GRID_W)[:, None]).astype(F32)
    inv = ROPE_THETA ** (-(lane % nf).astype(F32) / nf)
    ang = pos * inv[None, :]
    first = (lane % half) < nf
    cos = jnp.cos(ang)
    sin = jnp.sin(ang)
    sin_lo = jnp.where(first[None, :], -sin, 0.0)
    sin_hi = jnp.where(first[None, :], 0.0, sin)
    return cos, sin_lo, sin_hi


def _gla_body(*refs, has_s0, n_seq):
    y_ref = refs[10 if has_s0 else 8]

    @pl.when(pl.program_id(0) < n_seq)
    def _():
        _gla_sequence(*refs, has_s0=has_s0)

    @pl.when(pl.program_id(0) >= n_seq)
    def _():
        y_ref[...] = jnp.zeros_like(y_ref)


def _gla_sequence(*refs, has_s0):
    if has_s0:
        (q_ref, k_ref, v_ref, r_ref, za_ref, w2_ref, b2_ref, g_ref, s0_ref, _yc_hbm,
         y_ref, sfin_ref, bc_scr, o_scr, qe_scr, upd_scr, st_scr, last_scr) = refs
    else:
        (q_ref, k_ref, v_ref, r_ref, za_ref, w2_ref, b2_ref, g_ref,
         y_ref, sfin_ref, bc_scr, o_scr, qe_scr, upd_scr, st_scr, last_scr) = refs
    t = q_ref.shape[0]
    c, nh = GLA_CHUNK, GLA_HEADS
    n = t // c
    rb = nh * c

    def iota(shape, axis):
        return lax.broadcasted_iota(jnp.int32, shape, axis)

    z = jnp.dot(za_ref[...].astype(BF16), w2_ref[...], preferred_element_type=F32) + b2_ref[...]
    la = (jnp.minimum(z, 0.0) - jnp.log1p(jnp.exp(-jnp.abs(z)))) * (1.0 / GLA_GATE_NORM)
    same_chunk = (iota((rb, rb), 0) // c) == (iota((rb, rb), 1) // c)
    cum = (jnp.where(same_chunk & (iota((rb, rb), 0) >= iota((rb, rb), 1)), 1.0, 0.0).astype(BF16),
           jnp.where(same_chunk & (iota((rb, rb), 0) <= iota((rb, rb), 1)), 1.0, 0.0).astype(BF16))
    for r0 in range(0, t, rb):
        for d in (0, 1):
            rest = la[r0:r0 + rb, d * GLA_QK_W:(d + 1) * GLA_QK_W]
            bc = None
            for _ in range(3):
                term = rest.astype(BF16)
                rest = rest - term.astype(F32)
                part = jnp.dot(cum[d], term, preferred_element_type=F32)
                bc = part if bc is None else bc + part
            bc_scr[r0:r0 + rb, d * GLA_QK_W:(d + 1) * GLA_QK_W] = bc

    in_chunk = iota((c, nh * c), 1) % c
    keeps = (iota((c, nh * c), 0) >= in_chunk, iota((c, nh * c), 0) <= in_chunk)
    own_k = (iota((nh * c, GLA_QK_W), 0) // c) == (iota((nh * c, GLA_QK_W), 1) // GLA_DK)
    own_v = (iota((nh * c, GLA_V_W), 0) // c) == (iota((nh * c, GLA_V_W), 1) // GLA_DV)
    own_s = (iota((nh * GLA_DV, GLA_QK_W), 0) // GLA_DV) == (iota((nh * GLA_DV, GLA_QK_W), 1) // GLA_DK)
    lane_head = iota((GLA_DV, GLA_QK_W), 1) // GLA_DK

    def stack(x):
        return jnp.concatenate([x] * nh, axis=0)

    def chunk_rows(ch):
        return pl.ds(pl.multiple_of(ch * c, c), c)

    def local(ch, carry):
        rows = chunk_rows(ch)
        q = q_ref[rows, :].astype(F32) * (GLA_DK ** -0.5)
        k = k_ref[rows, :].astype(F32)
        v = v_ref[rows, :].astype(F32)
        v_bd = jnp.where(own_v, stack(v), 0.0).astype(BF16)
        vt = v.T.astype(BF16)
        o_sum = None
        for d in (0, 1):
            bc = bc_scr[rows, d * GLA_QK_W:(d + 1) * GLA_QK_W]
            last = bc[c - 1:c, :] if d == 0 else bc[0:1, :]
            qe = (q * jnp.exp(bc)).astype(BF16)
            kd = (k * jnp.exp(last - bc)).astype(BF16)
            k_bd = jnp.where(own_k, stack(k * jnp.exp(-bc)), 0.0).astype(BF16)
            qe_scr[d, rows, :] = qe
            last_scr[d, ch] = jnp.broadcast_to(last, (SUBLANES, GLA_QK_W))
            att = jnp.where(keeps[d], _nt_dot(qe, k_bd), 0.0)
            o_d = jnp.dot(att.astype(BF16), v_bd, preferred_element_type=F32)
            u = jnp.dot(vt, kd, preferred_element_type=F32)
            upd = jnp.where(lane_head == 0, u[:GLA_DV], 0.0)
            for h in range(1, nh):
                upd = jnp.where(lane_head == h, u[h * GLA_DV:(h + 1) * GLA_DV], upd)
            upd_scr[d, ch] = upd
            o_sum = o_d if o_sum is None else o_sum + o_d
        o_scr[rows, :] = o_sum
        return carry

    lax.fori_loop(0, n, local, 0, unroll=2)

    for d in (0, 1):
        def scan(step, st, d=d):
            ch = step if d == 0 else n - 1 - step
            st_scr[d, ch] = st
            return st * jnp.exp(last_scr[d, ch][0:1, :]) + upd_scr[d, ch]

        st0 = s0_ref[d] if has_s0 else jnp.zeros((GLA_DV, GLA_QK_W), F32)
        sfin_ref[d] = lax.fori_loop(0, n, scan, st0)

    def cross(ch, carry):
        rows = chunk_rows(ch)
        acc = o_scr[rows, :]
        for d in (0, 1):
            s_bd = jnp.where(own_s, stack(st_scr[d, ch]), 0.0).astype(BF16)
            acc = acc + _nt_dot(qe_scr[d, rows, :], s_bd)
        o_scr[rows, :] = acc
        return carry

    lax.fori_loop(0, n, cross, 0, unroll=2)

    g = g_ref[...]
    rb = 256
    for r0 in range(0, t, rb):
        of = o_scr[r0:r0 + rb, :]
        normed = []
        for h in range(GLA_HEADS):
            oh = of[:, h * GLA_DV:(h + 1) * GLA_DV]
            mu = jnp.mean(oh, axis=-1, keepdims=True)
            d = oh - mu
            var = jnp.mean(d * d, axis=-1, keepdims=True)
            normed.append(d * lax.rsqrt(var + LN_EPS) * g)
        r = r_ref[r0:r0 + rb, :].astype(F32)
        y_ref[r0:r0 + rb, :] = jnp.concatenate(normed, axis=1) * (r * _sigmoid(r))


def _gla(geom, p, latent, w2, b2, norm_g, s0t=None, yc=None):
    if latent:
        t, n_seq, row0 = geom.t_lat, geom.n_lat, geom.tok_ctx // geom.t_lat
    else:
        t, n_seq, row0 = geom.t_ctx, geom.n_ctx, 0
    has_s0 = s0t is not None
    steps = n_seq if latent else geom.n_tok // t
    last = n_seq - 1

    def col(width, off):
        return pl.BlockSpec((t, width), lambda b: (row0 + jnp.minimum(b, last), off // width))

    in_specs = [
        col(GLA_QK_W, COL_QC), col(GLA_QK_W, COL_KC), col(GLA_V_W, COL_VC), col(GLA_V_W, COL_RC),
        col(LANES, COL_ZA),
        pl.BlockSpec((LANES, 2 * GLA_QK_W), lambda b: (0, 0)),
        pl.BlockSpec((1, 2 * GLA_QK_W), lambda b: (0, 0)),
        pl.BlockSpec((1, GLA_DV), lambda b: (0, 0)),
    ]
    args = [p, p, p, p, p, w2, b2, norm_g]
    if has_s0:
        in_specs += [pl.BlockSpec((None, 2, GLA_DV, GLA_QK_W), lambda b: (b, 0, 0, 0)),
                     pl.BlockSpec(memory_space=pl.ANY)]
        args += [s0t, yc]
    return pl.pallas_call(
        functools.partial(_gla_body, has_s0=has_s0, n_seq=n_seq),
        grid=(steps,),
        in_specs=in_specs,
        out_specs=[
            pl.BlockSpec((t, GLA_V_W), lambda b: (row0 + b, 0)),
            pl.BlockSpec((None, 2, GLA_DV, GLA_QK_W), lambda b: (jnp.minimum(b, last), 0, 0, 0)),
        ],
        out_shape=[
            jax.ShapeDtypeStruct((geom.n_tok, GLA_V_W), F32),
            jax.ShapeDtypeStruct((n_seq, 2, GLA_DV, GLA_QK_W), F32),
        ],
        scratch_shapes=[
            pltpu.VMEM((t, 2 * GLA_QK_W), F32),
            pltpu.VMEM((t, GLA_V_W), F32),
            pltpu.VMEM((2, t, GLA_QK_W), BF16),
            pltpu.VMEM((2, t // GLA_CHUNK, GLA_DV, GLA_QK_W), F32),
            pltpu.VMEM((2, t // GLA_CHUNK, GLA_DV, GLA_QK_W), F32),
            pltpu.VMEM((2, t // GLA_CHUNK, SUBLANES, GLA_QK_W), F32),
        ],
        input_output_aliases={len(args) - 1: 0} if has_s0 else {},
        compiler_params=_params(("arbitrary",)),
        name="gla_lat" if latent else "gla_ctx",
    )(*args)


def _merge_body(geom, x_ref, gt_ref, cv_ref, cvp_ref, cvn_ref, yb_ref, yc_ref, wbr_ref, wout_ref, cw_ref,
                g1_ref, sc2_ref, sh2_ref, lng_ref, lnb_ref, rwh_ref, rwl_ref, rb_ref,
                x1_ref, h2_ref, tidx_ref, tgate_ref, trank_ref, tcount_ref, *, dn_alpha):
    i = pl.program_id(0)
    tm = x_ref.shape[0]
    ctx_tiles = geom.tok_ctx // tm
    per_seq = jnp.where(i < ctx_tiles, geom.t_ctx // tm, geom.t_lat // tm)
    j = jnp.where(i < ctx_tiles, i, i - ctx_tiles) % per_seq
    has_prev = (j > 0).astype(F32)
    has_next = (j < per_seq - 1).astype(F32)

    cv = cv_ref[...].astype(F32)
    u, bg, cg = cv[:, :CONV_W], cv[:, CONV_W:2 * CONV_W], cv[:, 2 * CONV_W:]
    v = cg * u
    halo = cvp_ref.shape[0]
    pv = cvp_ref[...].astype(F32)[halo - 1:halo, :]
    nv = cvn_ref[...].astype(F32)[0:1, :]
    v_before = has_prev * (pv[:, 2 * CONV_W:] * pv[:, :CONV_W])
    v_after = has_next * (nv[:, 2 * CONV_W:] * nv[:, :CONV_W])
    row = lax.broadcasted_iota(jnp.int32, (tm, 1), 0)
    v_prev = jnp.where(row == 0, v_before, pltpu.roll(v, 1, 0))
    v_next = jnp.where(row == tm - 1, v_after, pltpu.roll(v, tm - 1, 0))
    cw = cw_ref[...]
    y_a = bg * (cw[0:1, :] * v_prev + cw[1:2, :] * v + cw[2:3, :] * v_next)

    gt = gt_ref[...].astype(F32)
    merged = (_sigmoid(gt[:, :D_MODEL]) * jnp.dot(y_a.astype(BF16), wbr_ref[0], preferred_element_type=F32)
              + _sigmoid(gt[:, D_MODEL:2 * D_MODEL])
              * jnp.dot(yb_ref[...].astype(BF16), wbr_ref[1], preferred_element_type=F32)
              + _sigmoid(gt[:, 2 * D_MODEL:])
              * jnp.dot(yc_ref[...].astype(BF16), wbr_ref[2], preferred_element_type=F32))
    mix = jnp.dot(merged.astype(BF16), wout_ref[...], preferred_element_type=F32)
    x1 = _layernorm_rows(dn_alpha * x_ref[...] + g1_ref[...] * mix, lng_ref[...], lnb_ref[...])
    x1_ref[...] = x1
    h2 = x1 * (1.0 + sc2_ref[...]) + sh2_ref[...]
    _store_row_tiles(h2_ref, h2)

    h_hi = h2.astype(BF16)
    h_lo = (h2 - h_hi.astype(F32)).astype(BF16)
    logits = (jnp.dot(h_hi, rwh_ref[...], preferred_element_type=F32)
              + jnp.dot(h_lo, rwh_ref[...], preferred_element_type=F32)
              + jnp.dot(h_hi, rwl_ref[...], preferred_element_type=F32)) + rb_ref[...]
    lane = lax.broadcasted_iota(jnp.int32, logits.shape, 1).astype(F32)
    vals, top_v, top_i = logits, [], []
    for _ in range(TOP_K):
        m = jnp.max(vals, axis=-1, keepdims=True)
        am = jnp.min(jnp.where(vals == m, lane, float(N_EXPERTS)), axis=-1, keepdims=True)
        top_v.append(m)
        top_i.append(am)
        vals = jnp.where(lane == am, -jnp.inf, vals)
    ex = [jnp.exp(tv - top_v[0]) for tv in top_v]
    den = ex[0] + ex[1] + ex[2] + ex[3]
    slot = lax.broadcasted_iota(jnp.int32, (tm, TOP_K), 1)
    gates, idxs = ex[TOP_K - 1], top_i[TOP_K - 1]
    for k in range(TOP_K - 2, -1, -1):
        gates = jnp.where(slot == k, ex[k], gates)
        idxs = jnp.where(slot == k, top_i[k], idxs)
    tgate_ref[...] = gates / den
    tidx_ref[...] = idxs.astype(jnp.int32)

    onehots = [(lane == ti).astype(F32) for ti in top_i]
    member = onehots[0] + onehots[1] + onehots[2] + onehots[3]
    earlier = (lax.broadcasted_iota(jnp.int32, (tm, tm), 0) > lax.broadcasted_iota(jnp.int32, (tm, tm), 1))
    before = jnp.dot(earlier.astype(BF16), member.astype(BF16), preferred_element_type=F32)
    ranks = jnp.sum(onehots[TOP_K - 1] * before, axis=-1, keepdims=True)
    for k in range(TOP_K - 2, -1, -1):
        ranks = jnp.where(slot == k, jnp.sum(onehots[k] * before, axis=-1, keepdims=True), ranks)
    trank_ref[...] = ranks.astype(jnp.int32)
    tcount_ref[...] = (before[tm - 1:tm, :] + member[tm - 1:tm, :]).astype(jnp.int32)


def _merge(geom, x, p, yb, yc, mod, lw, dn_alpha):
    tm = TOK_TILE
    n = geom.n_tok
    halo = BF16_SUBLANES
    halo_per_tile = tm // halo
    n_halo = n // halo
    conv_blk = COL_CONV // (3 * CONV_W)
    const2 = lambda i: (0, 0)
    return pl.pallas_call(
        functools.partial(_merge_body, geom, dn_alpha=dn_alpha),
        grid=(n // tm,),
        in_specs=[
            pl.BlockSpec((tm, D_MODEL), lambda i: (i, 0)),
            pl.BlockSpec((tm, 3 * D_MODEL), lambda i: (i, 0)),
            pl.BlockSpec((tm, 3 * CONV_W), lambda i: (i, conv_blk)),
            pl.BlockSpec((halo, 3 * CONV_W), lambda i: (jnp.maximum(i * halo_per_tile - 1, 0), conv_blk)),
            pl.BlockSpec((halo, 3 * CONV_W),
                         lambda i: (jnp.minimum((i + 1) * halo_per_tile, n_halo - 1), conv_blk)),
            pl.BlockSpec((tm, MIX_W), lambda i: (i, 0)),
            pl.BlockSpec((tm, MIX_W), lambda i: (i, 0)),
            pl.BlockSpec((3, MIX_W, D_MODEL), lambda i: (0, 0, 0)),
            pl.BlockSpec((D_MODEL, D_MODEL), const2),
            pl.BlockSpec((SUBLANES, CONV_W), const2),
            _mod_spec(geom, tm, 2), _mod_spec(geom, tm, 4), _mod_spec(geom, tm, 3),
            pl.BlockSpec((1, D_MODEL), const2), pl.BlockSpec((1, D_MODEL), const2),
            pl.BlockSpec((D_MODEL, N_EXPERTS), const2), pl.BlockSpec((D_MODEL, N_EXPERTS), const2),
            pl.BlockSpec((1, N_EXPERTS), const2),
        ],
        out_specs=[
            pl.BlockSpec((tm, D_MODEL), lambda i: (i, 0)),
            pl.BlockSpec((tm * ROW_TILE, LANES), lambda i: (i, 0)),
            pl.BlockSpec((tm, TOP_K), lambda i: (i, 0)),
            pl.BlockSpec((tm, TOP_K), lambda i: (i, 0)),
            pl.BlockSpec((tm, TOP_K), lambda i: (i, 0)),
            pl.BlockSpec((None, 1, N_EXPERTS), lambda i: (i, 0, 0)),
        ],
        out_shape=[
            jax.ShapeDtypeStruct((n, D_MODEL), F32),
            jax.ShapeDtypeStruct((n * ROW_TILE, LANES), F32),
            jax.ShapeDtypeStruct((n, TOP_K), jnp.int32),
            jax.ShapeDtypeStruct((n, TOP_K), F32),
            jax.ShapeDtypeStruct((n, TOP_K), jnp.int32),
            jax.ShapeDtypeStruct((n // tm, 1, N_EXPERTS), jnp.int32),
        ],
        compiler_params=_params(("arbitrary",)),
        name="merge",
    )(x, p, p, p, p, yb, yc, lw["w_branch"], lw["w_out"], lw["conv_w"], mod, mod, mod,
      lw["ln1_g"], lw["ln1_b"], lw["rw_hi"], lw["rw_lo"], lw["r_b"])


def _route(top_idx, rank, tile_counts, bm, tile):
    n_tok = top_idx.shape[0]
    n_tiles = n_tok // tile
    tcnt = tile_counts.reshape(n_tiles, N_EXPERTS)
    counts = jnp.sum(tcnt, axis=0)
    padded = (counts + bm - 1) // bm * bm
    pend = jnp.cumsum(padded)
    pstart = pend - padded
    tile_base = pstart[None, :] + jnp.cumsum(tcnt, axis=0) - tcnt
    tok_base = jnp.broadcast_to(tile_base[:, None, :], (n_tiles, tile, N_EXPERTS)).reshape(n_tok, 1, N_EXPERTS)
    onehot = top_idx[:, :, None] == jnp.arange(N_EXPERTS, dtype=jnp.int32)[None, None, :]
    dest = (jnp.sum(jnp.where(onehot, tok_base, 0), axis=-1) + rank).reshape(-1).astype(jnp.int32)
    n_blocks = n_tok * TOP_K // bm + N_EXPERTS
    blk_start = jnp.arange(n_blocks, dtype=jnp.int32) * bm
    blk_e = jnp.minimum(jnp.sum((pend[None, :] <= blk_start[:, None]).astype(jnp.int32), axis=1), N_EXPERTS - 1)
    n_used = (pend[-1] // bm).astype(jnp.int32).reshape(1)
    return dest, blk_e.astype(jnp.int32), n_used, (pstart + counts).astype(jnp.int32), (padded - counts).astype(
        jnp.int32)


def _row_tile(ref, row):
    return ref.at[pl.ds(pl.multiple_of(row * ROW_TILE, ROW_TILE), ROW_TILE), :]


def _dispatch_body(dest_ref, pad_start_ref, pad_n_ref, n_used_ref, h_ref, xs_hbm, zbuf, sem, zsem, bsem,
                   *, chunk, bm):
    i = pl.program_id(0)
    n_blocks = xs_hbm.shape[0] // (bm * ROW_TILE)

    @pl.when(i == 0)
    def _():
        zbuf[...] = jnp.zeros_like(zbuf)
        zrow = zbuf.at[pl.ds(0, ROW_TILE), :]

        def per_expert(e, total):
            start, cnt = pad_start_ref[e], pad_n_ref[e]

            def one(j, carry):
                pltpu.make_async_copy(zrow, _row_tile(xs_hbm, start + j), zsem).start()
                return carry

            lax.fori_loop(0, cnt, one, 0)
            return total + cnt

        total = lax.fori_loop(0, N_EXPERTS, per_expert, 0)

        def block_copy(b):
            return pltpu.make_async_copy(zbuf, xs_hbm.at[pl.ds(pl.multiple_of(b * bm * ROW_TILE, bm * ROW_TILE),
                                                                bm * ROW_TILE), :], bsem)

        def start_block(b, carry):
            block_copy(b).start()
            return carry

        def wait_block(b, carry):
            block_copy(b).wait()
            return carry

        lax.fori_loop(n_used_ref[0], n_blocks, start_block, 0)

        def wait_one(j, carry):
            pltpu.make_async_copy(zrow, _row_tile(xs_hbm, 0), zsem).wait()
            return carry

        lax.fori_loop(0, total, wait_one, 0)
        lax.fori_loop(n_used_ref[0], n_blocks, wait_block, 0)

    def one_token(r, carry):
        t = i * chunk + r
        for k in range(TOP_K):
            pltpu.make_async_copy(_row_tile(h_ref, r), _row_tile(xs_hbm, dest_ref[t * TOP_K + k]),
                                  sem).start(priority=k % 2)
        return carry

    lax.fori_loop(0, chunk, one_token, 0, unroll=4)
    rows = chunk * TOP_K * ROW_TILE
    pltpu.make_async_copy(xs_hbm.at[pl.ds(0, rows), :], xs_hbm.at[pl.ds(0, rows), :], sem).wait()


def _dispatch(h2_tiles, dest, pad_start, pad_n, n_used, n_blocks):
    chunk, bm = DISPATCH_CHUNK, MOE_ROWS
    n_tok = h2_tiles.shape[0] // ROW_TILE
    grid_spec = pltpu.PrefetchScalarGridSpec(
        num_scalar_prefetch=4,
        grid=(n_tok // chunk,),
        in_specs=[pl.BlockSpec((chunk * ROW_TILE, LANES), lambda i, *_: (i, 0))],
        out_specs=pl.BlockSpec(memory_space=pl.ANY),
        scratch_shapes=[
            pltpu.VMEM((bm * ROW_TILE, LANES), F32),
            pltpu.SemaphoreType.DMA,
            pltpu.SemaphoreType.DMA,
            pltpu.SemaphoreType.DMA,
        ],
    )
    return pl.pallas_call(
        functools.partial(_dispatch_body, chunk=chunk, bm=bm),
        grid_spec=grid_spec,
        out_shape=jax.ShapeDtypeStruct((n_blocks * bm * ROW_TILE, LANES), F32),
        compiler_params=_params(("arbitrary",), disable_bounds_checks=True),
        name="dispatch",
    )(dest, pad_start, pad_n, n_used, h2_tiles)


def _expert_body(blk_e_ref, n_used_ref, first_ref, next_e_ref, parity_ref, xs_ref, wgu_hbm, bgu_ref, wdn_hbm,
                 bdn_ref, y_ref, wgu_f32, wdn_f32, wsem, wgu_bf, wdn_bf, *, layer):
    i = pl.program_id(0)
    bm = y_ref.shape[0] // ROW_TILE
    n_used = n_used_ref[0]

    def weight_copies(e, s):
        return (pltpu.make_async_copy(wgu_hbm.at[layer, e], wgu_f32.at[s], wsem.at[s]),
                pltpu.make_async_copy(wdn_hbm.at[layer, e], wdn_f32.at[s], wsem.at[s]))

    @pl.when(i == 0)
    def _():
        for cp in weight_copies(blk_e_ref[0], 0):
            cp.start()

    @pl.when(i < n_used)
    def _():
        @pl.when(first_ref[i] == 1)
        def _():
            s = parity_ref[i]
            for cp in weight_copies(blk_e_ref[i], s):
                cp.wait()

            @pl.when(next_e_ref[i] >= 0)
            def _():
                for cp in weight_copies(next_e_ref[i], 1 - s):
                    cp.start()

            wgu_bf[...] = wgu_f32[s].astype(BF16)
            wdn_bf[...] = wdn_f32[s].astype(BF16)

        x = _load_row_tiles(xs_ref, bm).astype(BF16)
        acc = jnp.zeros((bm, D_MODEL), F32)
        for f0 in range(0, D_FF, FF_CHUNK):
            g = jnp.dot(x, wgu_bf[:, f0:f0 + FF_CHUNK], preferred_element_type=F32) + bgu_ref[:, f0:f0 + FF_CHUNK]
            u = (jnp.dot(x, wgu_bf[:, D_FF + f0:D_FF + f0 + FF_CHUNK], preferred_element_type=F32)
                 + bgu_ref[:, D_FF + f0:D_FF + f0 + FF_CHUNK])
            g = jnp.minimum(g, SWIGLU_LIMIT)
            u = jnp.clip(u, -SWIGLU_LIMIT, SWIGLU_LIMIT)
            act = (u + 1.0) * (g * _sigmoid(SWIGLU_ALPHA * g))
            acc = acc + jnp.dot(act.astype(BF16), wdn_bf[f0:f0 + FF_CHUNK, :], preferred_element_type=F32)
        _store_row_tiles(y_ref, acc + bdn_ref[...])

    @pl.when(i >= n_used)
    def _():
        y_ref[...] = jnp.zeros_like(y_ref)


def _expert_runs(blk_e, n_used):
    n_blocks = blk_e.shape[0]
    idx = jnp.arange(n_blocks, dtype=jnp.int32)
    before = jnp.concatenate([jnp.full((1,), -1, jnp.int32), blk_e[:-1]])
    first = (blk_e != before) & (idx < n_used[0])
    parity = (jnp.cumsum(first.astype(jnp.int32)) - 1) % 2
    starts = jnp.where(first, idx, n_blocks)
    later = jnp.concatenate([lax.cummin(starts, reverse=True)[1:], jnp.full((1,), n_blocks, jnp.int32)])
    next_e = jnp.where(later < n_blocks, blk_e[jnp.minimum(later, n_blocks - 1)], -1)
    return first.astype(jnp.int32), next_e.astype(jnp.int32), parity.astype(jnp.int32)


def _experts(xs, layer, blk_e, n_used, w_gate_up, b_gate_up, w_down, b_down):
    bm = MOE_ROWS
    n_blocks = blk_e.shape[0]
    first, next_e, parity = _expert_runs(blk_e, n_used)
    grid_spec = pltpu.PrefetchScalarGridSpec(
        num_scalar_prefetch=5,
        grid=(n_blocks,),
        in_specs=[
            pl.BlockSpec((bm * ROW_TILE, LANES), lambda i, be, nu, *_: (jnp.minimum(i, nu[0] - 1), 0)),
            pl.BlockSpec(memory_space=pl.ANY),
            pl.BlockSpec((None, None, 1, 2 * D_FF), lambda i, be, *_: (layer, be[i], 0, 0)),
            pl.BlockSpec(memory_space=pl.ANY),
            pl.BlockSpec((None, None, 1, D_MODEL), lambda i, be, *_: (layer, be[i], 0, 0)),
        ],
        out_specs=pl.BlockSpec((bm * ROW_TILE, LANES), lambda i, *_: (i, 0)),
        scratch_shapes=[
            pltpu.VMEM((2, D_MODEL, 2 * D_FF), F32),
            pltpu.VMEM((2, D_FF, D_MODEL), F32),
            pltpu.SemaphoreType.DMA((2,)),
            pltpu.VMEM((D_MODEL, 2 * D_FF), BF16),
            pltpu.VMEM((D_FF, D_MODEL), BF16),
        ],
    )
    depth = w_gate_up.shape[0]
    return pl.pallas_call(
        functools.partial(_expert_body, layer=layer),
        grid_spec=grid_spec,
        out_shape=jax.ShapeDtypeStruct((n_blocks * bm * ROW_TILE, LANES), F32),
        compiler_params=_params(("arbitrary",)),
        name="experts",
    )(blk_e, n_used, first, next_e, parity, xs, w_gate_up, b_gate_up.reshape(depth, N_EXPERTS, 1, 2 * D_FF),
      w_down, b_down.reshape(depth, N_EXPERTS, 1, D_MODEL))


def _combine_body(geom, dest_ref, y_hbm, x1_ref, gate_ref, g2_ref, lng_ref, lnb_ref, *rest, dn_alpha, split):
    if split:
        o_ctx_ref, o_lat_ref, buf, sem = rest
        o_ref = o_ctx_ref
    else:
        o_ref, buf, sem = rest
    i = pl.program_id(0)
    n_steps = pl.num_programs(0)
    tc = o_ref.shape[0]

    def issue(step, slot):
        base = step * tc * TOP_K

        def one(r, carry):
            for k in range(TOP_K):
                pltpu.make_async_copy(_row_tile(y_hbm, dest_ref[base + r * TOP_K + k]),
                                      _row_tile(buf.at[slot, k], r), sem.at[slot]).start(priority=k % 2)
            return carry

        lax.fori_loop(0, tc, one, 0, unroll=4)

    @pl.when(i == 0)
    def _():
        issue(0, 0)

    @pl.when(i + 1 < n_steps)
    def _():
        issue(i + 1, (i + 1) % 2)

    slot = i % 2
    for k in range(TOP_K):
        pltpu.make_async_copy(y_hbm.at[pl.ds(0, tc * ROW_TILE), :], buf.at[slot, k], sem.at[slot]).wait()
    gate = gate_ref[...]
    moe = gate[:, 0:1] * _load_row_tiles(buf.at[slot, 0], tc)
    for k in range(1, TOP_K):
        moe = moe + gate[:, k:k + 1] * _load_row_tiles(buf.at[slot, k], tc)
    z = dn_alpha * x1_ref[...] + g2_ref[...] * moe
    out = _layernorm_rows(z, lng_ref[...], lnb_ref[...])
    if split:
        ctx_steps = geom.tok_ctx // tc

        @pl.when(i < ctx_steps)
        def _():
            o_ctx_ref[...] = out

        @pl.when(i >= ctx_steps)
        def _():
            o_lat_ref[...] = out
    else:
        o_ref[...] = out


def _combine(geom, y_rows, dest, x1, top_gate, mod, ln_g, ln_b, dn_alpha, split=False):
    tc = COMB_TILE
    n = geom.n_tok
    if split:
        ctx_steps = geom.tok_ctx // tc
        out_specs = [pl.BlockSpec((tc, D_MODEL), lambda i, d: (jnp.minimum(i, ctx_steps - 1), 0)),
                     pl.BlockSpec((tc, D_MODEL), lambda i, d: (jnp.maximum(i - ctx_steps, 0), 0))]
        out_shape = [jax.ShapeDtypeStruct((geom.tok_ctx, D_MODEL), F32),
                     jax.ShapeDtypeStruct((geom.tok_lat, D_MODEL), F32)]
    else:
        out_specs = pl.BlockSpec((tc, D_MODEL), lambda i, d: (i, 0))
        out_shape = jax.ShapeDtypeStruct((n, D_MODEL), F32)
    grid_spec = pltpu.PrefetchScalarGridSpec(
        num_scalar_prefetch=1,
        grid=(n // tc,),
        in_specs=[
            pl.BlockSpec(memory_space=pl.ANY),
            pl.BlockSpec((tc, D_MODEL), lambda i, d: (i, 0)),
            pl.BlockSpec((tc, TOP_K), lambda i, d: (i, 0)),
            _mod_spec(geom, tc, 5),
            pl.BlockSpec((1, D_MODEL), lambda i, d: (0, 0)),
            pl.BlockSpec((1, D_MODEL), lambda i, d: (0, 0)),
        ],
        out_specs=out_specs,
        scratch_shapes=[
            pltpu.VMEM((2, TOP_K, tc * ROW_TILE, LANES), F32),
            pltpu.SemaphoreType.DMA((2,)),
        ],
    )
    return pl.pallas_call(
        functools.partial(_combine_body, geom, dn_alpha=dn_alpha, split=split),
        grid_spec=grid_spec,
        out_shape=out_shape,
        compiler_params=_params(("arbitrary",), disable_bounds_checks=True),
        name="combine",
    )(dest, y_rows, x1, top_gate, mod, ln_g, ln_b)


IN_SPLITS = (CONV_W, CONV_W, CONV_W, ATT_Q_W, ATT_KV_W, ATT_KV_W, GLA_QK_W, GLA_QK_W, GLA_V_W, GLA_V_W,
             GLA_GATE_RANK, GLA_GATE_RANK, D_MODEL, D_MODEL, D_MODEL)


def _w_in_body(w_ref, o_ref):
    w = w_ref[...]
    parts, o = [], 0
    for width in IN_SPLITS:
        parts.append(w[:, o:o + width])
        o += width
    u_a, b_a, c_a, q_b, k_b, v_b, q_c, k_c, v_c, r_c, za_f, za_b, gt_a, gt_b, gt_c = parts
    pad = jnp.zeros((w.shape[0], LANES - 2 * GLA_GATE_RANK), w.dtype)
    o_ref[...] = jnp.concatenate(
        [gt_a, gt_b, gt_c, u_a, b_a, c_a, q_b, v_c, r_c, q_c, k_c, k_b, v_b, za_f, za_b, pad], axis=1).astype(BF16)


def _rearrange_w_in(w_in):
    depth, d, in_cols = w_in.shape
    rows = 256
    return pl.pallas_call(
        _w_in_body,
        grid=(depth, d // rows),
        in_specs=[pl.BlockSpec((None, rows, in_cols), lambda l, i: (l, i, 0))],
        out_specs=pl.BlockSpec((None, rows, P_COLS), lambda l, i: (l, i, 0)),
        out_shape=jax.ShapeDtypeStruct((depth, d, P_COLS), BF16),
        compiler_params=_params(("arbitrary", "arbitrary")),
        name="w_in_layout",
    )(w_in)


def kernel(x_prompt, x_sample, cache_k, cache_v, state_gla, c, c_ctx, ada_w, ada_b, w_in, conv_w, attn_sink,
           gla_wa2, gla_ba2, gla_norm_g, w_branch, w_out, ln1_g, ln1_b, ln2_g, ln2_b, router_w, router_b,
           w_gate_up, b_gate_up, w_down, b_down):
    depth = w_in.shape[0]
    n_ctx, t_ctx, _ = x_prompt.shape
    n_lat, t_lat, _ = x_sample.shape
    geom = _Geom(n_ctx, t_ctx, n_lat, t_lat)
    dn_alpha = (2 * depth) ** 0.25

    mod_rows = -(-(n_lat + 1) // SUBLANES) * SUBLANES
    cond = jnp.zeros((mod_rows, D_MODEL), F32).at[:n_lat].set(c).at[n_lat].set(c_ctx)
    mods = _modulation(cond, ada_w, ada_b).reshape(depth, mod_rows, 1, 6 * D_MODEL)

    w_in_bf = _rearrange_w_in(w_in)
    w_branch_bf = w_branch.astype(BF16)
    w_out_bf = w_out.astype(BF16)
    conv_w_pad = jnp.zeros((depth, SUBLANES, CONV_W), F32).at[:, :conv_w.shape[1]].set(conv_w)
    rw_hi = router_w.astype(BF16)
    rw_lo = (router_w - rw_hi.astype(F32)).astype(BF16)
    w2 = jnp.zeros((depth, LANES, 2 * GLA_QK_W), F32)
    w2 = w2.at[:, :GLA_GATE_RANK, :GLA_QK_W].set(gla_wa2[:, 0])
    w2 = w2.at[:, GLA_GATE_RANK:2 * GLA_GATE_RANK, GLA_QK_W:].set(gla_wa2[:, 1]).astype(BF16)
    b2 = gla_ba2.reshape(depth, 1, 2 * GLA_QK_W)
    rope_tabs = _rope_tables(t_lat)
    s0t = jnp.transpose(state_gla, (0, 1, 2, 5, 3, 4)).reshape(n_lat, depth, 2, GLA_DV, GLA_QK_W)
    past = cache_k.shape[2]
    ck = cache_k.reshape(n_lat, depth, past, ATT_KV_W)
    cv = cache_v.reshape(n_lat, depth, past, ATT_KV_W)

    x = jnp.concatenate([x_prompt.reshape(-1, D_MODEL), x_sample.reshape(-1, D_MODEL)], axis=0)
    new_k, new_v, new_s = [], [], []
    for l in range(depth):
        mod = mods[l]
        lw = dict(w_branch=w_branch_bf[l], w_out=w_out_bf[l], conv_w=conv_w_pad[l],
                  ln1_g=ln1_g[l].reshape(1, -1), ln1_b=ln1_b[l].reshape(1, -1),
                  rw_hi=rw_hi[l], rw_lo=rw_lo[l], r_b=router_b[l].reshape(1, -1))
        p = _in_proj(geom, x, mod, w_in_bf[l])
        kv_shape = (n_ctx, t_ctx, ATT_KV_HEADS, HEAD_DIM)
        new_k.append(p[:geom.tok_ctx, COL_KB:COL_KB + ATT_KV_W].astype(F32).reshape(kv_shape))
        new_v.append(p[:geom.tok_ctx, COL_VB:COL_VB + ATT_KV_W].astype(F32).reshape(kv_shape))
        yb = _attn_lat(geom, p, ck[:, l], cv[:, l], rope_tabs, attn_sink[l], _attn_ctx(geom, p, attn_sink[l]))
        g_norm = gla_norm_g[l].reshape(1, GLA_DV)
        yc, s_ctx = _gla(geom, p, False, w2[l], b2[l], g_norm)
        yc, _ = _gla(geom, p, True, w2[l], b2[l], g_norm, s0t[:, l], yc)
        new_s.append(jnp.transpose(s_ctx.reshape(n_ctx, 2, GLA_DV, GLA_HEADS, GLA_DK), (0, 1, 3, 4, 2)))
        x1, h2_tiles, top_idx, top_gate, rank, tile_counts = _merge(geom, x, p, yb, yc, mod, lw, dn_alpha)
        dest, blk_e, n_used, pad_start, pad_n = _route(top_idx, rank, tile_counts, MOE_ROWS, TOK_TILE)
        xs = _dispatch(h2_tiles, dest, pad_start, pad_n, n_used, blk_e.shape[0])
        y_rows = _experts(xs, l, blk_e, n_used, w_gate_up, b_gate_up, w_down, b_down)
        x = _combine(geom, y_rows, dest, x1, top_gate, mod, ln2_g[l].reshape(1, -1), ln2_b[l].reshape(1, -1),
                     dn_alpha, split=(l == depth - 1))
    y_prompt = x[0].reshape(n_ctx, t_ctx, D_MODEL)
    y_sample = x[1].reshape(n_lat, t_lat, D_MODEL)
    return (y_prompt, y_sample, jnp.stack(new_k, axis=1), jnp.stack(new_v, axis=1), jnp.stack(new_s, axis=1))
```

```python
import functools

import jax
import jax.numpy as jnp
from jax import lax
from jax.experimental import pallas as pl
from jax.experimental.pallas import tpu as pltpu

F32 = jnp.float32
BF16 = jnp.bfloat16
HIGHEST = lax.Precision.HIGHEST

D_MODEL = 1024
MIX_W = D_MODEL // 2
CONV_W = MIX_W
HEAD_DIM = 64
ATT_HEADS = MIX_W // HEAD_DIM
ATT_KV_HEADS = 2
ATT_GROUP = ATT_HEADS // ATT_KV_HEADS
ATT_Q_W = ATT_HEADS * HEAD_DIM
ATT_KV_W = ATT_KV_HEADS * HEAD_DIM
WINDOW = 128
ATT_BLOCK = 128
ATT_SCALE = HEAD_DIM ** -0.5
ROPE_THETA = 10000.0
NEG_INF = -1e30
GRID_W = 64
GLA_HEADS = 4
GLA_DK = MIX_W // 2 // GLA_HEADS
GLA_DV = MIX_W // GLA_HEADS
GLA_QK_W = GLA_HEADS * GLA_DK
GLA_V_W = GLA_HEADS * GLA_DV
GLA_GATE_RANK = 16
GLA_GATE_NORM = 16.0
GLA_CHUNK = 64
N_EXPERTS = 32
TOP_K = 4
D_FF = D_MODEL
SWIGLU_LIMIT = 7.0
SWIGLU_ALPHA = 1.702
LN_EPS = 1e-5

LANES = 128
SUBLANES = 8
BF16_SUBLANES = 16
VMEM_LIMIT_BYTES = 56 * 1024 * 1024

COL_GATES = 0
COL_CONV = 3 * D_MODEL
COL_QB = COL_CONV + 3 * CONV_W
COL_VC = COL_QB + ATT_Q_W
COL_RC = COL_VC + GLA_V_W
COL_QC = COL_RC + GLA_V_W
COL_KC = COL_QC + GLA_QK_W
COL_KB = COL_KC + GLA_QK_W
COL_VB = COL_KB + ATT_KV_W
COL_ZA = COL_VB + ATT_KV_W
P_COLS = COL_ZA + LANES

TOK_TILE = 256
MOE_ROWS = 256
COMB_TILE = 256
DISPATCH_CHUNK = 1024
FF_CHUNK = 1024

ROW_TILE = D_MODEL // LANES


def _params(semantics, **kw):
    return pltpu.CompilerParams(dimension_semantics=semantics, vmem_limit_bytes=VMEM_LIMIT_BYTES, **kw)


def _sigmoid(x):
    return 1.0 / (1.0 + jnp.exp(-x))


def _layernorm_rows(z, g, b):
    mu = jnp.mean(z, axis=-1, keepdims=True)
    d = z - mu
    var = jnp.mean(d * d, axis=-1, keepdims=True)
    return d * lax.rsqrt(var + LN_EPS) * g + b


def _store_row_tiles(ref, x):
    rows = x.shape[0]
    for j in range(ROW_TILE):
        ref[pl.ds(j, rows, stride=ROW_TILE), :] = x[:, j * LANES:(j + 1) * LANES]


def _load_row_tiles(ref, rows):
    return jnp.concatenate([ref[pl.ds(j, rows, stride=ROW_TILE), :] for j in range(ROW_TILE)], axis=1)


def _ada_body(c_ref, w_ref, b_ref, o_ref):
    c = c_ref[...]
    s = c * _sigmoid(c)
    o_ref[...] = jnp.dot(s, w_ref[...], precision=HIGHEST, preferred_element_type=F32) + b_ref[...]


def _modulation(cond, ada_w, ada_b):
    depth, d, n6 = ada_w.shape
    rows = cond.shape[0]
    tn = 1536
    return pl.pallas_call(
        _ada_body,
        grid=(depth, n6 // tn),
        in_specs=[
            pl.BlockSpec((rows, d), lambda l, j: (0, 0)),
            pl.BlockSpec((None, d, tn), lambda l, j: (l, 0, j)),
            pl.BlockSpec((None, 1, tn), lambda l, j: (l, 0, j)),
        ],
        out_specs=pl.BlockSpec((None, rows, tn), lambda l, j: (l, 0, j)),
        out_shape=jax.ShapeDtypeStruct((depth, rows, n6), F32),
        compiler_params=_params(("arbitrary", "arbitrary")),
        name="modulation",
    )(cond, ada_w, ada_b.reshape(depth, 1, n6))


class _Geom:
    def __init__(self, n_ctx, t_ctx, n_lat, t_lat):
        self.n_ctx, self.t_ctx, self.n_lat, self.t_lat = n_ctx, t_ctx, n_lat, t_lat
        self.tok_ctx = n_ctx * t_ctx
        self.tok_lat = n_lat * t_lat
        self.n_tok = self.tok_ctx + self.tok_lat
        self.ctx_row = n_lat
        assert t_ctx % TOK_TILE == 0 and t_lat % TOK_TILE == 0 and self.tok_ctx % t_lat == 0

    def seq_of_tile(self, i, tile):
        ctx_tiles = self.tok_ctx // tile
        return jnp.where(i < ctx_tiles, self.ctx_row, (i - ctx_tiles) // (self.t_lat // tile))


def _mod_spec(geom, tile, which):
    return pl.BlockSpec((None, 1, D_MODEL), lambda i, *_: (geom.seq_of_tile(i, tile), 0, which))


IN_CHUNK = 1408


def _inproj_body(x_ref, sc_ref, sh_ref, w_ref, o_ref):
    h = (x_ref[...] * (1.0 + sc_ref[...]) + sh_ref[...]).astype(BF16)
    for c0 in range(0, P_COLS, IN_CHUNK):
        o_ref[:, c0:c0 + IN_CHUNK] = jnp.dot(h, w_ref[:, c0:c0 + IN_CHUNK],
                                             preferred_element_type=F32).astype(o_ref.dtype)


def _in_proj(geom, x, mod, w_in_bf, layer):
    tm = TOK_TILE
    return pl.pallas_call(
        _inproj_body,
        grid=(geom.n_tok // tm,),
        in_specs=[
            pl.BlockSpec((tm, D_MODEL), lambda i: (i, 0)),
            _mod_spec(geom, tm, 1),
            _mod_spec(geom, tm, 0),
            pl.BlockSpec((None, D_MODEL, P_COLS), lambda i: (layer, 0, 0), pipeline_mode=pl.Buffered(1)),
        ],
        out_specs=pl.BlockSpec((tm, P_COLS), lambda i: (i, 0)),
        out_shape=jax.ShapeDtypeStruct((geom.n_tok, P_COLS), BF16),
        compiler_params=_params(("arbitrary",)),
        name="in_proj",
    )(x, mod, mod, w_in_bf)


def _nt_dot(a, b):
    return lax.dot_general(a, b, (((1,), (1,)), ((), ())), preferred_element_type=F32)


def _fold_sublanes(x, op):
    out = x[0:SUBLANES]
    for r0 in range(SUBLANES, x.shape[0], SUBLANES):
        out = op(out, x[r0:r0 + SUBLANES])
    return out


def _split_kv(k_blocks, v_blocks):
    keys, vts = [], []
    for g in range(ATT_KV_HEADS):
        hs = slice(g * HEAD_DIM, (g + 1) * HEAD_DIM)
        keys.append([kb[:, hs].astype(BF16) for kb in k_blocks])
        vts.append([vb[:, hs].astype(F32).T.astype(BF16) for vb in v_blocks])
    return keys, vts


def _attend(q, keys, vts, biases, sink_ref):
    rows = q.shape[0]
    qb = q.astype(BF16)
    pair_outs = []
    for g in range(ATT_KV_HEADS):
        heads = range(g * ATT_GROUP, (g + 1) * ATT_GROUP)
        q_group = jnp.concatenate([qb[:, h * HEAD_DIM:(h + 1) * HEAD_DIM] for h in heads], axis=0)
        sink = jnp.concatenate([jnp.full((SUBLANES, rows), sink_ref[h], F32) for h in heads], axis=1)
        bias_group = [None if b is None else jnp.concatenate([b] * ATT_GROUP, axis=1) for b in biases]
        scores = []
        m_part = sink
        for kb, bias in zip(keys[g], bias_group):
            s = _nt_dot(kb, q_group)
            if bias is not None:
                s = s + bias
            scores.append(s)
            m_part = jnp.maximum(m_part, _fold_sublanes(s, jnp.maximum))
        m = jnp.max(m_part, axis=0, keepdims=True)
        d_part = jnp.zeros_like(m_part)
        acc = jnp.zeros((HEAD_DIM, ATT_GROUP * rows), F32)
        for s, vt in zip(scores, vts[g]):
            p = jnp.exp(s - m)
            d_part = d_part + _fold_sublanes(p, jnp.add)
            acc = acc + jnp.dot(vt, p.astype(BF16), preferred_element_type=F32)
        den = jnp.sum(d_part, axis=0, keepdims=True) + jnp.exp(sink[0:1] - m)
        out_t = acc / den
        for a in range(0, ATT_GROUP, 2):
            pair_outs.append(jnp.concatenate([out_t[:, a * rows:(a + 1) * rows],
                                              out_t[:, (a + 1) * rows:(a + 2) * rows]], axis=0).T)
    return jnp.concatenate(pair_outs, axis=1)


def _attn_ctx_body(q_ref, k_ref, v_ref, sink_ref, o_ref, *, n_seq):
    t = q_ref.shape[0]

    @pl.when(pl.program_id(0) < n_seq)
    def _():
        blk = ATT_BLOCK
        q = q_ref[...].astype(F32) * ATT_SCALE
        k = k_ref[...]
        v = v_ref[...]
        blocks = [slice(r0, r0 + blk) for r0 in range(0, t, blk)]
        keys, vts = _split_kv([k[b] for b in blocks], [v[b] for b in blocks])
        o_ref[...] = _attend(q, keys, vts, [None] * len(blocks), sink_ref)

    @pl.when(pl.program_id(0) >= n_seq)
    def _():
        o_ref[...] = jnp.zeros_like(o_ref)


def _attn_ctx(geom, p, sink):
    t = geom.t_ctx
    last = geom.n_ctx - 1
    return pl.pallas_call(
        functools.partial(_attn_ctx_body, n_seq=geom.n_ctx),
        grid=(geom.n_tok // t,),
        in_specs=[
            pl.BlockSpec((t, ATT_Q_W), lambda b: (jnp.minimum(b, last), COL_QB // ATT_Q_W)),
            pl.BlockSpec((t, ATT_KV_W), lambda b: (jnp.minimum(b, last), COL_KB // ATT_KV_W)),
            pl.BlockSpec((t, ATT_KV_W), lambda b: (jnp.minimum(b, last), COL_VB // ATT_KV_W)),
            pl.BlockSpec(memory_space=pltpu.SMEM),
        ],
        out_specs=pl.BlockSpec((t, ATT_Q_W), lambda b: (b, 0)),
        out_shape=jax.ShapeDtypeStruct((geom.n_tok, ATT_Q_W), F32),
        compiler_params=_params(("arbitrary",)),
        name="attn_ctx",
    )(p, p, p, sink)


def _rope(x, cos, sin_lo, sin_hi):
    nf = HEAD_DIM // 4
    outs = []
    for j in range(x.shape[1] // LANES):
        xj = x[:, j * LANES:(j + 1) * LANES]
        outs.append(xj * cos + pltpu.roll(xj, LANES - nf, 1) * sin_lo + pltpu.roll(xj, nf, 1) * sin_hi)
    return outs[0] if len(outs) == 1 else jnp.concatenate(outs, axis=1)


def _attn_lat_body(q_ref, k_ref, v_ref, ck_ref, cv_ref, qcos_ref, qslo_ref, qshi_ref,
                   kcos_ref, kslo_ref, kshi_ref, sink_ref, yb_hbm, o_ref):
    del yb_hbm
    i = pl.program_id(1)
    nb = k_ref.shape[0] // ATT_BLOCK
    blk = ATT_BLOCK
    q = _rope(q_ref[...].astype(F32), qcos_ref[...], qslo_ref[...], qshi_ref[...]) * ATT_SCALE
    k_in_blk = lax.broadcasted_iota(jnp.int32, (blk, blk), 0)
    q_in_blk = lax.broadcasted_iota(jnp.int32, (blk, blk), 1)
    k_blocks, v_blocks, biases = [], [], []
    for j in (-1, 0, 1):
        kb = i + j
        kbc = jnp.clip(kb, 0, nb - 1)
        rows = pl.ds(pl.multiple_of(kbc * blk, blk), blk)
        k_blocks.append(_rope(k_ref[rows, :].astype(F32), kcos_ref[rows, :], kslo_ref[rows, :], kshi_ref[rows, :]))
        v_blocks.append(v_ref[rows, :])
        outside = jnp.where((kb >= 0) & (kb < nb), 0, 4 * WINDOW)
        dist = (j * blk + k_in_blk) - q_in_blk
        biases.append(jnp.where(jnp.abs(dist) + outside <= WINDOW, 0.0, NEG_INF))
    for r0 in range(0, ck_ref.shape[0], blk):
        k_blocks.append(ck_ref[r0:r0 + blk, :])
        v_blocks.append(cv_ref[r0:r0 + blk, :])
        biases.append(None)
    keys, vts = _split_kv(k_blocks, v_blocks)
    o_ref[...] = _attend(q, keys, vts, biases, sink_ref)


def _attn_lat(geom, p, cache_k, cache_v, rope_tabs, sink, yb):
    t, blk = geom.t_lat, ATT_BLOCK
    nb = t // blk
    past = cache_k.shape[1]
    q_row0 = geom.tok_ctx // blk
    s_row0 = geom.tok_ctx // t
    cos, slo, shi = rope_tabs
    q_tab = pl.BlockSpec((blk, LANES), lambda b, i: (i, 0))
    k_tab = pl.BlockSpec((t, LANES), lambda b, i: (0, 0))
    return pl.pallas_call(
        _attn_lat_body,
        grid=(geom.n_lat, nb),
        in_specs=[
            pl.BlockSpec((blk, ATT_Q_W), lambda b, i: (q_row0 + b * nb + i, COL_QB // ATT_Q_W)),
            pl.BlockSpec((t, ATT_KV_W), lambda b, i: (s_row0 + b, COL_KB // ATT_KV_W)),
            pl.BlockSpec((t, ATT_KV_W), lambda b, i: (s_row0 + b, COL_VB // ATT_KV_W)),
            pl.BlockSpec((None, past, ATT_KV_W), lambda b, i: (b, 0, 0)),
            pl.BlockSpec((None, past, ATT_KV_W), lambda b, i: (b, 0, 0)),
            q_tab, q_tab, q_tab, k_tab, k_tab, k_tab,
            pl.BlockSpec(memory_space=pltpu.SMEM),
            pl.BlockSpec(memory_space=pl.ANY),
        ],
        out_specs=pl.BlockSpec((blk, ATT_Q_W), lambda b, i: (q_row0 + b * nb + i, 0)),
        out_shape=jax.ShapeDtypeStruct((geom.n_tok, ATT_Q_W), F32),
        input_output_aliases={12: 0},
        compiler_params=_params(("arbitrary", "arbitrary")),
        name="attn_lat",
    )(p, p, p, cache_k, cache_v, cos, slo, shi, cos, slo, shi, sink, yb)


def _rope_tables(t_lat):
    half = HEAD_DIM // 2
    nf = half // 2
    tok = jnp.arange(t_lat)
    lane = jnp.arange(LANES) % HEAD_DIM
    pos = jnp.where(lane[None, :] < half, (tok // GRID_W)[:, None], (tok % GRID_W)[:, None]).astype(F32)
    inv = ROPE_THETA ** (-(lane % nf).astype(F32) / nf)
    ang = pos * inv[None, :]
    first = (lane % half) < nf
    cos = jnp.cos(ang)
    sin = jnp.sin(ang)
    sin_lo = jnp.where(first[None, :], -sin, 0.0)
    sin_hi = jnp.where(first[None, :], 0.0, sin)
    return cos, sin_lo, sin_hi


def _gla_body(*refs, has_s0, n_seq):
    y_ref = refs[10 if has_s0 else 8]

    @pl.when(pl.program_id(0) < n_seq)
    def _():
        _gla_sequence(*refs, has_s0=has_s0)

    @pl.when(pl.program_id(0) >= n_seq)
    def _():
        y_ref[...] = jnp.zeros_like(y_ref)


def _gla_sequence(*refs, has_s0):
    if has_s0:
        (q_ref, k_ref, v_ref, r_ref, za_ref, w2_ref, b2_ref, g_ref, s0_ref, _yc_hbm,
         y_ref, sfin_ref, bc_scr, o_scr, qe_scr, upd_scr, st_scr, last_scr) = refs
    else:
        (q_ref, k_ref, v_ref, r_ref, za_ref, w2_ref, b2_ref, g_ref,
         y_ref, sfin_ref, bc_scr, o_scr, qe_scr, upd_scr, st_scr, last_scr) = refs
    t = q_ref.shape[0]
    c, nh = GLA_CHUNK, GLA_HEADS
    n = t // c
    rb = nh * c

    def iota(shape, axis):
        return lax.broadcasted_iota(jnp.int32, shape, axis)

    same_chunk = (iota((rb, rb), 0) // c) == (iota((rb, rb), 1) // c)
    cum = (jnp.where(same_chunk & (iota((rb, rb), 0) >= iota((rb, rb), 1)), 1.0, 0.0).astype(BF16),
           jnp.where(same_chunk & (iota((rb, rb), 0) <= iota((rb, rb), 1)), 1.0, 0.0).astype(BF16))
    for r0 in range(0, t, rb):
        z = jnp.dot(za_ref[r0:r0 + rb, :].astype(BF16), w2_ref[...], preferred_element_type=F32) + b2_ref[...]
        la = (jnp.minimum(z, 0.0) - jnp.log1p(jnp.exp(-jnp.abs(z)))) * (1.0 / GLA_GATE_NORM)
        for d in (0, 1):
            rest = la[:, d * GLA_QK_W:(d + 1) * GLA_QK_W]
            bc = None
            for _ in range(3):
                term = rest.astype(BF16)
                rest = rest - term.astype(F32)
                part = jnp.dot(cum[d], term, preferred_element_type=F32)
                bc = part if bc is None else bc + part
            bc_scr[r0:r0 + rb, d * GLA_QK_W:(d + 1) * GLA_QK_W] = bc

    in_chunk = iota((c, nh * c), 1) % c
    keeps = (iota((c, nh * c), 0) >= in_chunk, iota((c, nh * c), 0) <= in_chunk)
    own_k = (iota((nh * c, GLA_QK_W), 0) // c) == (iota((nh * c, GLA_QK_W), 1) // GLA_DK)
    own_v = (iota((nh * c, GLA_V_W), 0) // c) == (iota((nh * c, GLA_V_W), 1) // GLA_DV)
    own_s = (iota((nh * GLA_DV, GLA_QK_W), 0) // GLA_DV) == (iota((nh * GLA_DV, GLA_QK_W), 1) // GLA_DK)
    lane_head = iota((GLA_DV, GLA_QK_W), 1) // GLA_DK

    def stack(x):
        return jnp.concatenate([x] * nh, axis=0)

    def chunk_rows(ch):
        return pl.ds(pl.multiple_of(ch * c, c), c)

    def local(ch, carry):
        rows = chunk_rows(ch)
        q = q_ref[rows, :].astype(F32) * (GLA_DK ** -0.5)
        k = k_ref[rows, :].astype(F32)
        v = v_ref[rows, :].astype(F32)
        v_bd = jnp.where(own_v, stack(v), 0.0).astype(BF16)
        vt = v.T.astype(BF16)
        o_sum = None
        for d in (0, 1):
            bc = bc_scr[rows, d * GLA_QK_W:(d + 1) * GLA_QK_W]
            last = bc[c - 1:c, :] if d == 0 else bc[0:1, :]
            qe = (q * jnp.exp(bc)).astype(BF16)
            kd = (k * jnp.exp(last - bc)).astype(BF16)
            k_bd = jnp.where(own_k, stack(k * jnp.exp(-bc)), 0.0).astype(BF16)
            qe_scr[d, rows, :] = qe
            last_scr[d, ch] = jnp.broadcast_to(last, (SUBLANES, GLA_QK_W))
            att = jnp.where(keeps[d], _nt_dot(qe, k_bd), 0.0)
            o_d = jnp.dot(att.astype(BF16), v_bd, preferred_element_type=F32)
            u = jnp.dot(vt, kd, preferred_element_type=F32)
            upd = jnp.where(lane_head == 0, u[:GLA_DV], 0.0)
            for h in range(1, nh):
                upd = jnp.where(lane_head == h, u[h * GLA_DV:(h + 1) * GLA_DV], upd)
            upd_scr[d, ch] = upd
            o_sum = o_d if o_sum is None else o_sum + o_d
        o_scr[rows, :] = o_sum
        return carry

    lax.fori_loop(0, n, local, 0, unroll=2)

    for d in (0, 1):
        def scan(step, st, d=d):
            ch = step if d == 0 else n - 1 - step
            st_scr[d, ch] = st
            return st * jnp.exp(last_scr[d, ch][0:1, :]) + upd_scr[d, ch]

        st0 = s0_ref[d] if has_s0 else jnp.zeros((GLA_DV, GLA_QK_W), F32)
        sfin_ref[d] = lax.fori_loop(0, n, scan, st0)

    def cross(ch, carry):
        rows = chunk_rows(ch)
        acc = o_scr[rows, :]
        for d in (0, 1):
            s_bd = jnp.where(own_s, stack(st_scr[d, ch]), 0.0).astype(BF16)
            acc = acc + _nt_dot(qe_scr[d, rows, :], s_bd)
        o_scr[rows, :] = acc
        return carry

    lax.fori_loop(0, n, cross, 0, unroll=2)

    g = g_ref[...]

    def finish(ch, carry):
        rows = chunk_rows(ch)
        of = o_scr[rows, :]
        normed = []
        for h in range(nh):
            oh = of[:, h * GLA_DV:(h + 1) * GLA_DV]
            mu = jnp.mean(oh, axis=-1, keepdims=True)
            dev = oh - mu
            var = jnp.mean(dev * dev, axis=-1, keepdims=True)
            normed.append(dev * lax.rsqrt(var + LN_EPS) * g)
        r = r_ref[rows, :].astype(F32)
        y_ref[rows, :] = jnp.concatenate(normed, axis=1) * (r * _sigmoid(r))
        return carry

    lax.fori_loop(0, n, finish, 0, unroll=2)


def _gla(geom, p, latent, w2, b2, norm_g, s0t=None, yc=None):
    if latent:
        t, n_seq, row0 = geom.t_lat, geom.n_lat, geom.tok_ctx // geom.t_lat
    else:
        t, n_seq, row0 = geom.t_ctx, geom.n_ctx, 0
    has_s0 = s0t is not None
    steps = n_seq if latent else geom.n_tok // t
    last = n_seq - 1

    def col(width, off):
        return pl.BlockSpec((t, width), lambda b: (row0 + jnp.minimum(b, last), off // width))

    in_specs = [
        col(GLA_QK_W, COL_QC), col(GLA_QK_W, COL_KC), col(GLA_V_W, COL_VC), col(GLA_V_W, COL_RC),
        col(LANES, COL_ZA),
        pl.BlockSpec((LANES, 2 * GLA_QK_W), lambda b: (0, 0)),
        pl.BlockSpec((1, 2 * GLA_QK_W), lambda b: (0, 0)),
        pl.BlockSpec((1, GLA_DV), lambda b: (0, 0)),
    ]
    args = [p, p, p, p, p, w2, b2, norm_g]
    if has_s0:
        in_specs += [pl.BlockSpec((None, 2, GLA_DV, GLA_QK_W), lambda b: (b, 0, 0, 0)),
                     pl.BlockSpec(memory_space=pl.ANY)]
        args += [s0t, yc]
    return pl.pallas_call(
        functools.partial(_gla_body, has_s0=has_s0, n_seq=n_seq),
        grid=(steps,),
        in_specs=in_specs,
        out_specs=[
            pl.BlockSpec((t, GLA_V_W), lambda b: (row0 + b, 0)),
            pl.BlockSpec((None, 2, GLA_DV, GLA_QK_W), lambda b: (jnp.minimum(b, last), 0, 0, 0)),
        ],
        out_shape=[
            jax.ShapeDtypeStruct((geom.n_tok, GLA_V_W), F32),
            jax.ShapeDtypeStruct((n_seq, 2, GLA_DV, GLA_QK_W), F32),
        ],
        scratch_shapes=[
            pltpu.VMEM((t, 2 * GLA_QK_W), F32),
            pltpu.VMEM((t, GLA_V_W), F32),
            pltpu.VMEM((2, t, GLA_QK_W), BF16),
            pltpu.VMEM((2, t // GLA_CHUNK, GLA_DV, GLA_QK_W), F32),
            pltpu.VMEM((2, t // GLA_CHUNK, GLA_DV, GLA_QK_W), F32),
            pltpu.VMEM((2, t // GLA_CHUNK, SUBLANES, GLA_QK_W), F32),
        ],
        input_output_aliases={len(args) - 1: 0} if has_s0 else {},
        compiler_params=_params(("arbitrary",)),
        name="gla_lat" if latent else "gla_ctx",
    )(*args)


def _merge_body(geom, x_ref, gt_ref, cv_ref, cvp_ref, cvn_ref, yb_ref, yc_ref, wbr_ref, wout_ref, cw_ref,
                g1_ref, sc2_ref, sh2_ref, lng_ref, lnb_ref, rwh_ref, rwl_ref, rb_ref,
                x1_ref, h2_ref, tidx_ref, tgate_ref, trank_ref, tcount_ref, *, dn_alpha):
    i = pl.program_id(0)
    tm = x_ref.shape[0]
    ctx_tiles = geom.tok_ctx // tm
    per_seq = jnp.where(i < ctx_tiles, geom.t_ctx // tm, geom.t_lat // tm)
    j = jnp.where(i < ctx_tiles, i, i - ctx_tiles) % per_seq
    has_prev = (j > 0).astype(F32)
    has_next = (j < per_seq - 1).astype(F32)

    cv = cv_ref[...].astype(F32)
    u, bg, cg = cv[:, :CONV_W], cv[:, CONV_W:2 * CONV_W], cv[:, 2 * CONV_W:]
    v = cg * u
    halo = cvp_ref.shape[0]
    pv = cvp_ref[...].astype(F32)[halo - 1:halo, :]
    nv = cvn_ref[...].astype(F32)[0:1, :]
    v_before = has_prev * (pv[:, 2 * CONV_W:] * pv[:, :CONV_W])
    v_after = has_next * (nv[:, 2 * CONV_W:] * nv[:, :CONV_W])
    row = lax.broadcasted_iota(jnp.int32, (tm, 1), 0)
    v_prev = jnp.where(row == 0, v_before, pltpu.roll(v, 1, 0))
    v_next = jnp.where(row == tm - 1, v_after, pltpu.roll(v, tm - 1, 0))
    cw = cw_ref[...]
    y_a = bg * (cw[0:1, :] * v_prev + cw[1:2, :] * v + cw[2:3, :] * v_next)

    gt = gt_ref[...].astype(F32)
    merged = (_sigmoid(gt[:, :D_MODEL]) * jnp.dot(y_a.astype(BF16), wbr_ref[0], preferred_element_type=F32)
              + _sigmoid(gt[:, D_MODEL:2 * D_MODEL])
              * jnp.dot(yb_ref[...].astype(BF16), wbr_ref[1], preferred_element_type=F32)
              + _sigmoid(gt[:, 2 * D_MODEL:])
              * jnp.dot(yc_ref[...].astype(BF16), wbr_ref[2], preferred_element_type=F32))
    mix = jnp.dot(merged.astype(BF16), wout_ref[...], preferred_element_type=F32)
    x1 = _layernorm_rows(dn_alpha * x_ref[...] + g1_ref[...] * mix, lng_ref[...], lnb_ref[...])
    x1_ref[...] = x1
    h2 = x1 * (1.0 + sc2_ref[...]) + sh2_ref[...]
    _store_row_tiles(h2_ref, h2)

    h_hi = h2.astype(BF16)
    h_lo = (h2 - h_hi.astype(F32)).astype(BF16)
    logits = (jnp.dot(h_hi, rwh_ref[...], preferred_element_type=F32)
              + jnp.dot(h_lo, rwh_ref[...], preferred_element_type=F32)
              + jnp.dot(h_hi, rwl_ref[...], preferred_element_type=F32)) + rb_ref[...]
    lane = lax.broadcasted_iota(jnp.int32, logits.shape, 1).astype(F32)
    vals, top_v, top_i = logits, [], []
    for _ in range(TOP_K):
        m = jnp.max(vals, axis=-1, keepdims=True)
        am = jnp.min(jnp.where(vals == m, lane, float(N_EXPERTS)), axis=-1, keepdims=True)
        top_v.append(m)
        top_i.append(am)
        vals = jnp.where(lane == am, -jnp.inf, vals)
    ex = [jnp.exp(tv - top_v[0]) for tv in top_v]
    den = ex[0] + ex[1] + ex[2] + ex[3]
    slot = lax.broadcasted_iota(jnp.int32, (tm, TOP_K), 1)
    gates, idxs = ex[TOP_K - 1], top_i[TOP_K - 1]
    for k in range(TOP_K - 2, -1, -1):
        gates = jnp.where(slot == k, ex[k], gates)
        idxs = jnp.where(slot == k, top_i[k], idxs)
    tgate_ref[...] = gates / den
    tidx_ref[...] = idxs.astype(jnp.int32)

    onehots = [(lane == ti).astype(F32) for ti in top_i]
    member = onehots[0] + onehots[1] + onehots[2] + onehots[3]
    earlier = (lax.broadcasted_iota(jnp.int32, (tm, tm), 0) > lax.broadcasted_iota(jnp.int32, (tm, tm), 1))
    before = jnp.dot(earlier.astype(BF16), member.astype(BF16), preferred_element_type=F32)
    ranks = jnp.sum(onehots[TOP_K - 1] * before, axis=-1, keepdims=True)
    for k in range(TOP_K - 2, -1, -1):
        ranks = jnp.where(slot == k, jnp.sum(onehots[k] * before, axis=-1, keepdims=True), ranks)
    trank_ref[...] = ranks.astype(jnp.int32)
    tcount_ref[...] = (before[tm - 1:tm, :] + member[tm - 1:tm, :]).astype(jnp.int32)


def _merge(geom, x, p, yb, yc, mod, lw, dn_alpha):
    tm = TOK_TILE
    n = geom.n_tok
    halo = BF16_SUBLANES
    halo_per_tile = tm // halo
    n_halo = n // halo
    conv_blk = COL_CONV // (3 * CONV_W)
    const2 = lambda i: (0, 0)
    return pl.pallas_call(
        functools.partial(_merge_body, geom, dn_alpha=dn_alpha),
        grid=(n // tm,),
        in_specs=[
            pl.BlockSpec((tm, D_MODEL), lambda i: (i, 0)),
            pl.BlockSpec((tm, 3 * D_MODEL), lambda i: (i, 0)),
            pl.BlockSpec((tm, 3 * CONV_W), lambda i: (i, conv_blk)),
            pl.BlockSpec((halo, 3 * CONV_W), lambda i: (jnp.maximum(i * halo_per_tile - 1, 0), conv_blk)),
            pl.BlockSpec((halo, 3 * CONV_W),
                         lambda i: (jnp.minimum((i + 1) * halo_per_tile, n_halo - 1), conv_blk)),
            pl.BlockSpec((tm, MIX_W), lambda i: (i, 0)),
            pl.BlockSpec((tm, MIX_W), lambda i: (i, 0)),
            pl.BlockSpec((3, MIX_W, D_MODEL), lambda i: (0, 0, 0)),
            pl.BlockSpec((D_MODEL, D_MODEL), const2),
            pl.BlockSpec((SUBLANES, CONV_W), const2),
            _mod_spec(geom, tm, 2), _mod_spec(geom, tm, 4), _mod_spec(geom, tm, 3),
            pl.BlockSpec((1, D_MODEL), const2), pl.BlockSpec((1, D_MODEL), const2),
            pl.BlockSpec((D_MODEL, N_EXPERTS), const2), pl.BlockSpec((D_MODEL, N_EXPERTS), const2),
            pl.BlockSpec((1, N_EXPERTS), const2),
        ],
        out_specs=[
            pl.BlockSpec((tm, D_MODEL), lambda i: (i, 0)),
            pl.BlockSpec((tm * ROW_TILE, LANES), lambda i: (i, 0)),
            pl.BlockSpec((tm, TOP_K), lambda i: (i, 0)),
            pl.BlockSpec((tm, TOP_K), lambda i: (i, 0)),
            pl.BlockSpec((tm, TOP_K), lambda i: (i, 0)),
            pl.BlockSpec((None, 1, N_EXPERTS), lambda i: (i, 0, 0)),
        ],
        out_shape=[
            jax.ShapeDtypeStruct((n, D_MODEL), F32),
            jax.ShapeDtypeStruct((n * ROW_TILE, LANES), F32),
            jax.ShapeDtypeStruct((n, TOP_K), jnp.int32),
            jax.ShapeDtypeStruct((n, TOP_K), F32),
            jax.ShapeDtypeStruct((n, TOP_K), jnp.int32),
            jax.ShapeDtypeStruct((n // tm, 1, N_EXPERTS), jnp.int32),
        ],
        compiler_params=_params(("arbitrary",)),
        name="merge",
    )(x, p, p, p, p, yb, yc, lw["w_branch"], lw["w_out"], lw["conv_w"], mod, mod, mod,
      lw["ln1_g"], lw["ln1_b"], lw["rw_hi"], lw["rw_lo"], lw["r_b"])


def _route(top_idx, rank, tile_counts, bm, tile):
    n_tok = top_idx.shape[0]
    n_tiles = n_tok // tile
    tcnt = tile_counts.reshape(n_tiles, N_EXPERTS)
    counts = jnp.sum(tcnt, axis=0)
    padded = (counts + bm - 1) // bm * bm
    pend = jnp.cumsum(padded)
    pstart = pend - padded
    tile_base = pstart[None, :] + jnp.cumsum(tcnt, axis=0) - tcnt
    tok_base = jnp.broadcast_to(tile_base[:, None, :], (n_tiles, tile, N_EXPERTS)).reshape(n_tok, 1, N_EXPERTS)
    onehot = top_idx[:, :, None] == jnp.arange(N_EXPERTS, dtype=jnp.int32)[None, None, :]
    dest = (jnp.sum(jnp.where(onehot, tok_base, 0), axis=-1) + rank).reshape(-1).astype(jnp.int32)
    n_blocks = n_tok * TOP_K // bm + N_EXPERTS
    blk_start = jnp.arange(n_blocks, dtype=jnp.int32) * bm
    blk_e = jnp.minimum(jnp.sum((pend[None, :] <= blk_start[:, None]).astype(jnp.int32), axis=1), N_EXPERTS - 1)
    n_used = (pend[-1] // bm).astype(jnp.int32).reshape(1)
    return dest, blk_e.astype(jnp.int32), n_used, (pstart + counts).astype(jnp.int32), (padded - counts).astype(
        jnp.int32)


def _row_tile(ref, row):
    return ref.at[pl.ds(pl.multiple_of(row * ROW_TILE, ROW_TILE), ROW_TILE), :]


def _dispatch_body(dest_ref, pad_start_ref, pad_n_ref, n_used_ref, h_ref, xs_hbm, zbuf, sem, zsem, bsem,
                   *, chunk, bm):
    i = pl.program_id(0)
    n_blocks = xs_hbm.shape[0] // (bm * ROW_TILE)

    @pl.when(i == 0)
    def _():
        zbuf[...] = jnp.zeros_like(zbuf)
        zrow = zbuf.at[pl.ds(0, ROW_TILE), :]

        def per_expert(e, total):
            start, cnt = pad_start_ref[e], pad_n_ref[e]

            def one(j, carry):
                pltpu.make_async_copy(zrow, _row_tile(xs_hbm, start + j), zsem).start()
                return carry

            lax.fori_loop(0, cnt, one, 0)
            return total + cnt

        total = lax.fori_loop(0, N_EXPERTS, per_expert, 0)

        def block_copy(b):
            return pltpu.make_async_copy(zbuf, xs_hbm.at[pl.ds(pl.multiple_of(b * bm * ROW_TILE, bm * ROW_TILE),
                                                                bm * ROW_TILE), :], bsem)

        def start_block(b, carry):
            block_copy(b).start()
            return carry

        def wait_block(b, carry):
            block_copy(b).wait()
            return carry

        lax.fori_loop(n_used_ref[0], n_blocks, start_block, 0)

        def wait_one(j, carry):
            pltpu.make_async_copy(zrow, _row_tile(xs_hbm, 0), zsem).wait()
            return carry

        lax.fori_loop(0, total, wait_one, 0)
        lax.fori_loop(n_used_ref[0], n_blocks, wait_block, 0)

    def one_token(r, carry):
        t = i * chunk + r
        for k in range(TOP_K):
            pltpu.make_async_copy(_row_tile(h_ref, r), _row_tile(xs_hbm, dest_ref[t * TOP_K + k]),
                                  sem).start(priority=k % 2)
        return carry

    lax.fori_loop(0, chunk, one_token, 0, unroll=4)
    rows = chunk * TOP_K * ROW_TILE
    pltpu.make_async_copy(xs_hbm.at[pl.ds(0, rows), :], xs_hbm.at[pl.ds(0, rows), :], sem).wait()


def _dispatch(h2_tiles, dest, pad_start, pad_n, n_used, n_blocks):
    chunk, bm = DISPATCH_CHUNK, MOE_ROWS
    n_tok = h2_tiles.shape[0] // ROW_TILE
    grid_spec = pltpu.PrefetchScalarGridSpec(
        num_scalar_prefetch=4,
        grid=(n_tok // chunk,),
        in_specs=[pl.BlockSpec((chunk * ROW_TILE, LANES), lambda i, *_: (i, 0))],
        out_specs=pl.BlockSpec(memory_space=pl.ANY),
        scratch_shapes=[
            pltpu.VMEM((bm * ROW_TILE, LANES), F32),
            pltpu.SemaphoreType.DMA,
            pltpu.SemaphoreType.DMA,
            pltpu.SemaphoreType.DMA,
        ],
    )
    return pl.pallas_call(
        functools.partial(_dispatch_body, chunk=chunk, bm=bm),
        grid_spec=grid_spec,
        out_shape=jax.ShapeDtypeStruct((n_blocks * bm * ROW_TILE, LANES), F32),
        compiler_params=_params(("arbitrary",), disable_bounds_checks=True),
        name="dispatch",
    )(dest, pad_start, pad_n, n_used, h2_tiles)


def _expert_body(blk_e_ref, n_used_ref, first_ref, next_e_ref, parity_ref, xs_ref, wgu_hbm, bgu_ref, wdn_hbm,
                 bdn_ref, y_ref, wgu_f32, wdn_f32, wsem, wgu_bf, wdn_bf, *, layer):
    i = pl.program_id(0)
    bm = y_ref.shape[0] // ROW_TILE
    n_used = n_used_ref[0]

    def weight_copies(e, s):
        return (pltpu.make_async_copy(wgu_hbm.at[layer, e], wgu_f32.at[s], wsem.at[s]),
                pltpu.make_async_copy(wdn_hbm.at[layer, e], wdn_f32.at[s], wsem.at[s]))

    @pl.when(i == 0)
    def _():
        for cp in weight_copies(blk_e_ref[0], 0):
            cp.start()

    @pl.when(i < n_used)
    def _():
        @pl.when(first_ref[i] == 1)
        def _():
            s = parity_ref[i]
            for cp in weight_copies(blk_e_ref[i], s):
                cp.wait()

            @pl.when(next_e_ref[i] >= 0)
            def _():
                for cp in weight_copies(next_e_ref[i], 1 - s):
                    cp.start()

            wgu_bf[...] = wgu_f32[s].astype(BF16)
            wdn_bf[...] = wdn_f32[s].astype(BF16)

        x = _load_row_tiles(xs_ref, bm).astype(BF16)
        acc = jnp.zeros((bm, D_MODEL), F32)
        for f0 in range(0, D_FF, FF_CHUNK):
            g = jnp.dot(x, wgu_bf[:, f0:f0 + FF_CHUNK], preferred_element_type=F32) + bgu_ref[:, f0:f0 + FF_CHUNK]
            u = (jnp.dot(x, wgu_bf[:, D_FF + f0:D_FF + f0 + FF_CHUNK], preferred_element_type=F32)
                 + bgu_ref[:, D_FF + f0:D_FF + f0 + FF_CHUNK])
            g = jnp.minimum(g, SWIGLU_LIMIT)
            u = jnp.clip(u, -SWIGLU_LIMIT, SWIGLU_LIMIT)
            act = (u + 1.0) * (g * _sigmoid(SWIGLU_ALPHA * g))
            acc = acc + jnp.dot(act.astype(BF16), wdn_bf[f0:f0 + FF_CHUNK, :], preferred_element_type=F32)
        _store_row_tiles(y_ref, acc + bdn_ref[...])

    @pl.when(i >= n_used)
    def _():
        y_ref[...] = jnp.zeros_like(y_ref)


def _expert_runs(blk_e, n_used):
    n_blocks = blk_e.shape[0]
    idx = jnp.arange(n_blocks, dtype=jnp.int32)
    before = jnp.concatenate([jnp.full((1,), -1, jnp.int32), blk_e[:-1]])
    first = (blk_e != before) & (idx < n_used[0])
    parity = (jnp.cumsum(first.astype(jnp.int32)) - 1) % 2
    starts = jnp.where(first, idx, n_blocks)
    later = jnp.concatenate([lax.cummin(starts, reverse=True)[1:], jnp.full((1,), n_blocks, jnp.int32)])
    next_e = jnp.where(later < n_blocks, blk_e[jnp.minimum(later, n_blocks - 1)], -1)
    return first.astype(jnp.int32), next_e.astype(jnp.int32), parity.astype(jnp.int32)


def _experts(xs, layer, blk_e, n_used, w_gate_up, b_gate_up, w_down, b_down):
    bm = MOE_ROWS
    n_blocks = blk_e.shape[0]
    first, next_e, parity = _expert_runs(blk_e, n_used)
    grid_spec = pltpu.PrefetchScalarGridSpec(
        num_scalar_prefetch=5,
        grid=(n_blocks,),
        in_specs=[
            pl.BlockSpec((bm * ROW_TILE, LANES), lambda i, be, nu, *_: (jnp.minimum(i, nu[0] - 1), 0)),
            pl.BlockSpec(memory_space=pl.ANY),
            pl.BlockSpec((None, None, 1, 2 * D_FF), lambda i, be, *_: (layer, be[i], 0, 0)),
            pl.BlockSpec(memory_space=pl.ANY),
            pl.BlockSpec((None, None, 1, D_MODEL), lambda i, be, *_: (layer, be[i], 0, 0)),
        ],
        out_specs=pl.BlockSpec((bm * ROW_TILE, LANES), lambda i, *_: (i, 0)),
        scratch_shapes=[
            pltpu.VMEM((2, D_MODEL, 2 * D_FF), F32),
            pltpu.VMEM((2, D_FF, D_MODEL), F32),
            pltpu.SemaphoreType.DMA((2,)),
            pltpu.VMEM((D_MODEL, 2 * D_FF), BF16),
            pltpu.VMEM((D_FF, D_MODEL), BF16),
        ],
    )
    depth = w_gate_up.shape[0]
    return pl.pallas_call(
        functools.partial(_expert_body, layer=layer),
        grid_spec=grid_spec,
        out_shape=jax.ShapeDtypeStruct((n_blocks * bm * ROW_TILE, LANES), F32),
        compiler_params=_params(("arbitrary",)),
        name="experts",
    )(blk_e, n_used, first, next_e, parity, xs, w_gate_up, b_gate_up.reshape(depth, N_EXPERTS, 1, 2 * D_FF),
      w_down, b_down.reshape(depth, N_EXPERTS, 1, D_MODEL))


def _combine_body(geom, dest_ref, y_hbm, x1_ref, gate_ref, g2_ref, lng_ref, lnb_ref, *rest, dn_alpha, split):
    if split:
        o_ctx_ref, o_lat_ref, buf, sem = rest
        o_ref = o_ctx_ref
    else:
        o_ref, buf, sem = rest
    i = pl.program_id(0)
    n_steps = pl.num_programs(0)
    tc = o_ref.shape[0]

    def issue(step, slot):
        base = step * tc * TOP_K

        def one(r, carry):
            for k in range(TOP_K):
                pltpu.make_async_copy(_row_tile(y_hbm, dest_ref[base + r * TOP_K + k]),
                                      _row_tile(buf.at[slot, k], r), sem.at[slot]).start(priority=k % 2)
            return carry

        lax.fori_loop(0, tc, one, 0, unroll=4)

    @pl.when(i == 0)
    def _():
        issue(0, 0)

    @pl.when(i + 1 < n_steps)
    def _():
        issue(i + 1, (i + 1) % 2)

    slot = i % 2
    for k in range(TOP_K):
        pltpu.make_async_copy(y_hbm.at[pl.ds(0, tc * ROW_TILE), :], buf.at[slot, k], sem.at[slot]).wait()
    gate = gate_ref[...]
    moe = gate[:, 0:1] * _load_row_tiles(buf.at[slot, 0], tc)
    for k in range(1, TOP_K):
        moe = moe + gate[:, k:k + 1] * _load_row_tiles(buf.at[slot, k], tc)
    z = dn_alpha * x1_ref[...] + g2_ref[...] * moe
    out = _layernorm_rows(z, lng_ref[...], lnb_ref[...])
    if split:
        ctx_steps = geom.tok_ctx // tc

        @pl.when(i < ctx_steps)
        def _():
            o_ctx_ref[...] = out

        @pl.when(i >= ctx_steps)
        def _():
            o_lat_ref[...] = out
    else:
        o_ref[...] = out


def _combine(geom, y_rows, dest, x1, top_gate, mod, ln_g, ln_b, dn_alpha, split=False):
    tc = COMB_TILE
    n = geom.n_tok
    if split:
        ctx_steps = geom.tok_ctx // tc
        out_specs = [pl.BlockSpec((tc, D_MODEL), lambda i, d: (jnp.minimum(i, ctx_steps - 1), 0)),
                     pl.BlockSpec((tc, D_MODEL), lambda i, d: (jnp.maximum(i - ctx_steps, 0), 0))]
        out_shape = [jax.ShapeDtypeStruct((geom.tok_ctx, D_MODEL), F32),
                     jax.ShapeDtypeStruct((geom.tok_lat, D_MODEL), F32)]
    else:
        out_specs = pl.BlockSpec((tc, D_MODEL), lambda i, d: (i, 0))
        out_shape = jax.ShapeDtypeStruct((n, D_MODEL), F32)
    grid_spec = pltpu.PrefetchScalarGridSpec(
        num_scalar_prefetch=1,
        grid=(n // tc,),
        in_specs=[
            pl.BlockSpec(memory_space=pl.ANY),
            pl.BlockSpec((tc, D_MODEL), lambda i, d: (i, 0)),
            pl.BlockSpec((tc, TOP_K), lambda i, d: (i, 0)),
            _mod_spec(geom, tc, 5),
            pl.BlockSpec((1, D_MODEL), lambda i, d: (0, 0)),
            pl.BlockSpec((1, D_MODEL), lambda i, d: (0, 0)),
        ],
        out_specs=out_specs,
        scratch_shapes=[
            pltpu.VMEM((2, TOP_K, tc * ROW_TILE, LANES), F32),
            pltpu.SemaphoreType.DMA((2,)),
        ],
    )
    return pl.pallas_call(
        functools.partial(_combine_body, geom, dn_alpha=dn_alpha, split=split),
        grid_spec=grid_spec,
        out_shape=out_shape,
        compiler_params=_params(("arbitrary",), disable_bounds_checks=True),
        name="combine",
    )(dest, y_rows, x1, top_gate, mod, ln_g, ln_b)


IN_SPLITS = (CONV_W, CONV_W, CONV_W, ATT_Q_W, ATT_KV_W, ATT_KV_W, GLA_QK_W, GLA_QK_W, GLA_V_W, GLA_V_W,
             GLA_GATE_RANK, GLA_GATE_RANK, D_MODEL, D_MODEL, D_MODEL)


def _w_in_body(w_ref, o_ref):
    w = w_ref[...]
    parts, o = [], 0
    for width in IN_SPLITS:
        parts.append(w[:, o:o + width])
        o += width
    u_a, b_a, c_a, q_b, k_b, v_b, q_c, k_c, v_c, r_c, za_f, za_b, gt_a, gt_b, gt_c = parts
    pad = jnp.zeros((w.shape[0], LANES - 2 * GLA_GATE_RANK), w.dtype)
    o_ref[...] = jnp.concatenate(
        [gt_a, gt_b, gt_c, u_a, b_a, c_a, q_b, v_c, r_c, q_c, k_c, k_b, v_b, za_f, za_b, pad], axis=1).astype(BF16)


def _rearrange_w_in(w_in):
    depth, d, in_cols = w_in.shape
    rows = 256
    return pl.pallas_call(
        _w_in_body,
        grid=(depth, d // rows),
        in_specs=[pl.BlockSpec((None, rows, in_cols), lambda l, i: (l, i, 0))],
        out_specs=pl.BlockSpec((None, rows, P_COLS), lambda l, i: (l, i, 0)),
        out_shape=jax.ShapeDtypeStruct((depth, d, P_COLS), BF16),
        compiler_params=_params(("arbitrary", "arbitrary")),
        name="w_in_layout",
    )(w_in)


def kernel(x_prompt, x_sample, cache_k, cache_v, state_gla, c, c_ctx, ada_w, ada_b, w_in, conv_w, attn_sink,
           gla_wa2, gla_ba2, gla_norm_g, w_branch, w_out, ln1_g, ln1_b, ln2_g, ln2_b, router_w, router_b,
           w_gate_up, b_gate_up, w_down, b_down):
    depth = w_in.shape[0]
    n_ctx, t_ctx, _ = x_prompt.shape
    n_lat, t_lat, _ = x_sample.shape
    geom = _Geom(n_ctx, t_ctx, n_lat, t_lat)
    dn_alpha = (2 * depth) ** 0.25

    mod_rows = -(-(n_lat + 1) // SUBLANES) * SUBLANES
    cond = jnp.zeros((mod_rows, D_MODEL), F32).at[:n_lat].set(c).at[n_lat].set(c_ctx)
    mods = _modulation(cond, ada_w, ada_b).reshape(depth, mod_rows, 1, 6 * D_MODEL)

    w_in_bf = _rearrange_w_in(w_in)
    w_branch_bf = w_branch.astype(BF16)
    w_out_bf = w_out.astype(BF16)
    conv_w_pad = jnp.zeros((depth, SUBLANES, CONV_W), F32).at[:, :conv_w.shape[1]].set(conv_w)
    rw_hi = router_w.astype(BF16)
    rw_lo = (router_w - rw_hi.astype(F32)).astype(BF16)
    w2 = jnp.zeros((depth, LANES, 2 * GLA_QK_W), F32)
    w2 = w2.at[:, :GLA_GATE_RANK, :GLA_QK_W].set(gla_wa2[:, 0])
    w2 = w2.at[:, GLA_GATE_RANK:2 * GLA_GATE_RANK, GLA_QK_W:].set(gla_wa2[:, 1]).astype(BF16)
    b2 = gla_ba2.reshape(depth, 1, 2 * GLA_QK_W)
    rope_tabs = _rope_tables(t_lat)
    s0t = jnp.transpose(state_gla, (0, 1, 2, 5, 3, 4)).reshape(n_lat, depth, 2, GLA_DV, GLA_QK_W)
    past = cache_k.shape[2]
    ck = cache_k.reshape(n_lat, depth, past, ATT_KV_W)
    cv = cache_v.reshape(n_lat, depth, past, ATT_KV_W)

    x = jnp.concatenate([x_prompt.reshape(-1, D_MODEL), x_sample.reshape(-1, D_MODEL)], axis=0)
    new_k, new_v, new_s = [], [], []
    for l in range(depth):
        mod = mods[l]
        lw = dict(w_branch=w_branch_bf[l], w_out=w_out_bf[l], conv_w=conv_w_pad[l],
                  ln1_g=ln1_g[l].reshape(1, -1), ln1_b=ln1_b[l].reshape(1, -1),
                  rw_hi=rw_hi[l], rw_lo=rw_lo[l], r_b=router_b[l].reshape(1, -1))
        p = _in_proj(geom, x, mod, w_in_bf, l)
        kv_shape = (n_ctx, t_ctx, ATT_KV_HEADS, HEAD_DIM)
        new_k.append(p[:geom.tok_ctx, COL_KB:COL_KB + ATT_KV_W].astype(F32).reshape(kv_shape))
        new_v.append(p[:geom.tok_ctx, COL_VB:COL_VB + ATT_KV_W].astype(F32).reshape(kv_shape))
        yb = _attn_lat(geom, p, ck[:, l], cv[:, l], rope_tabs, attn_sink[l], _attn_ctx(geom, p, attn_sink[l]))
        g_norm = gla_norm_g[l].reshape(1, GLA_DV)
        yc, s_ctx = _gla(geom, p, False, w2[l], b2[l], g_norm)
        yc, _ = _gla(geom, p, True, w2[l], b2[l], g_norm, s0t[:, l], yc)
        new_s.append(jnp.transpose(s_ctx.reshape(n_ctx, 2, GLA_DV, GLA_HEADS, GLA_DK), (0, 1, 3, 4, 2)))
        x1, h2_tiles, top_idx, top_gate, rank, tile_counts = _merge(geom, x, p, yb, yc, mod, lw, dn_alpha)
        dest, blk_e, n_used, pad_start, pad_n = _route(top_idx, rank, tile_counts, MOE_ROWS, TOK_TILE)
        xs = _dispatch(h2_tiles, dest, pad_start, pad_n, n_used, blk_e.shape[0])
        y_rows = _experts(xs, l, blk_e, n_used, w_gate_up, b_gate_up, w_down, b_down)
        x = _combine(geom, y_rows, dest, x1, top_gate, mod, ln2_g[l].reshape(1, -1), ln2_b[l].reshape(1, -1),
                     dn_alpha, split=(l == depth - 1))
    y_prompt = x[0].reshape(n_ctx, t_ctx, D_MODEL)
    y_sample = x[1].reshape(n_lat, t_lat, D_MODEL)
    return (y_prompt, y_sample, jnp.stack(new_k, axis=1), jnp.stack(new_v, axis=1), jnp.stack(new_s, axis=1))
```

```python
import functools

import jax
import jax.numpy as jnp
from jax import lax
from jax.experimental import pallas as pl
from jax.experimental.pallas import tpu as pltpu

F32 = jnp.float32
BF16 = jnp.bfloat16
HIGHEST = lax.Precision.HIGHEST

D_MODEL = 1024
MIX_W = D_MODEL // 2
CONV_W = MIX_W
HEAD_DIM = 64
ATT_HEADS = MIX_W // HEAD_DIM
ATT_KV_HEADS = 2
ATT_GROUP = ATT_HEADS // ATT_KV_HEADS
ATT_Q_W = ATT_HEADS * HEAD_DIM
ATT_KV_W = ATT_KV_HEADS * HEAD_DIM
WINDOW = 128
ATT_BLOCK = 128
ATT_SCALE = HEAD_DIM ** -0.5
ROPE_THETA = 10000.0
NEG_INF = -1e30
GRID_W = 64
GLA_HEADS = 4
GLA_DK = MIX_W // 2 // GLA_HEADS
GLA_DV = MIX_W // GLA_HEADS
GLA_QK_W = GLA_HEADS * GLA_DK
GLA_V_W = GLA_HEADS * GLA_DV
GLA_GATE_RANK = 16
GLA_GATE_NORM = 16.0
GLA_CHUNK = 64
N_EXPERTS = 32
TOP_K = 4
D_FF = D_MODEL
SWIGLU_LIMIT = 7.0
SWIGLU_ALPHA = 1.702
LN_EPS = 1e-5

LANES = 128
SUBLANES = 8
BF16_SUBLANES = 16
VMEM_LIMIT_BYTES = 56 * 1024 * 1024

COL_GATES = 0
COL_CONV = 3 * D_MODEL
COL_QB = COL_CONV + 3 * CONV_W
COL_VC = COL_QB + ATT_Q_W
COL_RC = COL_VC + GLA_V_W
COL_QC = COL_RC + GLA_V_W
COL_KC = COL_QC + GLA_QK_W
COL_KB = COL_KC + GLA_QK_W
COL_VB = COL_KB + ATT_KV_W
COL_ZA = COL_VB + ATT_KV_W
P_COLS = COL_ZA + LANES

TOK_TILE = 256
IN_TILE = 512
MOE_ROWS = 512
COMB_TILE = 256
DISPATCH_CHUNK = 1024
FF_CHUNK = 1024

ROW_TILE = D_MODEL // LANES


def _params(semantics, **kw):
    return pltpu.CompilerParams(dimension_semantics=semantics, vmem_limit_bytes=VMEM_LIMIT_BYTES, **kw)


def _sigmoid(x):
    return 1.0 / (1.0 + jnp.exp(-x))


def _layernorm_rows(z, g, b):
    mu = jnp.mean(z, axis=-1, keepdims=True)
    d = z - mu
    var = jnp.mean(d * d, axis=-1, keepdims=True)
    return d * lax.rsqrt(var + LN_EPS) * g + b


def _store_row_tiles(ref, x):
    rows = x.shape[0]
    for j in range(ROW_TILE):
        ref[pl.ds(j, rows, stride=ROW_TILE), :] = x[:, j * LANES:(j + 1) * LANES]


def _load_row_tiles(ref, rows):
    return jnp.concatenate([ref[pl.ds(j, rows, stride=ROW_TILE), :] for j in range(ROW_TILE)], axis=1)


def _ada_body(c_ref, w_ref, b_ref, o_ref):
    c = c_ref[...]
    s = c * _sigmoid(c)
    o_ref[...] = jnp.dot(s, w_ref[...], precision=HIGHEST, preferred_element_type=F32) + b_ref[...]


def _modulation(cond, ada_w, ada_b):
    depth, d, n6 = ada_w.shape
    rows = cond.shape[0]
    tn = 1536
    return pl.pallas_call(
        _ada_body,
        grid=(depth, n6 // tn),
        in_specs=[
            pl.BlockSpec((rows, d), lambda l, j: (0, 0)),
            pl.BlockSpec((None, d, tn), lambda l, j: (l, 0, j)),
            pl.BlockSpec((None, 1, tn), lambda l, j: (l, 0, j)),
        ],
        out_specs=pl.BlockSpec((None, rows, tn), lambda l, j: (l, 0, j)),
        out_shape=jax.ShapeDtypeStruct((depth, rows, n6), F32),
        compiler_params=_params(("arbitrary", "arbitrary")),
        name="modulation",
    )(cond, ada_w, ada_b.reshape(depth, 1, n6))


class _Geom:
    def __init__(self, n_ctx, t_ctx, n_lat, t_lat):
        self.n_ctx, self.t_ctx, self.n_lat, self.t_lat = n_ctx, t_ctx, n_lat, t_lat
        self.tok_ctx = n_ctx * t_ctx
        self.tok_lat = n_lat * t_lat
        self.n_tok = self.tok_ctx + self.tok_lat
        self.ctx_row = n_lat
        assert t_ctx % TOK_TILE == 0 and t_lat % TOK_TILE == 0 and self.tok_ctx % t_lat == 0

    def seq_of_tile(self, i, tile):
        ctx_tiles = self.tok_ctx // tile
        return jnp.where(i < ctx_tiles, self.ctx_row, (i - ctx_tiles) // (self.t_lat // tile))


def _mod_spec(geom, tile, which):
    return pl.BlockSpec((None, 1, D_MODEL), lambda i, *_: (geom.seq_of_tile(i, tile), 0, which))


IN_CHUNK = 1408


def _token_pair_specs(geom, tile):
    ctx_tiles = geom.tok_ctx // tile
    return [pl.BlockSpec((tile, D_MODEL), lambda i, *_: (jnp.minimum(i, ctx_tiles - 1), 0)),
            pl.BlockSpec((tile, D_MODEL), lambda i, *_: (jnp.maximum(i - ctx_tiles, 0), 0))]


def _token_tile(geom, x_ctx_ref, x_lat_ref):
    tile = x_ctx_ref.shape[0]
    is_ctx = pl.program_id(0) < geom.tok_ctx // tile
    return jnp.where(is_ctx, x_ctx_ref[...], x_lat_ref[...])


def _inproj_body(geom, xc_ref, xl_ref, sc_ref, sh_ref, w_ref, o_ref):
    h = (_token_tile(geom, xc_ref, xl_ref) * (1.0 + sc_ref[...]) + sh_ref[...]).astype(BF16)
    for c0 in range(0, P_COLS, IN_CHUNK):
        o_ref[:, c0:c0 + IN_CHUNK] = jnp.dot(h, w_ref[:, c0:c0 + IN_CHUNK],
                                             preferred_element_type=F32).astype(o_ref.dtype)


def _in_proj(geom, x_pair, mod, w_in_bf, layer):
    tm = IN_TILE
    return pl.pallas_call(
        functools.partial(_inproj_body, geom),
        grid=(geom.n_tok // tm,),
        in_specs=_token_pair_specs(geom, tm) + [
            _mod_spec(geom, tm, 1),
            _mod_spec(geom, tm, 0),
            pl.BlockSpec((None, D_MODEL, P_COLS), lambda i: (layer, 0, 0), pipeline_mode=pl.Buffered(1)),
        ],
        out_specs=pl.BlockSpec((tm, P_COLS), lambda i: (i, 0)),
        out_shape=jax.ShapeDtypeStruct((geom.n_tok, P_COLS), BF16),
        compiler_params=_params(("arbitrary",)),
        name="in_proj",
    )(x_pair[0], x_pair[1], mod, mod, w_in_bf)


def _nt_dot(a, b):
    return lax.dot_general(a, b, (((1,), (1,)), ((), ())), preferred_element_type=F32)


def _fold_sublanes(x, op):
    out = x[0:SUBLANES]
    for r0 in range(SUBLANES, x.shape[0], SUBLANES):
        out = op(out, x[r0:r0 + SUBLANES])
    return out


def _split_kv(k_blocks, v_blocks):
    keys, vts = [], []
    for g in range(ATT_KV_HEADS):
        hs = slice(g * HEAD_DIM, (g + 1) * HEAD_DIM)
        keys.append([kb[:, hs].astype(BF16) for kb in k_blocks])
        vts.append([vb[:, hs].astype(F32).T.astype(BF16) for vb in v_blocks])
    return keys, vts


def _attend(q, keys, vts, biases, sink_ref):
    rows = q.shape[0]
    qb = q.astype(BF16)
    pair_outs = []
    for g in range(ATT_KV_HEADS):
        heads = range(g * ATT_GROUP, (g + 1) * ATT_GROUP)
        q_group = jnp.concatenate([qb[:, h * HEAD_DIM:(h + 1) * HEAD_DIM] for h in heads], axis=0)
        sink = jnp.concatenate([jnp.full((SUBLANES, rows), sink_ref[h], F32) for h in heads], axis=1)
        bias_group = [None if b is None else jnp.concatenate([b] * ATT_GROUP, axis=1) for b in biases]
        scores = []
        m_part = sink
        for kb, bias in zip(keys[g], bias_group):
            s = _nt_dot(kb, q_group)
            if bias is not None:
                s = s + bias
            scores.append(s)
            m_part = jnp.maximum(m_part, _fold_sublanes(s, jnp.maximum))
        m = jnp.max(m_part, axis=0, keepdims=True)
        d_part = jnp.zeros_like(m_part)
        acc = jnp.zeros((HEAD_DIM, ATT_GROUP * rows), F32)
        for s, vt in zip(scores, vts[g]):
            p = jnp.exp(s - m)
            d_part = d_part + _fold_sublanes(p, jnp.add)
            acc = acc + jnp.dot(vt, p.astype(BF16), preferred_element_type=F32)
        den = jnp.sum(d_part, axis=0, keepdims=True) + jnp.exp(sink[0:1] - m)
        out_t = acc / den
        for a in range(0, ATT_GROUP, 2):
            pair_outs.append(jnp.concatenate([out_t[:, a * rows:(a + 1) * rows],
                                              out_t[:, (a + 1) * rows:(a + 2) * rows]], axis=0).T)
    return jnp.concatenate(pair_outs, axis=1)


def _attn_ctx_body(q_ref, k_ref, v_ref, sink_ref, o_ref, *, n_seq):
    t = q_ref.shape[0]

    @pl.when(pl.program_id(0) < n_seq)
    def _():
        blk = ATT_BLOCK
        q = q_ref[...].astype(F32) * ATT_SCALE
        k = k_ref[...]
        v = v_ref[...]
        blocks = [slice(r0, r0 + blk) for r0 in range(0, t, blk)]
        keys, vts = _split_kv([k[b] for b in blocks], [v[b] for b in blocks])
        o_ref[...] = _attend(q, keys, vts, [None] * len(blocks), sink_ref)

    @pl.when(pl.program_id(0) >= n_seq)
    def _():
        o_ref[...] = jnp.zeros_like(o_ref)


def _attn_ctx(geom, p, sink):
    t = geom.t_ctx
    last = geom.n_ctx - 1
    return pl.pallas_call(
        functools.partial(_attn_ctx_body, n_seq=geom.n_ctx),
        grid=(geom.n_tok // t,),
        in_specs=[
            pl.BlockSpec((t, ATT_Q_W), lambda b: (jnp.minimum(b, last), COL_QB // ATT_Q_W)),
            pl.BlockSpec((t, ATT_KV_W), lambda b: (jnp.minimum(b, last), COL_KB // ATT_KV_W)),
            pl.BlockSpec((t, ATT_KV_W), lambda b: (jnp.minimum(b, last), COL_VB // ATT_KV_W)),
            pl.BlockSpec(memory_space=pltpu.SMEM),
        ],
        out_specs=pl.BlockSpec((t, ATT_Q_W), lambda b: (b, 0)),
        out_shape=jax.ShapeDtypeStruct((geom.n_tok, ATT_Q_W), F32),
        compiler_params=_params(("arbitrary",)),
        name="attn_ctx",
    )(p, p, p, sink)


def _rope(x, cos, sin_lo, sin_hi):
    nf = HEAD_DIM // 4
    outs = []
    for j in range(x.shape[1] // LANES):
        xj = x[:, j * LANES:(j + 1) * LANES]
        outs.append(xj * cos + pltpu.roll(xj, LANES - nf, 1) * sin_lo + pltpu.roll(xj, nf, 1) * sin_hi)
    return outs[0] if len(outs) == 1 else jnp.concatenate(outs, axis=1)


def _attn_lat_body(q_ref, k_ref, v_ref, ck_ref, cv_ref, qcos_ref, qslo_ref, qshi_ref,
                   kcos_ref, kslo_ref, kshi_ref, sink_ref, yb_hbm, o_ref):
    del yb_hbm
    i = pl.program_id(1)
    nb = k_ref.shape[0] // ATT_BLOCK
    blk = ATT_BLOCK
    q = _rope(q_ref[...].astype(F32), qcos_ref[...], qslo_ref[...], qshi_ref[...]) * ATT_SCALE
    k_in_blk = lax.broadcasted_iota(jnp.int32, (blk, blk), 0)
    q_in_blk = lax.broadcasted_iota(jnp.int32, (blk, blk), 1)
    k_blocks, v_blocks, biases = [], [], []
    for j in (-1, 0, 1):
        kb = i + j
        kbc = jnp.clip(kb, 0, nb - 1)
        rows = pl.ds(pl.multiple_of(kbc * blk, blk), blk)
        k_blocks.append(_rope(k_ref[rows, :].astype(F32), kcos_ref[rows, :], kslo_ref[rows, :], kshi_ref[rows, :]))
        v_blocks.append(v_ref[rows, :])
        outside = jnp.where((kb >= 0) & (kb < nb), 0, 4 * WINDOW)
        dist = (j * blk + k_in_blk) - q_in_blk
        biases.append(jnp.where(jnp.abs(dist) + outside <= WINDOW, 0.0, NEG_INF))
    for r0 in range(0, ck_ref.shape[0], blk):
        k_blocks.append(ck_ref[r0:r0 + blk, :])
        v_blocks.append(cv_ref[r0:r0 + blk, :])
        biases.append(None)
    keys, vts = _split_kv(k_blocks, v_blocks)
    o_ref[...] = _attend(q, keys, vts, biases, sink_ref)


def _attn_lat(geom, p, cache_k, cache_v, rope_tabs, sink, yb):
    t, blk = geom.t_lat, ATT_BLOCK
    nb = t // blk
    past = cache_k.shape[1]
    q_row0 = geom.tok_ctx // blk
    s_row0 = geom.tok_ctx // t
    cos, slo, shi = rope_tabs
    q_tab = pl.BlockSpec((blk, LANES), lambda b, i: (i, 0))
    k_tab = pl.BlockSpec((t, LANES), lambda b, i: (0, 0))
    return pl.pallas_call(
        _attn_lat_body,
        grid=(geom.n_lat, nb),
        in_specs=[
            pl.BlockSpec((blk, ATT_Q_W), lambda b, i: (q_row0 + b * nb + i, COL_QB // ATT_Q_W)),
            pl.BlockSpec((t, ATT_KV_W), lambda b, i: (s_row0 + b, COL_KB // ATT_KV_W)),
            pl.BlockSpec((t, ATT_KV_W), lambda b, i: (s_row0 + b, COL_VB // ATT_KV_W)),
            pl.BlockSpec((None, past, ATT_KV_W), lambda b, i: (b, 0, 0)),
            pl.BlockSpec((None, past, ATT_KV_W), lambda b, i: (b, 0, 0)),
            q_tab, q_tab, q_tab, k_tab, k_tab, k_tab,
            pl.BlockSpec(memory_space=pltpu.SMEM),
            pl.BlockSpec(memory_space=pl.ANY),
        ],
        out_specs=pl.BlockSpec((blk, ATT_Q_W), lambda b, i: (q_row0 + b * nb + i, 0)),
        out_shape=jax.ShapeDtypeStruct((geom.n_tok, ATT_Q_W), F32),
        input_output_aliases={12: 0},
        compiler_params=_params(("arbitrary", "arbitrary")),
        name="attn_lat",
    )(p, p, p, cache_k, cache_v, cos, slo, shi, cos, slo, shi, sink, yb)


def _rope_tables(t_lat):
    half = HEAD_DIM // 2
    nf = half // 2
    tok = jnp.arange(t_lat)
    lane = jnp.arange(LANES) % HEAD_DIM
    pos = jnp.where(lane[None, :] < half, (tok // GRID_W)[:, None], (tok % GRID_W)[:, None]).astype(F32)
    inv = ROPE_THETA ** (-(lane % nf).astype(F32) / nf)
    ang = pos * inv[None, :]
    first = (lane % half) < nf
    cos = jnp.cos(ang)
    sin = jnp.sin(ang)
    sin_lo = jnp.where(first[None, :], -sin, 0.0)
    sin_hi = jnp.where(first[None, :], 0.0, sin)
    return cos, sin_lo, sin_hi


def _gla_body(*refs, has_s0, n_seq):
    y_ref = refs[10 if has_s0 else 8]

    @pl.when(pl.program_id(0) < n_seq)
    def _():
        _gla_sequence(*refs, has_s0=has_s0)

    @pl.when(pl.program_id(0) >= n_seq)
    def _():
        y_ref[...] = jnp.zeros_like(y_ref)


def _gla_sequence(*refs, has_s0):
    if has_s0:
        (q_ref, k_ref, v_ref, r_ref, za_ref, w2_ref, b2_ref, g_ref, s0_ref, _yc_hbm,
         y_ref, sfin_ref, bc_scr, o_scr, qe_scr, upd_scr, st_scr, last_scr) = refs
    else:
        (q_ref, k_ref, v_ref, r_ref, za_ref, w2_ref, b2_ref, g_ref,
         y_ref, sfin_ref, bc_scr, o_scr, qe_scr, upd_scr, st_scr, last_scr) = refs
    t = q_ref.shape[0]
    c, nh = GLA_CHUNK, GLA_HEADS
    n = t // c
    rb = nh * c

    def iota(shape, axis):
        return lax.broadcasted_iota(jnp.int32, shape, axis)

    same_chunk = (iota((rb, rb), 0) // c) == (iota((rb, rb), 1) // c)
    cum = (jnp.where(same_chunk & (iota((rb, rb), 0) >= iota((rb, rb), 1)), 1.0, 0.0).astype(BF16),
           jnp.where(same_chunk & (iota((rb, rb), 0) <= iota((rb, rb), 1)), 1.0, 0.0).astype(BF16))
    for r0 in range(0, t, rb):
        z = jnp.dot(za_ref[r0:r0 + rb, :].astype(BF16), w2_ref[...], preferred_element_type=F32) + b2_ref[...]
        la = (jnp.minimum(z, 0.0) - jnp.log1p(jnp.exp(-jnp.abs(z)))) * (1.0 / GLA_GATE_NORM)
        for d in (0, 1):
            rest = la[:, d * GLA_QK_W:(d + 1) * GLA_QK_W]
            bc = None
            for _ in range(3):
                term = rest.astype(BF16)
                rest = rest - term.astype(F32)
                part = jnp.dot(cum[d], term, preferred_element_type=F32)
                bc = part if bc is None else bc + part
            bc_scr[r0:r0 + rb, d * GLA_QK_W:(d + 1) * GLA_QK_W] = bc

    in_chunk = iota((c, nh * c), 1) % c
    keeps = (iota((c, nh * c), 0) >= in_chunk, iota((c, nh * c), 0) <= in_chunk)
    own_k = (iota((nh * c, GLA_QK_W), 0) // c) == (iota((nh * c, GLA_QK_W), 1) // GLA_DK)
    own_v = (iota((nh * c, GLA_V_W), 0) // c) == (iota((nh * c, GLA_V_W), 1) // GLA_DV)
    own_s = (iota((nh * GLA_DV, GLA_QK_W), 0) // GLA_DV) == (iota((nh * GLA_DV, GLA_QK_W), 1) // GLA_DK)
    lane_head = iota((GLA_DV, GLA_QK_W), 1) // GLA_DK

    def stack(x):
        return jnp.concatenate([x] * nh, axis=0)

    def chunk_rows(ch):
        return pl.ds(pl.multiple_of(ch * c, c), c)

    def local(ch, carry):
        rows = chunk_rows(ch)
        q = q_ref[rows, :].astype(F32) * (GLA_DK ** -0.5)
        k = k_ref[rows, :].astype(F32)
        v = v_ref[rows, :].astype(F32)
        v_bd = jnp.where(own_v, stack(v), 0.0).astype(BF16)
        vt = v.T.astype(BF16)
        o_sum = None
        for d in (0, 1):
            bc = bc_scr[rows, d * GLA_QK_W:(d + 1) * GLA_QK_W]
            last = bc[c - 1:c, :] if d == 0 else bc[0:1, :]
            qe = (q * jnp.exp(bc)).astype(BF16)
            kd = (k * jnp.exp(last - bc)).astype(BF16)
            k_bd = jnp.where(own_k, stack(k * jnp.exp(-bc)), 0.0).astype(BF16)
            qe_scr[d, rows, :] = qe
            last_scr[d, ch] = jnp.broadcast_to(last, (SUBLANES, GLA_QK_W))
            att = jnp.where(keeps[d], _nt_dot(qe, k_bd), 0.0)
            o_d = jnp.dot(att.astype(BF16), v_bd, preferred_element_type=F32)
            u = jnp.dot(vt, kd, preferred_element_type=F32)
            upd = jnp.where(lane_head == 0, u[:GLA_DV], 0.0)
            for h in range(1, nh):
                upd = jnp.where(lane_head == h, u[h * GLA_DV:(h + 1) * GLA_DV], upd)
            upd_scr[d, ch] = upd
            o_sum = o_d if o_sum is None else o_sum + o_d
        o_scr[rows, :] = o_sum
        return carry

    lax.fori_loop(0, n, local, 0, unroll=2)

    for d in (0, 1):
        def scan(step, st, d=d):
            ch = step if d == 0 else n - 1 - step
            st_scr[d, ch] = st
            return st * jnp.exp(last_scr[d, ch][0:1, :]) + upd_scr[d, ch]

        st0 = s0_ref[d] if has_s0 else jnp.zeros((GLA_DV, GLA_QK_W), F32)
        sfin_ref[d] = lax.fori_loop(0, n, scan, st0)

    def cross(ch, carry):
        rows = chunk_rows(ch)
        acc = o_scr[rows, :]
        for d in (0, 1):
            s_bd = jnp.where(own_s, stack(st_scr[d, ch]), 0.0).astype(BF16)
            acc = acc + _nt_dot(qe_scr[d, rows, :], s_bd)
        o_scr[rows, :] = acc
        return carry

    lax.fori_loop(0, n, cross, 0, unroll=2)

    g = g_ref[...]

    def finish(ch, carry):
        rows = chunk_rows(ch)
        of = o_scr[rows, :]
        normed = []
        for h in range(nh):
            oh = of[:, h * GLA_DV:(h + 1) * GLA_DV]
            mu = jnp.mean(oh, axis=-1, keepdims=True)
            dev = oh - mu
            var = jnp.mean(dev * dev, axis=-1, keepdims=True)
            normed.append(dev * lax.rsqrt(var + LN_EPS) * g)
        r = r_ref[rows, :].astype(F32)
        y_ref[rows, :] = jnp.concatenate(normed, axis=1) * (r * _sigmoid(r))
        return carry

    lax.fori_loop(0, n, finish, 0, unroll=2)


def _gla(geom, p, latent, w2, b2, norm_g, s0t=None, yc=None):
    if latent:
        t, n_seq, row0 = geom.t_lat, geom.n_lat, geom.tok_ctx // geom.t_lat
    else:
        t, n_seq, row0 = geom.t_ctx, geom.n_ctx, 0
    has_s0 = s0t is not None
    steps = n_seq if latent else geom.n_tok // t
    last = n_seq - 1

    def col(width, off):
        return pl.BlockSpec((t, width), lambda b: (row0 + jnp.minimum(b, last), off // width))

    in_specs = [
        col(GLA_QK_W, COL_QC), col(GLA_QK_W, COL_KC), col(GLA_V_W, COL_VC), col(GLA_V_W, COL_RC),
        col(LANES, COL_ZA),
        pl.BlockSpec((LANES, 2 * GLA_QK_W), lambda b: (0, 0)),
        pl.BlockSpec((1, 2 * GLA_QK_W), lambda b: (0, 0)),
        pl.BlockSpec((1, GLA_DV), lambda b: (0, 0)),
    ]
    args = [p, p, p, p, p, w2, b2, norm_g]
    if has_s0:
        in_specs += [pl.BlockSpec((None, 2, GLA_DV, GLA_QK_W), lambda b: (b, 0, 0, 0)),
                     pl.BlockSpec(memory_space=pl.ANY)]
        args += [s0t, yc]
    return pl.pallas_call(
        functools.partial(_gla_body, has_s0=has_s0, n_seq=n_seq),
        grid=(steps,),
        in_specs=in_specs,
        out_specs=[
            pl.BlockSpec((t, GLA_V_W), lambda b: (row0 + b, 0)),
            pl.BlockSpec((None, 2, GLA_DV, GLA_QK_W), lambda b: (jnp.minimum(b, last), 0, 0, 0)),
        ],
        out_shape=[
            jax.ShapeDtypeStruct((geom.n_tok, GLA_V_W), F32),
            jax.ShapeDtypeStruct((n_seq, 2, GLA_DV, GLA_QK_W), F32),
        ],
        scratch_shapes=[
            pltpu.VMEM((t, 2 * GLA_QK_W), F32),
            pltpu.VMEM((t, GLA_V_W), F32),
            pltpu.VMEM((2, t, GLA_QK_W), BF16),
            pltpu.VMEM((2, t // GLA_CHUNK, GLA_DV, GLA_QK_W), F32),
            pltpu.VMEM((2, t // GLA_CHUNK, GLA_DV, GLA_QK_W), F32),
            pltpu.VMEM((2, t // GLA_CHUNK, SUBLANES, GLA_QK_W), F32),
        ],
        input_output_aliases={len(args) - 1: 0} if has_s0 else {},
        compiler_params=_params(("arbitrary",)),
        name="gla_lat" if latent else "gla_ctx",
    )(*args)


def _merge_body(geom, xc_ref, xl_ref, gt_ref, cv_ref, cvp_ref, cvn_ref, yb_ref, yc_ref, wbr_ref, wout_ref, cw_ref,
                g1_ref, sc2_ref, sh2_ref, lng_ref, lnb_ref, rwh_ref, rwl_ref, rb_ref,
                x1_ref, h2_ref, tidx_ref, tgate_ref, trank_ref, tcount_ref, *, dn_alpha):
    i = pl.program_id(0)
    tm = xc_ref.shape[0]
    ctx_tiles = geom.tok_ctx // tm
    per_seq = jnp.where(i < ctx_tiles, geom.t_ctx // tm, geom.t_lat // tm)
    j = jnp.where(i < ctx_tiles, i, i - ctx_tiles) % per_seq
    has_prev = (j > 0).astype(F32)
    has_next = (j < per_seq - 1).astype(F32)

    cv = cv_ref[...].astype(F32)
    u, bg, cg = cv[:, :CONV_W], cv[:, CONV_W:2 * CONV_W], cv[:, 2 * CONV_W:]
    v = cg * u
    halo = cvp_ref.shape[0]
    pv = cvp_ref[...].astype(F32)[halo - 1:halo, :]
    nv = cvn_ref[...].astype(F32)[0:1, :]
    v_before = has_prev * (pv[:, 2 * CONV_W:] * pv[:, :CONV_W])
    v_after = has_next * (nv[:, 2 * CONV_W:] * nv[:, :CONV_W])
    row = lax.broadcasted_iota(jnp.int32, (tm, 1), 0)
    v_prev = jnp.where(row == 0, v_before, pltpu.roll(v, 1, 0))
    v_next = jnp.where(row == tm - 1, v_after, pltpu.roll(v, tm - 1, 0))
    cw = cw_ref[...]
    y_a = bg * (cw[0:1, :] * v_prev + cw[1:2, :] * v + cw[2:3, :] * v_next)

    gt = gt_ref[...].astype(F32)
    merged = (_sigmoid(gt[:, :D_MODEL]) * jnp.dot(y_a.astype(BF16), wbr_ref[0], preferred_element_type=F32)
              + _sigmoid(gt[:, D_MODEL:2 * D_MODEL])
              * jnp.dot(yb_ref[...].astype(BF16), wbr_ref[1], preferred_element_type=F32)
              + _sigmoid(gt[:, 2 * D_MODEL:])
              * jnp.dot(yc_ref[...].astype(BF16), wbr_ref[2], preferred_element_type=F32))
    mix = jnp.dot(merged.astype(BF16), wout_ref[...], preferred_element_type=F32)
    x1 = _layernorm_rows(dn_alpha * _token_tile(geom, xc_ref, xl_ref) + g1_ref[...] * mix,
                         lng_ref[...], lnb_ref[...])
    x1_ref[...] = x1
    h2 = x1 * (1.0 + sc2_ref[...]) + sh2_ref[...]
    _store_row_tiles(h2_ref, h2)

    h_hi = h2.astype(BF16)
    h_lo = (h2 - h_hi.astype(F32)).astype(BF16)
    logits = (jnp.dot(h_hi, rwh_ref[...], preferred_element_type=F32)
              + jnp.dot(h_lo, rwh_ref[...], preferred_element_type=F32)
              + jnp.dot(h_hi, rwl_ref[...], preferred_element_type=F32)) + rb_ref[...]
    lane = lax.broadcasted_iota(jnp.int32, logits.shape, 1).astype(F32)
    vals, top_v, top_i = logits, [], []
    for _ in range(TOP_K):
        m = jnp.max(vals, axis=-1, keepdims=True)
        am = jnp.min(jnp.where(vals == m, lane, float(N_EXPERTS)), axis=-1, keepdims=True)
        top_v.append(m)
        top_i.append(am)
        vals = jnp.where(lane == am, -jnp.inf, vals)
    ex = [jnp.exp(tv - top_v[0]) for tv in top_v]
    den = ex[0] + ex[1] + ex[2] + ex[3]
    slot = lax.broadcasted_iota(jnp.int32, (tm, TOP_K), 1)
    gates, idxs = ex[TOP_K - 1], top_i[TOP_K - 1]
    for k in range(TOP_K - 2, -1, -1):
        gates = jnp.where(slot == k, ex[k], gates)
        idxs = jnp.where(slot == k, top_i[k], idxs)
    tgate_ref[...] = gates / den
    tidx_ref[...] = idxs.astype(jnp.int32)

    onehots = [(lane == ti).astype(F32) for ti in top_i]
    member = onehots[0] + onehots[1] + onehots[2] + onehots[3]
    earlier = (lax.broadcasted_iota(jnp.int32, (tm, tm), 0) > lax.broadcasted_iota(jnp.int32, (tm, tm), 1))
    before = jnp.dot(earlier.astype(BF16), member.astype(BF16), preferred_element_type=F32)
    ranks = jnp.sum(onehots[TOP_K - 1] * before, axis=-1, keepdims=True)
    for k in range(TOP_K - 2, -1, -1):
        ranks = jnp.where(slot == k, jnp.sum(onehots[k] * before, axis=-1, keepdims=True), ranks)
    trank_ref[...] = ranks.astype(jnp.int32)
    tcount_ref[...] = (before[tm - 1:tm, :] + member[tm - 1:tm, :]).astype(jnp.int32)


def _merge(geom, x_pair, p, yb, yc, mod, lw, dn_alpha):
    tm = TOK_TILE
    n = geom.n_tok
    halo = BF16_SUBLANES
    halo_per_tile = tm // halo
    n_halo = n // halo
    conv_blk = COL_CONV // (3 * CONV_W)
    const2 = lambda i: (0, 0)
    return pl.pallas_call(
        functools.partial(_merge_body, geom, dn_alpha=dn_alpha),
        grid=(n // tm,),
        in_specs=_token_pair_specs(geom, tm) + [
            pl.BlockSpec((tm, 3 * D_MODEL), lambda i: (i, 0)),
            pl.BlockSpec((tm, 3 * CONV_W), lambda i: (i, conv_blk)),
            pl.BlockSpec((halo, 3 * CONV_W), lambda i: (jnp.maximum(i * halo_per_tile - 1, 0), conv_blk)),
            pl.BlockSpec((halo, 3 * CONV_W),
                         lambda i: (jnp.minimum((i + 1) * halo_per_tile, n_halo - 1), conv_blk)),
            pl.BlockSpec((tm, MIX_W), lambda i: (i, 0)),
            pl.BlockSpec((tm, MIX_W), lambda i: (i, 0)),
            pl.BlockSpec((3, MIX_W, D_MODEL), lambda i: (0, 0, 0)),
            pl.BlockSpec((D_MODEL, D_MODEL), const2),
            pl.BlockSpec((SUBLANES, CONV_W), const2),
            _mod_spec(geom, tm, 2), _mod_spec(geom, tm, 4), _mod_spec(geom, tm, 3),
            pl.BlockSpec((1, D_MODEL), const2), pl.BlockSpec((1, D_MODEL), const2),
            pl.BlockSpec((D_MODEL, N_EXPERTS), const2), pl.BlockSpec((D_MODEL, N_EXPERTS), const2),
            pl.BlockSpec((1, N_EXPERTS), const2),
        ],
        out_specs=[
            pl.BlockSpec((tm, D_MODEL), lambda i: (i, 0)),
            pl.BlockSpec((tm * ROW_TILE, LANES), lambda i: (i, 0)),
            pl.BlockSpec((tm, TOP_K), lambda i: (i, 0)),
            pl.BlockSpec((tm, TOP_K), lambda i: (i, 0)),
            pl.BlockSpec((tm, TOP_K), lambda i: (i, 0)),
            pl.BlockSpec((None, 1, N_EXPERTS), lambda i: (i, 0, 0)),
        ],
        out_shape=[
            jax.ShapeDtypeStruct((n, D_MODEL), F32),
            jax.ShapeDtypeStruct((n * ROW_TILE, LANES), F32),
            jax.ShapeDtypeStruct((n, TOP_K), jnp.int32),
            jax.ShapeDtypeStruct((n, TOP_K), F32),
            jax.ShapeDtypeStruct((n, TOP_K), jnp.int32),
            jax.ShapeDtypeStruct((n // tm, 1, N_EXPERTS), jnp.int32),
        ],
        compiler_params=_params(("arbitrary",)),
        name="merge",
    )(x_pair[0], x_pair[1], p, p, p, p, yb, yc, lw["w_branch"], lw["w_out"], lw["conv_w"], mod, mod, mod,
      lw["ln1_g"], lw["ln1_b"], lw["rw_hi"], lw["rw_lo"], lw["r_b"])


def _route(top_idx, rank, tile_counts, bm, tile):
    n_tok = top_idx.shape[0]
    n_tiles = n_tok // tile
    tcnt = tile_counts.reshape(n_tiles, N_EXPERTS)
    counts = jnp.sum(tcnt, axis=0)
    padded = (counts + bm - 1) // bm * bm
    pend = jnp.cumsum(padded)
    pstart = pend - padded
    tile_base = pstart[None, :] + jnp.cumsum(tcnt, axis=0) - tcnt
    tok_base = jnp.broadcast_to(tile_base[:, None, :], (n_tiles, tile, N_EXPERTS)).reshape(n_tok, 1, N_EXPERTS)
    onehot = top_idx[:, :, None] == jnp.arange(N_EXPERTS, dtype=jnp.int32)[None, None, :]
    dest = (jnp.sum(jnp.where(onehot, tok_base, 0), axis=-1) + rank).reshape(-1).astype(jnp.int32)
    n_blocks = n_tok * TOP_K // bm + N_EXPERTS
    blk_start = jnp.arange(n_blocks, dtype=jnp.int32) * bm
    blk_e = jnp.minimum(jnp.sum((pend[None, :] <= blk_start[:, None]).astype(jnp.int32), axis=1), N_EXPERTS - 1)
    n_used = (pend[-1] // bm).astype(jnp.int32).reshape(1)
    return dest, blk_e.astype(jnp.int32), n_used, (pstart + counts).astype(jnp.int32), (padded - counts).astype(
        jnp.int32)


def _row_tile(ref, row):
    return ref.at[pl.ds(pl.multiple_of(row * ROW_TILE, ROW_TILE), ROW_TILE), :]


def _dispatch_body(dest_ref, pad_start_ref, pad_n_ref, n_used_ref, h_ref, xs_hbm, zbuf, sem, zsem, bsem,
                   *, chunk, bm):
    i = pl.program_id(0)
    n_blocks = xs_hbm.shape[0] // (bm * ROW_TILE)

    @pl.when(i == 0)
    def _():
        zbuf[...] = jnp.zeros_like(zbuf)
        zrow = zbuf.at[pl.ds(0, ROW_TILE), :]

        def per_expert(e, total):
            start, cnt = pad_start_ref[e], pad_n_ref[e]

            def one(j, carry):
                pltpu.make_async_copy(zrow, _row_tile(xs_hbm, start + j), zsem).start()
                return carry

            lax.fori_loop(0, cnt, one, 0)
            return total + cnt

        total = lax.fori_loop(0, N_EXPERTS, per_expert, 0)

        def block_copy(b):
            return pltpu.make_async_copy(zbuf, xs_hbm.at[pl.ds(pl.multiple_of(b * bm * ROW_TILE, bm * ROW_TILE),
                                                                bm * ROW_TILE), :], bsem)

        def start_block(b, carry):
            block_copy(b).start()
            return carry

        def wait_block(b, carry):
            block_copy(b).wait()
            return carry

        lax.fori_loop(n_used_ref[0], n_blocks, start_block, 0)

        def wait_one(j, carry):
            pltpu.make_async_copy(zrow, _row_tile(xs_hbm, 0), zsem).wait()
            return carry

        lax.fori_loop(0, total, wait_one, 0)
        lax.fori_loop(n_used_ref[0], n_blocks, wait_block, 0)

    def one_token(r, carry):
        t = i * chunk + r
        for k in range(TOP_K):
            pltpu.make_async_copy(_row_tile(h_ref, r), _row_tile(xs_hbm, dest_ref[t * TOP_K + k]),
                                  sem).start(priority=k % 2)
        return carry

    lax.fori_loop(0, chunk, one_token, 0, unroll=4)
    rows = chunk * TOP_K * ROW_TILE
    pltpu.make_async_copy(xs_hbm.at[pl.ds(0, rows), :], xs_hbm.at[pl.ds(0, rows), :], sem).wait()


def _dispatch(h2_tiles, dest, pad_start, pad_n, n_used, n_blocks):
    chunk, bm = DISPATCH_CHUNK, MOE_ROWS
    n_tok = h2_tiles.shape[0] // ROW_TILE
    grid_spec = pltpu.PrefetchScalarGridSpec(
        num_scalar_prefetch=4,
        grid=(n_tok // chunk,),
        in_specs=[pl.BlockSpec((chunk * ROW_TILE, LANES), lambda i, *_: (i, 0))],
        out_specs=pl.BlockSpec(memory_space=pl.ANY),
        scratch_shapes=[
            pltpu.VMEM((bm * ROW_TILE, LANES), F32),
            pltpu.SemaphoreType.DMA,
            pltpu.SemaphoreType.DMA,
            pltpu.SemaphoreType.DMA,
        ],
    )
    return pl.pallas_call(
        functools.partial(_dispatch_body, chunk=chunk, bm=bm),
        grid_spec=grid_spec,
        out_shape=jax.ShapeDtypeStruct((n_blocks * bm * ROW_TILE, LANES), F32),
        compiler_params=_params(("arbitrary",), disable_bounds_checks=True),
        name="dispatch",
    )(dest, pad_start, pad_n, n_used, h2_tiles)


def _expert_body(blk_e_ref, n_used_ref, first_ref, next_e_ref, parity_ref, xs_ref, wgu_hbm, bgu_ref, wdn_hbm,
                 bdn_ref, y_ref, wgu_f32, wdn_f32, wsem, wgu_bf, wdn_bf, *, layer):
    i = pl.program_id(0)
    bm = y_ref.shape[0] // ROW_TILE
    n_used = n_used_ref[0]

    def weight_copies(e, s):
        return (pltpu.make_async_copy(wgu_hbm.at[layer, e], wgu_f32.at[s], wsem.at[s]),
                pltpu.make_async_copy(wdn_hbm.at[layer, e], wdn_f32.at[s], wsem.at[s]))

    @pl.when(i == 0)
    def _():
        for cp in weight_copies(blk_e_ref[0], 0):
            cp.start()

    @pl.when(i < n_used)
    def _():
        @pl.when(first_ref[i] == 1)
        def _():
            s = parity_ref[i]
            for cp in weight_copies(blk_e_ref[i], s):
                cp.wait()

            @pl.when(next_e_ref[i] >= 0)
            def _():
                for cp in weight_copies(next_e_ref[i], 1 - s):
                    cp.start()

            wgu_bf[...] = wgu_f32[s].astype(BF16)
            wdn_bf[...] = wdn_f32[s].astype(BF16)

        x = _load_row_tiles(xs_ref, bm).astype(BF16)
        acc = jnp.zeros((bm, D_MODEL), F32)
        for f0 in range(0, D_FF, FF_CHUNK):
            g = jnp.dot(x, wgu_bf[:, f0:f0 + FF_CHUNK], preferred_element_type=F32) + bgu_ref[:, f0:f0 + FF_CHUNK]
            u = (jnp.dot(x, wgu_bf[:, D_FF + f0:D_FF + f0 + FF_CHUNK], preferred_element_type=F32)
                 + bgu_ref[:, D_FF + f0:D_FF + f0 + FF_CHUNK])
            g = jnp.minimum(g, SWIGLU_LIMIT)
            u = jnp.clip(u, -SWIGLU_LIMIT, SWIGLU_LIMIT)
            act = (u + 1.0) * (g * _sigmoid(SWIGLU_ALPHA * g))
            acc = acc + jnp.dot(act.astype(BF16), wdn_bf[f0:f0 + FF_CHUNK, :], preferred_element_type=F32)
        _store_row_tiles(y_ref, acc + bdn_ref[...])

    @pl.when(i >= n_used)
    def _():
        y_ref[...] = jnp.zeros_like(y_ref)


def _expert_runs(blk_e, n_used):
    n_blocks = blk_e.shape[0]
    idx = jnp.arange(n_blocks, dtype=jnp.int32)
    before = jnp.concatenate([jnp.full((1,), -1, jnp.int32), blk_e[:-1]])
    first = (blk_e != before) & (idx < n_used[0])
    parity = (jnp.cumsum(first.astype(jnp.int32)) - 1) % 2
    starts = jnp.where(first, idx, n_blocks)
    later = jnp.concatenate([lax.cummin(starts, reverse=True)[1:], jnp.full((1,), n_blocks, jnp.int32)])
    next_e = jnp.where(later < n_blocks, blk_e[jnp.minimum(later, n_blocks - 1)], -1)
    return first.astype(jnp.int32), next_e.astype(jnp.int32), parity.astype(jnp.int32)


def _experts(xs, layer, blk_e, n_used, w_gate_up, b_gate_up, w_down, b_down):
    bm = MOE_ROWS
    n_blocks = blk_e.shape[0]
    first, next_e, parity = _expert_runs(blk_e, n_used)
    grid_spec = pltpu.PrefetchScalarGridSpec(
        num_scalar_prefetch=5,
        grid=(n_blocks,),
        in_specs=[
            pl.BlockSpec((bm * ROW_TILE, LANES), lambda i, be, nu, *_: (jnp.minimum(i, nu[0] - 1), 0)),
            pl.BlockSpec(memory_space=pl.ANY),
            pl.BlockSpec((None, None, 1, 2 * D_FF), lambda i, be, *_: (layer, be[i], 0, 0)),
            pl.BlockSpec(memory_space=pl.ANY),
            pl.BlockSpec((None, None, 1, D_MODEL), lambda i, be, *_: (layer, be[i], 0, 0)),
        ],
        out_specs=pl.BlockSpec((bm * ROW_TILE, LANES), lambda i, *_: (i, 0)),
        scratch_shapes=[
            pltpu.VMEM((2, D_MODEL, 2 * D_FF), F32),
            pltpu.VMEM((2, D_FF, D_MODEL), F32),
            pltpu.SemaphoreType.DMA((2,)),
            pltpu.VMEM((D_MODEL, 2 * D_FF), BF16),
            pltpu.VMEM((D_FF, D_MODEL), BF16),
        ],
    )
    depth = w_gate_up.shape[0]
    return pl.pallas_call(
        functools.partial(_expert_body, layer=layer),
        grid_spec=grid_spec,
        out_shape=jax.ShapeDtypeStruct((n_blocks * bm * ROW_TILE, LANES), F32),
        compiler_params=_params(("arbitrary",)),
        name="experts",
    )(blk_e, n_used, first, next_e, parity, xs, w_gate_up, b_gate_up.reshape(depth, N_EXPERTS, 1, 2 * D_FF),
      w_down, b_down.reshape(depth, N_EXPERTS, 1, D_MODEL))


def _combine_body(geom, dest_ref, y_hbm, x1_ref, gate_ref, g2_ref, lng_ref, lnb_ref, o_ctx_ref, o_lat_ref,
                  buf, sem, *, dn_alpha):
    i = pl.program_id(0)
    n_steps = pl.num_programs(0)
    tc = o_ctx_ref.shape[0]

    def issue(step, slot):
        base = step * tc * TOP_K

        def one(r, carry):
            for k in range(TOP_K):
                pltpu.make_async_copy(_row_tile(y_hbm, dest_ref[base + r * TOP_K + k]),
                                      _row_tile(buf.at[slot, k], r), sem.at[slot]).start(priority=k % 2)
            return carry

        lax.fori_loop(0, tc, one, 0, unroll=4)

    @pl.when(i == 0)
    def _():
        issue(0, 0)

    @pl.when(i + 1 < n_steps)
    def _():
        issue(i + 1, (i + 1) % 2)

    slot = i % 2
    for k in range(TOP_K):
        pltpu.make_async_copy(y_hbm.at[pl.ds(0, tc * ROW_TILE), :], buf.at[slot, k], sem.at[slot]).wait()
    gate = gate_ref[...]
    moe = gate[:, 0:1] * _load_row_tiles(buf.at[slot, 0], tc)
    for k in range(1, TOP_K):
        moe = moe + gate[:, k:k + 1] * _load_row_tiles(buf.at[slot, k], tc)
    z = dn_alpha * x1_ref[...] + g2_ref[...] * moe
    out = _layernorm_rows(z, lng_ref[...], lnb_ref[...])
    ctx_steps = geom.tok_ctx // tc

    @pl.when(i < ctx_steps)
    def _():
        o_ctx_ref[...] = out

    @pl.when(i >= ctx_steps)
    def _():
        o_lat_ref[...] = out


def _combine(geom, y_rows, dest, x1, top_gate, mod, ln_g, ln_b, dn_alpha):
    tc = COMB_TILE
    n = geom.n_tok
    out_specs = _token_pair_specs(geom, tc)
    out_shape = [jax.ShapeDtypeStruct((geom.tok_ctx, D_MODEL), F32),
                 jax.ShapeDtypeStruct((geom.tok_lat, D_MODEL), F32)]
    grid_spec = pltpu.PrefetchScalarGridSpec(
        num_scalar_prefetch=1,
        grid=(n // tc,),
        in_specs=[
            pl.BlockSpec(memory_space=pl.ANY),
            pl.BlockSpec((tc, D_MODEL), lambda i, d: (i, 0)),
            pl.BlockSpec((tc, TOP_K), lambda i, d: (i, 0)),
            _mod_spec(geom, tc, 5),
            pl.BlockSpec((1, D_MODEL), lambda i, d: (0, 0)),
            pl.BlockSpec((1, D_MODEL), lambda i, d: (0, 0)),
        ],
        out_specs=out_specs,
        scratch_shapes=[
            pltpu.VMEM((2, TOP_K, tc * ROW_TILE, LANES), F32),
            pltpu.SemaphoreType.DMA((2,)),
        ],
    )
    return pl.pallas_call(
        functools.partial(_combine_body, geom, dn_alpha=dn_alpha),
        grid_spec=grid_spec,
        out_shape=out_shape,
        compiler_params=_params(("arbitrary",), disable_bounds_checks=True),
        name="combine",
    )(dest, y_rows, x1, top_gate, mod, ln_g, ln_b)


IN_SPLITS = (CONV_W, CONV_W, CONV_W, ATT_Q_W, ATT_KV_W, ATT_KV_W, GLA_QK_W, GLA_QK_W, GLA_V_W, GLA_V_W,
             GLA_GATE_RANK, GLA_GATE_RANK, D_MODEL, D_MODEL, D_MODEL)


def _w_in_body(w_ref, o_ref):
    w = w_ref[...]
    parts, o = [], 0
    for width in IN_SPLITS:
        parts.append(w[:, o:o + width])
        o += width
    u_a, b_a, c_a, q_b, k_b, v_b, q_c, k_c, v_c, r_c, za_f, za_b, gt_a, gt_b, gt_c = parts
    pad = jnp.zeros((w.shape[0], LANES - 2 * GLA_GATE_RANK), w.dtype)
    o_ref[...] = jnp.concatenate(
        [gt_a, gt_b, gt_c, u_a, b_a, c_a, q_b, v_c, r_c, q_c, k_c, k_b, v_b, za_f, za_b, pad], axis=1).astype(BF16)


def _rearrange_w_in(w_in):
    depth, d, in_cols = w_in.shape
    rows = 256
    return pl.pallas_call(
        _w_in_body,
        grid=(depth, d // rows),
        in_specs=[pl.BlockSpec((None, rows, in_cols), lambda l, i: (l, i, 0))],
        out_specs=pl.BlockSpec((None, rows, P_COLS), lambda l, i: (l, i, 0)),
        out_shape=jax.ShapeDtypeStruct((depth, d, P_COLS), BF16),
        compiler_params=_params(("arbitrary", "arbitrary")),
        name="w_in_layout",
    )(w_in)


def kernel(x_prompt, x_sample, cache_k, cache_v, state_gla, c, c_ctx, ada_w, ada_b, w_in, conv_w, attn_sink,
           gla_wa2, gla_ba2, gla_norm_g, w_branch, w_out, ln1_g, ln1_b, ln2_g, ln2_b, router_w, router_b,
           w_gate_up, b_gate_up, w_down, b_down):
    depth = w_in.shape[0]
    n_ctx, t_ctx, _ = x_prompt.shape
    n_lat, t_lat, _ = x_sample.shape
    geom = _Geom(n_ctx, t_ctx, n_lat, t_lat)
    dn_alpha = (2 * depth) ** 0.25

    mod_rows = -(-(n_lat + 1) // SUBLANES) * SUBLANES
    cond = jnp.zeros((mod_rows, D_MODEL), F32).at[:n_lat].set(c).at[n_lat].set(c_ctx)
    mods = _modulation(cond, ada_w, ada_b).reshape(depth, mod_rows, 1, 6 * D_MODEL)

    w_in_bf = _rearrange_w_in(w_in)
    w_branch_bf = w_branch.astype(BF16)
    w_out_bf = w_out.astype(BF16)
    conv_w_pad = jnp.zeros((depth, SUBLANES, CONV_W), F32).at[:, :conv_w.shape[1]].set(conv_w)
    rw_hi = router_w.astype(BF16)
    rw_lo = (router_w - rw_hi.astype(F32)).astype(BF16)
    w2 = jnp.zeros((depth, LANES, 2 * GLA_QK_W), F32)
    w2 = w2.at[:, :GLA_GATE_RANK, :GLA_QK_W].set(gla_wa2[:, 0])
    w2 = w2.at[:, GLA_GATE_RANK:2 * GLA_GATE_RANK, GLA_QK_W:].set(gla_wa2[:, 1]).astype(BF16)
    b2 = gla_ba2.reshape(depth, 1, 2 * GLA_QK_W)
    rope_tabs = _rope_tables(t_lat)
    s0t = jnp.transpose(state_gla, (0, 1, 2, 5, 3, 4)).reshape(n_lat, depth, 2, GLA_DV, GLA_QK_W)
    past = cache_k.shape[2]
    ck = cache_k.reshape(n_lat, depth, past, ATT_KV_W)
    cv = cache_v.reshape(n_lat, depth, past, ATT_KV_W)

    x = (x_prompt.reshape(-1, D_MODEL), x_sample.reshape(-1, D_MODEL))
    new_k, new_v, new_s = [], [], []
    for l in range(depth):
        mod = mods[l]
        lw = dict(w_branch=w_branch_bf[l], w_out=w_out_bf[l], conv_w=conv_w_pad[l],
                  ln1_g=ln1_g[l].reshape(1, -1), ln1_b=ln1_b[l].reshape(1, -1),
                  rw_hi=rw_hi[l], rw_lo=rw_lo[l], r_b=router_b[l].reshape(1, -1))
        p = _in_proj(geom, x, mod, w_in_bf, l)
        kv_shape = (n_ctx, t_ctx, ATT_KV_HEADS, HEAD_DIM)
        new_k.append(p[:geom.tok_ctx, COL_KB:COL_KB + ATT_KV_W].astype(F32).reshape(kv_shape))
        new_v.append(p[:geom.tok_ctx, COL_VB:COL_VB + ATT_KV_W].astype(F32).reshape(kv_shape))
        yb = _attn_lat(geom, p, ck[:, l], cv[:, l], rope_tabs, attn_sink[l], _attn_ctx(geom, p, attn_sink[l]))
        g_norm = gla_norm_g[l].reshape(1, GLA_DV)
        yc, s_ctx = _gla(geom, p, False, w2[l], b2[l], g_norm)
        yc, _ = _gla(geom, p, True, w2[l], b2[l], g_norm, s0t[:, l], yc)
        new_s.append(jnp.transpose(s_ctx.reshape(n_ctx, 2, GLA_DV, GLA_HEADS, GLA_DK), (0, 1, 3, 4, 2)))
        x1, h2_tiles, top_idx, top_gate, rank, tile_counts = _merge(geom, x, p, yb, yc, mod, lw, dn_alpha)
        dest, blk_e, n_used, pad_start, pad_n = _route(top_idx, rank, tile_counts, MOE_ROWS, TOK_TILE)
        xs = _dispatch(h2_tiles, dest, pad_start, pad_n, n_used, blk_e.shape[0])
        y_rows = _experts(xs, l, blk_e, n_used, w_gate_up, b_gate_up, w_down, b_down)
        x = _combine(geom, y_rows, dest, x1, top_gate, mod, ln2_g[l].reshape(1, -1), ln2_b[l].reshape(1, -1),
                     dn_alpha)
    y_prompt = x[0].reshape(n_ctx, t_ctx, D_MODEL)
    y_sample = x[1].reshape(n_lat, t_lat, D_MODEL)
    return (y_prompt, y_sample, jnp.stack(new_k, axis=1), jnp.stack(new_v, axis=1), jnp.stack(new_s, axis=1))
```

```python
import functools

import jax
import jax.numpy as jnp
from jax import lax
from jax.experimental import pallas as pl
from jax.experimental.pallas import tpu as pltpu

F32 = jnp.float32
BF16 = jnp.bfloat16
HIGHEST = lax.Precision.HIGHEST

D_MODEL = 1024
MIX_W = D_MODEL // 2
CONV_W = MIX_W
HEAD_DIM = 64
ATT_HEADS = MIX_W // HEAD_DIM
ATT_KV_HEADS = 2
ATT_GROUP = ATT_HEADS // ATT_KV_HEADS
ATT_Q_W = ATT_HEADS * HEAD_DIM
ATT_KV_W = ATT_KV_HEADS * HEAD_DIM
WINDOW = 128
ATT_BLOCK = 128
ATT_SCALE = HEAD_DIM ** -0.5
ROPE_THETA = 10000.0
NEG_INF = -1e30
GRID_W = 64
GLA_HEADS = 4
GLA_DK = MIX_W // 2 // GLA_HEADS
GLA_DV = MIX_W // GLA_HEADS
GLA_QK_W = GLA_HEADS * GLA_DK
GLA_V_W = GLA_HEADS * GLA_DV
GLA_GATE_RANK = 16
GLA_GATE_NORM = 16.0
GLA_CHUNK = 64
N_EXPERTS = 32
TOP_K = 4
D_FF = D_MODEL
SWIGLU_LIMIT = 7.0
SWIGLU_ALPHA = 1.702
LN_EPS = 1e-5

LANES = 128
SUBLANES = 8
BF16_SUBLANES = 16
VMEM_LIMIT_BYTES = 56 * 1024 * 1024

COL_GATES = 0
COL_CONV = 3 * D_MODEL
COL_QB = COL_CONV + 3 * CONV_W
COL_VC = COL_QB + ATT_Q_W
COL_RC = COL_VC + GLA_V_W
COL_QC = COL_RC + GLA_V_W
COL_KC = COL_QC + GLA_QK_W
COL_KB = COL_KC + GLA_QK_W
COL_VB = COL_KB + ATT_KV_W
COL_ZA = COL_VB + ATT_KV_W
P_COLS = COL_ZA + LANES

TOK_TILE = 256
IN_TILE = 512
MOE_ROWS = 512
COMB_TILE = 256
DISPATCH_CHUNK = 2048
PAD_GROUP = 16
FF_CHUNK = 1024

ROW_TILE = D_MODEL // LANES


def _params(semantics, **kw):
    return pltpu.CompilerParams(dimension_semantics=semantics, vmem_limit_bytes=VMEM_LIMIT_BYTES, **kw)


def _sigmoid(x):
    return 1.0 / (1.0 + jnp.exp(-x))


def _layernorm_rows(z, g, b):
    mu = jnp.mean(z, axis=-1, keepdims=True)
    d = z - mu
    var = jnp.mean(d * d, axis=-1, keepdims=True)
    return d * lax.rsqrt(var + LN_EPS) * g + b


def _store_row_tiles(ref, x):
    rows = x.shape[0]
    for j in range(ROW_TILE):
        ref[pl.ds(j, rows, stride=ROW_TILE), :] = x[:, j * LANES:(j + 1) * LANES]


def _load_row_tiles(ref, rows):
    return jnp.concatenate([ref[pl.ds(j, rows, stride=ROW_TILE), :] for j in range(ROW_TILE)], axis=1)


def _ada_body(c_ref, w_ref, b_ref, o_ref):
    c = c_ref[...]
    s = c * _sigmoid(c)
    o_ref[...] = jnp.dot(s, w_ref[...], precision=HIGHEST, preferred_element_type=F32) + b_ref[...]


def _modulation(cond, ada_w, ada_b):
    depth, d, n6 = ada_w.shape
    rows = cond.shape[0]
    tn = 1536
    return pl.pallas_call(
        _ada_body,
        grid=(depth, n6 // tn),
        in_specs=[
            pl.BlockSpec((rows, d), lambda l, j: (0, 0)),
            pl.BlockSpec((None, d, tn), lambda l, j: (l, 0, j)),
            pl.BlockSpec((None, 1, tn), lambda l, j: (l, 0, j)),
        ],
        out_specs=pl.BlockSpec((None, rows, tn), lambda l, j: (l, 0, j)),
        out_shape=jax.ShapeDtypeStruct((depth, rows, n6), F32),
        compiler_params=_params(("arbitrary", "arbitrary")),
        name="modulation",
    )(cond, ada_w, ada_b.reshape(depth, 1, n6))


class _Geom:
    def __init__(self, n_ctx, t_ctx, n_lat, t_lat):
        self.n_ctx, self.t_ctx, self.n_lat, self.t_lat = n_ctx, t_ctx, n_lat, t_lat
        self.tok_ctx = n_ctx * t_ctx
        self.tok_lat = n_lat * t_lat
        self.n_tok = self.tok_ctx + self.tok_lat
        self.ctx_row = n_lat
        assert t_ctx % TOK_TILE == 0 and t_lat % TOK_TILE == 0 and self.tok_ctx % t_lat == 0

    def seq_of_tile(self, i, tile):
        ctx_tiles = self.tok_ctx // tile
        return jnp.where(i < ctx_tiles, self.ctx_row, (i - ctx_tiles) // (self.t_lat // tile))


def _mod_spec(geom, tile, which):
    return pl.BlockSpec((None, 1, D_MODEL), lambda i, *_: (geom.seq_of_tile(i, tile), 0, which))


IN_CHUNK = 1408


def _token_pair_specs(geom, tile):
    ctx_tiles = geom.tok_ctx // tile
    return [pl.BlockSpec((tile, D_MODEL), lambda i, *_: (jnp.minimum(i, ctx_tiles - 1), 0)),
            pl.BlockSpec((tile, D_MODEL), lambda i, *_: (jnp.maximum(i - ctx_tiles, 0), 0))]


def _token_tile(geom, x_ctx_ref, x_lat_ref):
    tile = x_ctx_ref.shape[0]
    is_ctx = pl.program_id(0) < geom.tok_ctx // tile
    return jnp.where(is_ctx, x_ctx_ref[...], x_lat_ref[...])


def _inproj_body(geom, xc_ref, xl_ref, sc_ref, sh_ref, w_ref, o_ref):
    h = (_token_tile(geom, xc_ref, xl_ref) * (1.0 + sc_ref[...]) + sh_ref[...]).astype(BF16)
    for c0 in range(0, P_COLS, IN_CHUNK):
        o_ref[:, c0:c0 + IN_CHUNK] = jnp.dot(h, w_ref[:, c0:c0 + IN_CHUNK],
                                             preferred_element_type=F32).astype(o_ref.dtype)


def _in_proj(geom, x_pair, mod, w_in_bf, layer):
    tm = IN_TILE
    return pl.pallas_call(
        functools.partial(_inproj_body, geom),
        grid=(geom.n_tok // tm,),
        in_specs=_token_pair_specs(geom, tm) + [
            _mod_spec(geom, tm, 1),
            _mod_spec(geom, tm, 0),
            pl.BlockSpec((None, D_MODEL, P_COLS), lambda i: (layer, 0, 0), pipeline_mode=pl.Buffered(1)),
        ],
        out_specs=pl.BlockSpec((tm, P_COLS), lambda i: (i, 0)),
        out_shape=jax.ShapeDtypeStruct((geom.n_tok, P_COLS), BF16),
        compiler_params=_params(("arbitrary",)),
        name="in_proj",
    )(x_pair[0], x_pair[1], mod, mod, w_in_bf)


def _nt_dot(a, b):
    return lax.dot_general(a, b, (((1,), (1,)), ((), ())), preferred_element_type=F32)


def _fold_sublanes(x, op):
    out = x[0:SUBLANES]
    for r0 in range(SUBLANES, x.shape[0], SUBLANES):
        out = op(out, x[r0:r0 + SUBLANES])
    return out


def _split_kv(k_blocks, v_blocks):
    keys, vts = [], []
    for g in range(ATT_KV_HEADS):
        hs = slice(g * HEAD_DIM, (g + 1) * HEAD_DIM)
        keys.append([kb[:, hs].astype(BF16) for kb in k_blocks])
        vts.append([vb[:, hs].astype(F32).T.astype(BF16) for vb in v_blocks])
    return keys, vts


def _attend(q, keys, vts, biases, sink_ref):
    rows = q.shape[0]
    qb = q.astype(BF16)
    pair_outs = []
    for g in range(ATT_KV_HEADS):
        heads = range(g * ATT_GROUP, (g + 1) * ATT_GROUP)
        q_group = jnp.concatenate([qb[:, h * HEAD_DIM:(h + 1) * HEAD_DIM] for h in heads], axis=0)
        sink = jnp.concatenate([jnp.full((SUBLANES, rows), sink_ref[h], F32) for h in heads], axis=1)
        bias_group = [None if b is None else jnp.concatenate([b] * ATT_GROUP, axis=1) for b in biases]
        scores = []
        m_part = sink
        for kb, bias in zip(keys[g], bias_group):
            s = _nt_dot(kb, q_group)
            if bias is not None:
                s = s + bias
            scores.append(s)
            m_part = jnp.maximum(m_part, _fold_sublanes(s, jnp.maximum))
        m = jnp.max(m_part, axis=0, keepdims=True)
        d_part = jnp.zeros_like(m_part)
        acc = jnp.zeros((HEAD_DIM, ATT_GROUP * rows), F32)
        for s, vt in zip(scores, vts[g]):
            p = jnp.exp(s - m)
            d_part = d_part + _fold_sublanes(p, jnp.add)
            acc = acc + jnp.dot(vt, p.astype(BF16), preferred_element_type=F32)
        den = jnp.sum(d_part, axis=0, keepdims=True) + jnp.exp(sink[0:1] - m)
        out_t = acc / den
        for a in range(0, ATT_GROUP, 2):
            pair_outs.append(jnp.concatenate([out_t[:, a * rows:(a + 1) * rows],
                                              out_t[:, (a + 1) * rows:(a + 2) * rows]], axis=0).T)
    return jnp.concatenate(pair_outs, axis=1)


def _attn_ctx_body(q_ref, k_ref, v_ref, sink_ref, o_ref, *, n_seq):
    t = q_ref.shape[0]

    @pl.when(pl.program_id(0) < n_seq)
    def _():
        blk = ATT_BLOCK
        q = q_ref[...].astype(F32) * ATT_SCALE
        k = k_ref[...]
        v = v_ref[...]
        blocks = [slice(r0, r0 + blk) for r0 in range(0, t, blk)]
        keys, vts = _split_kv([k[b] for b in blocks], [v[b] for b in blocks])
        o_ref[...] = _attend(q, keys, vts, [None] * len(blocks), sink_ref)

    @pl.when(pl.program_id(0) >= n_seq)
    def _():
        o_ref[...] = jnp.zeros_like(o_ref)


def _attn_ctx(geom, p, sink):
    t = geom.t_ctx
    last = geom.n_ctx - 1
    return pl.pallas_call(
        functools.partial(_attn_ctx_body, n_seq=geom.n_ctx),
        grid=(geom.n_tok // t,),
        in_specs=[
            pl.BlockSpec((t, ATT_Q_W), lambda b: (jnp.minimum(b, last), COL_QB // ATT_Q_W)),
            pl.BlockSpec((t, ATT_KV_W), lambda b: (jnp.minimum(b, last), COL_KB // ATT_KV_W)),
            pl.BlockSpec((t, ATT_KV_W), lambda b: (jnp.minimum(b, last), COL_VB // ATT_KV_W)),
            pl.BlockSpec(memory_space=pltpu.SMEM),
        ],
        out_specs=pl.BlockSpec((t, ATT_Q_W), lambda b: (b, 0)),
        out_shape=jax.ShapeDtypeStruct((geom.n_tok, ATT_Q_W), F32),
        compiler_params=_params(("arbitrary",)),
        name="attn_ctx",
    )(p, p, p, sink)


def _rope(x, cos, sin_lo, sin_hi):
    nf = HEAD_DIM // 4
    outs = []
    for j in range(x.shape[1] // LANES):
        xj = x[:, j * LANES:(j + 1) * LANES]
        outs.append(xj * cos + pltpu.roll(xj, LANES - nf, 1) * sin_lo + pltpu.roll(xj, nf, 1) * sin_hi)
    return outs[0] if len(outs) == 1 else jnp.concatenate(outs, axis=1)


def _attn_lat_body(q_ref, k_ref, v_ref, ck_ref, cv_ref, qcos_ref, qslo_ref, qshi_ref,
                   kcos_ref, kslo_ref, kshi_ref, sink_ref, yb_hbm, o_ref):
    del yb_hbm
    i = pl.program_id(1)
    nb = k_ref.shape[0] // ATT_BLOCK
    blk = ATT_BLOCK
    q = _rope(q_ref[...].astype(F32), qcos_ref[...], qslo_ref[...], qshi_ref[...]) * ATT_SCALE
    k_in_blk = lax.broadcasted_iota(jnp.int32, (blk, blk), 0)
    q_in_blk = lax.broadcasted_iota(jnp.int32, (blk, blk), 1)
    k_blocks, v_blocks, biases = [], [], []
    for j in (-1, 0, 1):
        kb = i + j
        kbc = jnp.clip(kb, 0, nb - 1)
        rows = pl.ds(pl.multiple_of(kbc * blk, blk), blk)
        k_blocks.append(_rope(k_ref[rows, :].astype(F32), kcos_ref[rows, :], kslo_ref[rows, :], kshi_ref[rows, :]))
        v_blocks.append(v_ref[rows, :])
        outside = jnp.where((kb >= 0) & (kb < nb), 0, 4 * WINDOW)
        dist = (j * blk + k_in_blk) - q_in_blk
        biases.append(jnp.where(jnp.abs(dist) + outside <= WINDOW, 0.0, NEG_INF))
    for r0 in range(0, ck_ref.shape[0], blk):
        k_blocks.append(ck_ref[r0:r0 + blk, :])
        v_blocks.append(cv_ref[r0:r0 + blk, :])
        biases.append(None)
    keys, vts = _split_kv(k_blocks, v_blocks)
    o_ref[...] = _attend(q, keys, vts, biases, sink_ref)


def _attn_lat(geom, p, cache_k, cache_v, rope_tabs, sink, yb):
    t, blk = geom.t_lat, ATT_BLOCK
    nb = t // blk
    past = cache_k.shape[1]
    q_row0 = geom.tok_ctx // blk
    s_row0 = geom.tok_ctx // t
    cos, slo, shi = rope_tabs
    q_tab = pl.BlockSpec((blk, LANES), lambda b, i: (i, 0))
    k_tab = pl.BlockSpec((t, LANES), lambda b, i: (0, 0))
    return pl.pallas_call(
        _attn_lat_body,
        grid=(geom.n_lat, nb),
        in_specs=[
            pl.BlockSpec((blk, ATT_Q_W), lambda b, i: (q_row0 + b * nb + i, COL_QB // ATT_Q_W)),
            pl.BlockSpec((t, ATT_KV_W), lambda b, i: (s_row0 + b, COL_KB // ATT_KV_W)),
            pl.BlockSpec((t, ATT_KV_W), lambda b, i: (s_row0 + b, COL_VB // ATT_KV_W)),
            pl.BlockSpec((None, past, ATT_KV_W), lambda b, i: (b, 0, 0)),
            pl.BlockSpec((None, past, ATT_KV_W), lambda b, i: (b, 0, 0)),
            q_tab, q_tab, q_tab, k_tab, k_tab, k_tab,
            pl.BlockSpec(memory_space=pltpu.SMEM),
            pl.BlockSpec(memory_space=pl.ANY),
        ],
        out_specs=pl.BlockSpec((blk, ATT_Q_W), lambda b, i: (q_row0 + b * nb + i, 0)),
        out_shape=jax.ShapeDtypeStruct((geom.n_tok, ATT_Q_W), F32),
        input_output_aliases={12: 0},
        compiler_params=_params(("arbitrary", "arbitrary")),
        name="attn_lat",
    )(p, p, p, cache_k, cache_v, cos, slo, shi, cos, slo, shi, sink, yb)


def _rope_tables(t_lat):
    half = HEAD_DIM // 2
    nf = half // 2
    tok = jnp.arange(t_lat)
    lane = jnp.arange(LANES) % HEAD_DIM
    pos = jnp.where(lane[None, :] < half, (tok // GRID_W)[:, None], (tok % GRID_W)[:, None]).astype(F32)
    inv = ROPE_THETA ** (-(lane % nf).astype(F32) / nf)
    ang = pos * inv[None, :]
    first = (lane % half) < nf
    cos = jnp.cos(ang)
    sin = jnp.sin(ang)
    sin_lo = jnp.where(first[None, :], -sin, 0.0)
    sin_hi = jnp.where(first[None, :], 0.0, sin)
    return cos, sin_lo, sin_hi


def _gla_body(*refs, has_s0, n_seq):
    y_ref = refs[10 if has_s0 else 8]

    @pl.when(pl.program_id(0) < n_seq)
    def _():
        _gla_sequence(*refs, has_s0=has_s0)

    @pl.when(pl.program_id(0) >= n_seq)
    def _():
        y_ref[...] = jnp.zeros_like(y_ref)


def _gla_sequence(*refs, has_s0):
    if has_s0:
        (q_ref, k_ref, v_ref, r_ref, za_ref, w2_ref, b2_ref, g_ref, s0_ref, _yc_hbm,
         y_ref, sfin_ref, bc_scr, o_scr, qe_scr, upd_scr, st_scr, last_scr) = refs
    else:
        (q_ref, k_ref, v_ref, r_ref, za_ref, w2_ref, b2_ref, g_ref,
         y_ref, sfin_ref, bc_scr, o_scr, qe_scr, upd_scr, st_scr, last_scr) = refs
    t = q_ref.shape[0]
    c, nh = GLA_CHUNK, GLA_HEADS
    n = t // c
    rb = nh * c

    def iota(shape, axis):
        return lax.broadcasted_iota(jnp.int32, shape, axis)

    same_chunk = (iota((rb, rb), 0) // c) == (iota((rb, rb), 1) // c)
    cum = (jnp.where(same_chunk & (iota((rb, rb), 0) >= iota((rb, rb), 1)), 1.0, 0.0).astype(BF16),
           jnp.where(same_chunk & (iota((rb, rb), 0) <= iota((rb, rb), 1)), 1.0, 0.0).astype(BF16))
    for r0 in range(0, t, rb):
        z = jnp.dot(za_ref[r0:r0 + rb, :].astype(BF16), w2_ref[...], preferred_element_type=F32) + b2_ref[...]
        la = (jnp.minimum(z, 0.0) - jnp.log1p(jnp.exp(-jnp.abs(z)))) * (1.0 / GLA_GATE_NORM)
        for d in (0, 1):
            rest = la[:, d * GLA_QK_W:(d + 1) * GLA_QK_W]
            bc = None
            for _ in range(3):
                term = rest.astype(BF16)
                rest = rest - term.astype(F32)
                part = jnp.dot(cum[d], term, preferred_element_type=F32)
                bc = part if bc is None else bc + part
            bc_scr[r0:r0 + rb, d * GLA_QK_W:(d + 1) * GLA_QK_W] = bc

    in_chunk = iota((c, nh * c), 1) % c
    keeps = (iota((c, nh * c), 0) >= in_chunk, iota((c, nh * c), 0) <= in_chunk)
    own_k = (iota((nh * c, GLA_QK_W), 0) // c) == (iota((nh * c, GLA_QK_W), 1) // GLA_DK)
    own_v = (iota((nh * c, GLA_V_W), 0) // c) == (iota((nh * c, GLA_V_W), 1) // GLA_DV)
    own_s = (iota((nh * GLA_DV, GLA_QK_W), 0) // GLA_DV) == (iota((nh * GLA_DV, GLA_QK_W), 1) // GLA_DK)
    lane_head = iota((GLA_DV, GLA_QK_W), 1) // GLA_DK

    def stack(x):
        return jnp.concatenate([x] * nh, axis=0)

    def chunk_rows(ch):
        return pl.ds(pl.multiple_of(ch * c, c), c)

    def local(ch, carry):
        rows = chunk_rows(ch)
        q = q_ref[rows, :].astype(F32) * (GLA_DK ** -0.5)
        k = k_ref[rows, :].astype(F32)
        v = v_ref[rows, :].astype(F32)
        v_bd = jnp.where(own_v, stack(v), 0.0).astype(BF16)
        vt = v.T.astype(BF16)
        o_sum = None
        for d in (0, 1):
            bc = bc_scr[rows, d * GLA_QK_W:(d + 1) * GLA_QK_W]
            last = bc[c - 1:c, :] if d == 0 else bc[0:1, :]
            qe = (q * jnp.exp(bc)).astype(BF16)
            kd = (k * jnp.exp(last - bc)).astype(BF16)
            k_bd = jnp.where(own_k, stack(k * jnp.exp(-bc)), 0.0).astype(BF16)
            qe_scr[d, rows, :] = qe
            last_scr[d, ch] = jnp.broadcast_to(last, (SUBLANES, GLA_QK_W))
            att = jnp.where(keeps[d], _nt_dot(qe, k_bd), 0.0)
            o_d = jnp.dot(att.astype(BF16), v_bd, preferred_element_type=F32)
            u = jnp.dot(vt, kd, preferred_element_type=F32)
            upd = jnp.where(lane_head == 0, u[:GLA_DV], 0.0)
            for h in range(1, nh):
                upd = jnp.where(lane_head == h, u[h * GLA_DV:(h + 1) * GLA_DV], upd)
            upd_scr[d, ch] = upd
            o_sum = o_d if o_sum is None else o_sum + o_d
        o_scr[rows, :] = o_sum
        return carry

    lax.fori_loop(0, n, local, 0, unroll=2)

    for d in (0, 1):
        def scan(step, st, d=d):
            ch = step if d == 0 else n - 1 - step
            st_scr[d, ch] = st
            return st * jnp.exp(last_scr[d, ch][0:1, :]) + upd_scr[d, ch]

        st0 = s0_ref[d] if has_s0 else jnp.zeros((GLA_DV, GLA_QK_W), F32)
        sfin_ref[d] = lax.fori_loop(0, n, scan, st0)

    def cross(ch, carry):
        rows = chunk_rows(ch)
        acc = o_scr[rows, :]
        for d in (0, 1):
            s_bd = jnp.where(own_s, stack(st_scr[d, ch]), 0.0).astype(BF16)
            acc = acc + _nt_dot(qe_scr[d, rows, :], s_bd)
        o_scr[rows, :] = acc
        return carry

    lax.fori_loop(0, n, cross, 0, unroll=2)

    g = g_ref[...]

    def finish(ch, carry):
        rows = chunk_rows(ch)
        of = o_scr[rows, :]
        normed = []
        for h in range(nh):
            oh = of[:, h * GLA_DV:(h + 1) * GLA_DV]
            mu = jnp.mean(oh, axis=-1, keepdims=True)
            dev = oh - mu
            var = jnp.mean(dev * dev, axis=-1, keepdims=True)
            normed.append(dev * lax.rsqrt(var + LN_EPS) * g)
        r = r_ref[rows, :].astype(F32)
        y_ref[rows, :] = jnp.concatenate(normed, axis=1) * (r * _sigmoid(r))
        return carry

    lax.fori_loop(0, n, finish, 0, unroll=2)


def _gla(geom, p, latent, w2, b2, norm_g, s0t=None, yc=None):
    if latent:
        t, n_seq, row0 = geom.t_lat, geom.n_lat, geom.tok_ctx // geom.t_lat
    else:
        t, n_seq, row0 = geom.t_ctx, geom.n_ctx, 0
    has_s0 = s0t is not None
    steps = n_seq if latent else geom.n_tok // t
    last = n_seq - 1

    def col(width, off):
        return pl.BlockSpec((t, width), lambda b: (row0 + jnp.minimum(b, last), off // width))

    in_specs = [
        col(GLA_QK_W, COL_QC), col(GLA_QK_W, COL_KC), col(GLA_V_W, COL_VC), col(GLA_V_W, COL_RC),
        col(LANES, COL_ZA),
        pl.BlockSpec((LANES, 2 * GLA_QK_W), lambda b: (0, 0)),
        pl.BlockSpec((1, 2 * GLA_QK_W), lambda b: (0, 0)),
        pl.BlockSpec((1, GLA_DV), lambda b: (0, 0)),
    ]
    args = [p, p, p, p, p, w2, b2, norm_g]
    if has_s0:
        in_specs += [pl.BlockSpec((None, 2, GLA_DV, GLA_QK_W), lambda b: (b, 0, 0, 0)),
                     pl.BlockSpec(memory_space=pl.ANY)]
        args += [s0t, yc]
    return pl.pallas_call(
        functools.partial(_gla_body, has_s0=has_s0, n_seq=n_seq),
        grid=(steps,),
        in_specs=in_specs,
        out_specs=[
            pl.BlockSpec((t, GLA_V_W), lambda b: (row0 + b, 0)),
            pl.BlockSpec((None, 2, GLA_DV, GLA_QK_W), lambda b: (jnp.minimum(b, last), 0, 0, 0)),
        ],
        out_shape=[
            jax.ShapeDtypeStruct((geom.n_tok, GLA_V_W), F32),
            jax.ShapeDtypeStruct((n_seq, 2, GLA_DV, GLA_QK_W), F32),
        ],
        scratch_shapes=[
            pltpu.VMEM((t, 2 * GLA_QK_W), F32),
            pltpu.VMEM((t, GLA_V_W), F32),
            pltpu.VMEM((2, t, GLA_QK_W), BF16),
            pltpu.VMEM((2, t // GLA_CHUNK, GLA_DV, GLA_QK_W), F32),
            pltpu.VMEM((2, t // GLA_CHUNK, GLA_DV, GLA_QK_W), F32),
            pltpu.VMEM((2, t // GLA_CHUNK, SUBLANES, GLA_QK_W), F32),
        ],
        input_output_aliases={len(args) - 1: 0} if has_s0 else {},
        compiler_params=_params(("arbitrary",)),
        name="gla_lat" if latent else "gla_ctx",
    )(*args)


def _merge_body(geom, xc_ref, xl_ref, gt_ref, cv_ref, cvp_ref, cvn_ref, yb_ref, yc_ref, wbr_ref, wout_ref, cw_ref,
                g1_ref, sc2_ref, sh2_ref, lng_ref, lnb_ref, rwh_ref, rwl_ref, rb_ref,
                x1_ref, h2_ref, tidx_ref, tgate_ref, trank_ref, tcount_ref, *, dn_alpha):
    i = pl.program_id(0)
    tm = xc_ref.shape[0]
    ctx_tiles = geom.tok_ctx // tm
    per_seq = jnp.where(i < ctx_tiles, geom.t_ctx // tm, geom.t_lat // tm)
    j = jnp.where(i < ctx_tiles, i, i - ctx_tiles) % per_seq
    has_prev = (j > 0).astype(F32)
    has_next = (j < per_seq - 1).astype(F32)

    cv = cv_ref[...].astype(F32)
    u, bg, cg = cv[:, :CONV_W], cv[:, CONV_W:2 * CONV_W], cv[:, 2 * CONV_W:]
    v = cg * u
    halo = cvp_ref.shape[0]
    pv = cvp_ref[...].astype(F32)[halo - 1:halo, :]
    nv = cvn_ref[...].astype(F32)[0:1, :]
    v_before = has_prev * (pv[:, 2 * CONV_W:] * pv[:, :CONV_W])
    v_after = has_next * (nv[:, 2 * CONV_W:] * nv[:, :CONV_W])
    row = lax.broadcasted_iota(jnp.int32, (tm, 1), 0)
    v_prev = jnp.where(row == 0, v_before, pltpu.roll(v, 1, 0))
    v_next = jnp.where(row == tm - 1, v_after, pltpu.roll(v, tm - 1, 0))
    cw = cw_ref[...]
    y_a = bg * (cw[0:1, :] * v_prev + cw[1:2, :] * v + cw[2:3, :] * v_next)

    gt = gt_ref[...].astype(F32)
    merged = (_sigmoid(gt[:, :D_MODEL]) * jnp.dot(y_a.astype(BF16), wbr_ref[0], preferred_element_type=F32)
              + _sigmoid(gt[:, D_MODEL:2 * D_MODEL])
              * jnp.dot(yb_ref[...].astype(BF16), wbr_ref[1], preferred_element_type=F32)
              + _sigmoid(gt[:, 2 * D_MODEL:])
              * jnp.dot(yc_ref[...].astype(BF16), wbr_ref[2], preferred_element_type=F32))
    mix = jnp.dot(merged.astype(BF16), wout_ref[...], preferred_element_type=F32)
    x1 = _layernorm_rows(dn_alpha * _token_tile(geom, xc_ref, xl_ref) + g1_ref[...] * mix,
                         lng_ref[...], lnb_ref[...])
    x1_ref[...] = x1
    h2 = x1 * (1.0 + sc2_ref[...]) + sh2_ref[...]
    _store_row_tiles(h2_ref, h2)

    h_hi = h2.astype(BF16)
    h_lo = (h2 - h_hi.astype(F32)).astype(BF16)
    logits = (jnp.dot(h_hi, rwh_ref[...], preferred_element_type=F32)
              + jnp.dot(h_lo, rwh_ref[...], preferred_element_type=F32)
              + jnp.dot(h_hi, rwl_ref[...], preferred_element_type=F32)) + rb_ref[...]
    lane = lax.broadcasted_iota(jnp.int32, logits.shape, 1).astype(F32)
    vals, top_v, top_i = logits, [], []
    for _ in range(TOP_K):
        m = jnp.max(vals, axis=-1, keepdims=True)
        am = jnp.min(jnp.where(vals == m, lane, float(N_EXPERTS)), axis=-1, keepdims=True)
        top_v.append(m)
        top_i.append(am)
        vals = jnp.where(lane == am, -jnp.inf, vals)
    ex = [jnp.exp(tv - top_v[0]) for tv in top_v]
    den = ex[0] + ex[1] + ex[2] + ex[3]
    slot = lax.broadcasted_iota(jnp.int32, (tm, TOP_K), 1)
    gates, idxs = ex[TOP_K - 1], top_i[TOP_K - 1]
    for k in range(TOP_K - 2, -1, -1):
        gates = jnp.where(slot == k, ex[k], gates)
        idxs = jnp.where(slot == k, top_i[k], idxs)
    tgate_ref[...] = gates / den
    tidx_ref[...] = idxs.astype(jnp.int32)

    onehots = [(lane == ti).astype(F32) for ti in top_i]
    member = onehots[0] + onehots[1] + onehots[2] + onehots[3]
    earlier = (lax.broadcasted_iota(jnp.int32, (tm, tm), 0) > lax.broadcasted_iota(jnp.int32, (tm, tm), 1))
    before = jnp.dot(earlier.astype(BF16), member.astype(BF16), preferred_element_type=F32)
    ranks = jnp.sum(onehots[TOP_K - 1] * before, axis=-1, keepdims=True)
    for k in range(TOP_K - 2, -1, -1):
        ranks = jnp.where(slot == k, jnp.sum(onehots[k] * before, axis=-1, keepdims=True), ranks)
    trank_ref[...] = ranks.astype(jnp.int32)
    tcount_ref[...] = (before[tm - 1:tm, :] + member[tm - 1:tm, :]).astype(jnp.int32)


def _merge(geom, x_pair, p, yb, yc, mod, lw, dn_alpha):
    tm = TOK_TILE
    n = geom.n_tok
    halo = BF16_SUBLANES
    halo_per_tile = tm // halo
    n_halo = n // halo
    conv_blk = COL_CONV // (3 * CONV_W)
    const2 = lambda i: (0, 0)
    return pl.pallas_call(
        functools.partial(_merge_body, geom, dn_alpha=dn_alpha),
        grid=(n // tm,),
        in_specs=_token_pair_specs(geom, tm) + [
            pl.BlockSpec((tm, 3 * D_MODEL), lambda i: (i, 0)),
            pl.BlockSpec((tm, 3 * CONV_W), lambda i: (i, conv_blk)),
            pl.BlockSpec((halo, 3 * CONV_W), lambda i: (jnp.maximum(i * halo_per_tile - 1, 0), conv_blk)),
            pl.BlockSpec((halo, 3 * CONV_W),
                         lambda i: (jnp.minimum((i + 1) * halo_per_tile, n_halo - 1), conv_blk)),
            pl.BlockSpec((tm, MIX_W), lambda i: (i, 0)),
            pl.BlockSpec((tm, MIX_W), lambda i: (i, 0)),
            pl.BlockSpec((3, MIX_W, D_MODEL), lambda i: (0, 0, 0)),
            pl.BlockSpec((D_MODEL, D_MODEL), const2),
            pl.BlockSpec((SUBLANES, CONV_W), const2),
            _mod_spec(geom, tm, 2), _mod_spec(geom, tm, 4), _mod_spec(geom, tm, 3),
            pl.BlockSpec((1, D_MODEL), const2), pl.BlockSpec((1, D_MODEL), const2),
            pl.BlockSpec((D_MODEL, N_EXPERTS), const2), pl.BlockSpec((D_MODEL, N_EXPERTS), const2),
            pl.BlockSpec((1, N_EXPERTS), const2),
        ],
        out_specs=[
            pl.BlockSpec((tm, D_MODEL), lambda i: (i, 0)),
            pl.BlockSpec((tm * ROW_TILE, LANES), lambda i: (i, 0)),
            pl.BlockSpec((tm, TOP_K), lambda i: (i, 0)),
            pl.BlockSpec((tm, TOP_K), lambda i: (i, 0)),
            pl.BlockSpec((tm, TOP_K), lambda i: (i, 0)),
            pl.BlockSpec((None, 1, N_EXPERTS), lambda i: (i, 0, 0)),
        ],
        out_shape=[
            jax.ShapeDtypeStruct((n, D_MODEL), F32),
            jax.ShapeDtypeStruct((n * ROW_TILE, LANES), F32),
            jax.ShapeDtypeStruct((n, TOP_K), jnp.int32),
            jax.ShapeDtypeStruct((n, TOP_K), F32),
            jax.ShapeDtypeStruct((n, TOP_K), jnp.int32),
            jax.ShapeDtypeStruct((n // tm, 1, N_EXPERTS), jnp.int32),
        ],
        compiler_params=_params(("arbitrary",)),
        name="merge",
    )(x_pair[0], x_pair[1], p, p, p, p, yb, yc, lw["w_branch"], lw["w_out"], lw["conv_w"], mod, mod, mod,
      lw["ln1_g"], lw["ln1_b"], lw["rw_hi"], lw["rw_lo"], lw["r_b"])


def _route(top_idx, rank, tile_counts, bm, tile):
    n_tok = top_idx.shape[0]
    n_tiles = n_tok // tile
    tcnt = tile_counts.reshape(n_tiles, N_EXPERTS)
    counts = jnp.sum(tcnt, axis=0)
    padded = (counts + bm - 1) // bm * bm
    pend = jnp.cumsum(padded)
    pstart = pend - padded
    tile_base = pstart[None, :] + jnp.cumsum(tcnt, axis=0) - tcnt
    tok_base = jnp.broadcast_to(tile_base[:, None, :], (n_tiles, tile, N_EXPERTS)).reshape(n_tok, 1, N_EXPERTS)
    onehot = top_idx[:, :, None] == jnp.arange(N_EXPERTS, dtype=jnp.int32)[None, None, :]
    dest = (jnp.sum(jnp.where(onehot, tok_base, 0), axis=-1) + rank).reshape(-1).astype(jnp.int32)
    n_blocks = n_tok * TOP_K // bm + N_EXPERTS
    blk_start = jnp.arange(n_blocks, dtype=jnp.int32) * bm
    blk_e = jnp.minimum(jnp.sum((pend[None, :] <= blk_start[:, None]).astype(jnp.int32), axis=1), N_EXPERTS - 1)
    n_used = (pend[-1] // bm).astype(jnp.int32).reshape(1)
    return dest, blk_e.astype(jnp.int32), n_used, (pstart + counts).astype(jnp.int32), (padded - counts).astype(
        jnp.int32)


def _row_tile(ref, row):
    return ref.at[pl.ds(pl.multiple_of(row * ROW_TILE, ROW_TILE), ROW_TILE), :]


def _dispatch_body(dest_ref, pad_start_ref, pad_n_ref, n_used_ref, h_ref, xs_hbm, zbuf, sem, zsem, gsem, bsem,
                   *, chunk, bm):
    i = pl.program_id(0)
    n_blocks = xs_hbm.shape[0] // (bm * ROW_TILE)

    @pl.when(i == 0)
    def _():
        zbuf[...] = jnp.zeros_like(zbuf)
        zrow = zbuf.at[pl.ds(0, ROW_TILE), :]
        zgroup = zbuf.at[pl.ds(0, PAD_GROUP * ROW_TILE), :]

        def group_copy(row):
            return pltpu.make_async_copy(
                zgroup, xs_hbm.at[pl.ds(pl.multiple_of(row * ROW_TILE, ROW_TILE), PAD_GROUP * ROW_TILE), :], gsem)

        def per_expert(e, totals):
            start, cnt = pad_start_ref[e], pad_n_ref[e]
            groups = cnt // PAD_GROUP

            def many(j, carry):
                group_copy(start + j * PAD_GROUP).start()
                return carry

            def one(j, carry):
                pltpu.make_async_copy(zrow, _row_tile(xs_hbm, start + j), zsem).start()
                return carry

            lax.fori_loop(0, groups, many, 0)
            lax.fori_loop(groups * PAD_GROUP, cnt, one, 0)
            return totals[0] + groups, totals[1] + cnt - groups * PAD_GROUP

        total_groups, total = lax.fori_loop(0, N_EXPERTS, per_expert, (0, 0))

        def block_copy(b):
            return pltpu.make_async_copy(zbuf, xs_hbm.at[pl.ds(pl.multiple_of(b * bm * ROW_TILE, bm * ROW_TILE),
                                                                bm * ROW_TILE), :], bsem)

        def start_block(b, carry):
            block_copy(b).start()
            return carry

        def wait_block(b, carry):
            block_copy(b).wait()
            return carry

        lax.fori_loop(n_used_ref[0], n_blocks, start_block, 0)

        def wait_one(j, carry):
            pltpu.make_async_copy(zrow, _row_tile(xs_hbm, 0), zsem).wait()
            return carry

        def wait_group(j, carry):
            group_copy(0).wait()
            return carry

        lax.fori_loop(0, total_groups, wait_group, 0)
        lax.fori_loop(0, total, wait_one, 0)
        lax.fori_loop(n_used_ref[0], n_blocks, wait_block, 0)

    def one_token(r, carry):
        t = i * chunk + r
        for k in range(TOP_K):
            pltpu.make_async_copy(_row_tile(h_ref, r), _row_tile(xs_hbm, dest_ref[t * TOP_K + k]),
                                  sem).start(priority=k % 2)
        return carry

    lax.fori_loop(0, chunk, one_token, 0, unroll=4)
    rows = chunk * TOP_K * ROW_TILE
    pltpu.make_async_copy(xs_hbm.at[pl.ds(0, rows), :], xs_hbm.at[pl.ds(0, rows), :], sem).wait()


def _dispatch(h2_tiles, dest, pad_start, pad_n, n_used, n_blocks):
    chunk, bm = DISPATCH_CHUNK, MOE_ROWS
    n_tok = h2_tiles.shape[0] // ROW_TILE
    grid_spec = pltpu.PrefetchScalarGridSpec(
        num_scalar_prefetch=4,
        grid=(n_tok // chunk,),
        in_specs=[pl.BlockSpec((chunk * ROW_TILE, LANES), lambda i, *_: (i, 0))],
        out_specs=pl.BlockSpec(memory_space=pl.ANY),
        scratch_shapes=[
            pltpu.VMEM((bm * ROW_TILE, LANES), F32),
            pltpu.SemaphoreType.DMA,
            pltpu.SemaphoreType.DMA,
            pltpu.SemaphoreType.DMA,
            pltpu.SemaphoreType.DMA,
        ],
    )
    return pl.pallas_call(
        functools.partial(_dispatch_body, chunk=chunk, bm=bm),
        grid_spec=grid_spec,
        out_shape=jax.ShapeDtypeStruct((n_blocks * bm * ROW_TILE, LANES), F32),
        compiler_params=_params(("arbitrary",), disable_bounds_checks=True),
        name="dispatch",
    )(dest, pad_start, pad_n, n_used, h2_tiles)


def _expert_body(blk_e_ref, n_used_ref, first_ref, next_e_ref, parity_ref, xs_ref, wgu_hbm, bgu_ref, wdn_hbm,
                 bdn_ref, y_ref, wgu_f32, wdn_f32, wsem, wgu_bf, wdn_bf, *, layer):
    i = pl.program_id(0)
    bm = y_ref.shape[0] // ROW_TILE
    n_used = n_used_ref[0]

    def weight_copies(e, s):
        return (pltpu.make_async_copy(wgu_hbm.at[layer, e], wgu_f32.at[s], wsem.at[s]),
                pltpu.make_async_copy(wdn_hbm.at[layer, e], wdn_f32.at[s], wsem.at[s]))

    @pl.when(i == 0)
    def _():
        for cp in weight_copies(blk_e_ref[0], 0):
            cp.start()

    @pl.when(i < n_used)
    def _():
        @pl.when(first_ref[i] == 1)
        def _():
            s = parity_ref[i]
            for cp in weight_copies(blk_e_ref[i], s):
                cp.wait()

            @pl.when(next_e_ref[i] >= 0)
            def _():
                for cp in weight_copies(next_e_ref[i], 1 - s):
                    cp.start()

            wgu_bf[...] = wgu_f32[s].astype(BF16)
            wdn_bf[...] = wdn_f32[s].astype(BF16)

        x = _load_row_tiles(xs_ref, bm).astype(BF16)
        acc = jnp.zeros((bm, D_MODEL), F32)
        for f0 in range(0, D_FF, FF_CHUNK):
            g = jnp.dot(x, wgu_bf[:, f0:f0 + FF_CHUNK], preferred_element_type=F32) + bgu_ref[:, f0:f0 + FF_CHUNK]
            u = (jnp.dot(x, wgu_bf[:, D_FF + f0:D_FF + f0 + FF_CHUNK], preferred_element_type=F32)
                 + bgu_ref[:, D_FF + f0:D_FF + f0 + FF_CHUNK])
            g = jnp.minimum(g, SWIGLU_LIMIT)
            u = jnp.clip(u, -SWIGLU_LIMIT, SWIGLU_LIMIT)
            act = (u + 1.0) * (g * _sigmoid(SWIGLU_ALPHA * g))
            acc = acc + jnp.dot(act.astype(BF16), wdn_bf[f0:f0 + FF_CHUNK, :], preferred_element_type=F32)
        _store_row_tiles(y_ref, acc + bdn_ref[...])

    @pl.when(i >= n_used)
    def _():
        y_ref[...] = jnp.zeros_like(y_ref)


def _expert_runs(blk_e, n_used):
    n_blocks = blk_e.shape[0]
    idx = jnp.arange(n_blocks, dtype=jnp.int32)
    before = jnp.concatenate([jnp.full((1,), -1, jnp.int32), blk_e[:-1]])
    first = (blk_e != before) & (idx < n_used[0])
    parity = (jnp.cumsum(first.astype(jnp.int32)) - 1) % 2
    starts = jnp.where(first, idx, n_blocks)
    later = jnp.concatenate([lax.cummin(starts, reverse=True)[1:], jnp.full((1,), n_blocks, jnp.int32)])
    next_e = jnp.where(later < n_blocks, blk_e[jnp.minimum(later, n_blocks - 1)], -1)
    return first.astype(jnp.int32), next_e.astype(jnp.int32), parity.astype(jnp.int32)


def _experts(xs, layer, blk_e, n_used, w_gate_up, b_gate_up, w_down, b_down):
    bm = MOE_ROWS
    n_blocks = blk_e.shape[0]
    first, next_e, parity = _expert_runs(blk_e, n_used)
    grid_spec = pltpu.PrefetchScalarGridSpec(
        num_scalar_prefetch=5,
        grid=(n_blocks,),
        in_specs=[
            pl.BlockSpec((bm * ROW_TILE, LANES), lambda i, be, nu, *_: (jnp.minimum(i, nu[0] - 1), 0)),
            pl.BlockSpec(memory_space=pl.ANY),
            pl.BlockSpec((None, None, 1, 2 * D_FF), lambda i, be, *_: (layer, be[i], 0, 0)),
            pl.BlockSpec(memory_space=pl.ANY),
            pl.BlockSpec((None, None, 1, D_MODEL), lambda i, be, *_: (layer, be[i], 0, 0)),
        ],
        out_specs=pl.BlockSpec((bm * ROW_TILE, LANES), lambda i, *_: (i, 0)),
        scratch_shapes=[
            pltpu.VMEM((2, D_MODEL, 2 * D_FF), F32),
            pltpu.VMEM((2, D_FF, D_MODEL), F32),
            pltpu.SemaphoreType.DMA((2,)),
            pltpu.VMEM((D_MODEL, 2 * D_FF), BF16),
            pltpu.VMEM((D_FF, D_MODEL), BF16),
        ],
    )
    depth = w_gate_up.shape[0]
    return pl.pallas_call(
        functools.partial(_expert_body, layer=layer),
        grid_spec=grid_spec,
        out_shape=jax.ShapeDtypeStruct((n_blocks * bm * ROW_TILE, LANES), F32),
        compiler_params=_params(("arbitrary",)),
        name="experts",
    )(blk_e, n_used, first, next_e, parity, xs, w_gate_up, b_gate_up.reshape(depth, N_EXPERTS, 1, 2 * D_FF),
      w_down, b_down.reshape(depth, N_EXPERTS, 1, D_MODEL))


def _combine_body(geom, dest_ref, y_hbm, x1_ref, gate_ref, g2_ref, lng_ref, lnb_ref, o_ctx_ref, o_lat_ref,
                  buf, sem, *, dn_alpha):
    i = pl.program_id(0)
    n_steps = pl.num_programs(0)
    tc = o_ctx_ref.shape[0]

    def issue(step, slot):
        base = step * tc * TOP_K

        def one(r, carry):
            for k in range(TOP_K):
                pltpu.make_async_copy(_row_tile(y_hbm, dest_ref[base + r * TOP_K + k]),
                                      _row_tile(buf.at[slot, k], r), sem.at[slot]).start(priority=k % 2)
            return carry

        lax.fori_loop(0, tc, one, 0, unroll=4)

    @pl.when(i == 0)
    def _():
        issue(0, 0)

    @pl.when(i + 1 < n_steps)
    def _():
        issue(i + 1, (i + 1) % 2)

    slot = i % 2
    for k in range(TOP_K):
        pltpu.make_async_copy(y_hbm.at[pl.ds(0, tc * ROW_TILE), :], buf.at[slot, k], sem.at[slot]).wait()
    gate = gate_ref[...]
    moe = gate[:, 0:1] * _load_row_tiles(buf.at[slot, 0], tc)
    for k in range(1, TOP_K):
        moe = moe + gate[:, k:k + 1] * _load_row_tiles(buf.at[slot, k], tc)
    z = dn_alpha * x1_ref[...] + g2_ref[...] * moe
    out = _layernorm_rows(z, lng_ref[...], lnb_ref[...])
    ctx_steps = geom.tok_ctx // tc

    @pl.when(i < ctx_steps)
    def _():
        o_ctx_ref[...] = out

    @pl.when(i >= ctx_steps)
    def _():
        o_lat_ref[...] = out


def _combine(geom, y_rows, dest, x1, top_gate, mod, ln_g, ln_b, dn_alpha):
    tc = COMB_TILE
    n = geom.n_tok
    out_specs = _token_pair_specs(geom, tc)
    out_shape = [jax.ShapeDtypeStruct((geom.tok_ctx, D_MODEL), F32),
                 jax.ShapeDtypeStruct((geom.tok_lat, D_MODEL), F32)]
    grid_spec = pltpu.PrefetchScalarGridSpec(
        num_scalar_prefetch=1,
        grid=(n // tc,),
        in_specs=[
            pl.BlockSpec(memory_space=pl.ANY),
            pl.BlockSpec((tc, D_MODEL), lambda i, d: (i, 0)),
            pl.BlockSpec((tc, TOP_K), lambda i, d: (i, 0)),
            _mod_spec(geom, tc, 5),
            pl.BlockSpec((1, D_MODEL), lambda i, d: (0, 0)),
            pl.BlockSpec((1, D_MODEL), lambda i, d: (0, 0)),
        ],
        out_specs=out_specs,
        scratch_shapes=[
            pltpu.VMEM((2, TOP_K, tc * ROW_TILE, LANES), F32),
            pltpu.SemaphoreType.DMA((2,)),
        ],
    )
    return pl.pallas_call(
        functools.partial(_combine_body, geom, dn_alpha=dn_alpha),
        grid_spec=grid_spec,
        out_shape=out_shape,
        compiler_params=_params(("arbitrary",), disable_bounds_checks=True),
        name="combine",
    )(dest, y_rows, x1, top_gate, mod, ln_g, ln_b)


IN_SPLITS = (CONV_W, CONV_W, CONV_W, ATT_Q_W, ATT_KV_W, ATT_KV_W, GLA_QK_W, GLA_QK_W, GLA_V_W, GLA_V_W,
             GLA_GATE_RANK, GLA_GATE_RANK, D_MODEL, D_MODEL, D_MODEL)


def _w_in_body(w_ref, o_ref):
    w = w_ref[...]
    parts, o = [], 0
    for width in IN_SPLITS:
        parts.append(w[:, o:o + width])
        o += width
    u_a, b_a, c_a, q_b, k_b, v_b, q_c, k_c, v_c, r_c, za_f, za_b, gt_a, gt_b, gt_c = parts
    pad = jnp.zeros((w.shape[0], LANES - 2 * GLA_GATE_RANK), w.dtype)
    o_ref[...] = jnp.concatenate(
        [gt_a, gt_b, gt_c, u_a, b_a, c_a, q_b, v_c, r_c, q_c, k_c, k_b, v_b, za_f, za_b, pad], axis=1).astype(BF16)


def _rearrange_w_in(w_in):
    depth, d, in_cols = w_in.shape
    rows = 256
    return pl.pallas_call(
        _w_in_body,
        grid=(depth, d // rows),
        in_specs=[pl.BlockSpec((None, rows, in_cols), lambda l, i: (l, i, 0))],
        out_specs=pl.BlockSpec((None, rows, P_COLS), lambda l, i: (l, i, 0)),
        out_shape=jax.ShapeDtypeStruct((depth, d, P_COLS), BF16),
        compiler_params=_params(("arbitrary", "arbitrary")),
        name="w_in_layout",
    )(w_in)


def kernel(x_prompt, x_sample, cache_k, cache_v, state_gla, c, c_ctx, ada_w, ada_b, w_in, conv_w, attn_sink,
           gla_wa2, gla_ba2, gla_norm_g, w_branch, w_out, ln1_g, ln1_b, ln2_g, ln2_b, router_w, router_b,
           w_gate_up, b_gate_up, w_down, b_down):
    depth = w_in.shape[0]
    n_ctx, t_ctx, _ = x_prompt.shape
    n_lat, t_lat, _ = x_sample.shape
    geom = _Geom(n_ctx, t_ctx, n_lat, t_lat)
    dn_alpha = (2 * depth) ** 0.25

    mod_rows = -(-(n_lat + 1) // SUBLANES) * SUBLANES
    cond = jnp.zeros((mod_rows, D_MODEL), F32).at[:n_lat].set(c).at[n_lat].set(c_ctx)
    mods = _modulation(cond, ada_w, ada_b).reshape(depth, mod_rows, 1, 6 * D_MODEL)

    w_in_bf = _rearrange_w_in(w_in)
    w_branch_bf = w_branch.astype(BF16)
    w_out_bf = w_out.astype(BF16)
    conv_w_pad = jnp.zeros((depth, SUBLANES, CONV_W), F32).at[:, :conv_w.shape[1]].set(conv_w)
    rw_hi = router_w.astype(BF16)
    rw_lo = (router_w - rw_hi.astype(F32)).astype(BF16)
    w2 = jnp.zeros((depth, LANES, 2 * GLA_QK_W), F32)
    w2 = w2.at[:, :GLA_GATE_RANK, :GLA_QK_W].set(gla_wa2[:, 0])
    w2 = w2.at[:, GLA_GATE_RANK:2 * GLA_GATE_RANK, GLA_QK_W:].set(gla_wa2[:, 1]).astype(BF16)
    b2 = gla_ba2.reshape(depth, 1, 2 * GLA_QK_W)
    rope_tabs = _rope_tables(t_lat)
    s0t = jnp.transpose(state_gla, (0, 1, 2, 5, 3, 4)).reshape(n_lat, depth, 2, GLA_DV, GLA_QK_W)
    past = cache_k.shape[2]
    ck = cache_k.reshape(n_lat, depth, past, ATT_KV_W)
    cv = cache_v.reshape(n_lat, depth, past, ATT_KV_W)

    x = (x_prompt.reshape(-1, D_MODEL), x_sample.reshape(-1, D_MODEL))
    new_k, new_v, new_s = [], [], []
    for l in range(depth):
        mod = mods[l]
        lw = dict(w_branch=w_branch_bf[l], w_out=w_out_bf[l], conv_w=conv_w_pad[l],
                  ln1_g=ln1_g[l].reshape(1, -1), ln1_b=ln1_b[l].reshape(1, -1),
                  rw_hi=rw_hi[l], rw_lo=rw_lo[l], r_b=router_b[l].reshape(1, -1))
        p = _in_proj(geom, x, mod, w_in_bf, l)
        kv_shape = (n_ctx, t_ctx, ATT_KV_HEADS, HEAD_DIM)
        new_k.append(p[:geom.tok_ctx, COL_KB:COL_KB + ATT_KV_W].astype(F32).reshape(kv_shape))
        new_v.append(p[:geom.tok_ctx, COL_VB:COL_VB + ATT_KV_W].astype(F32).reshape(kv_shape))
        yb = _attn_lat(geom, p, ck[:, l], cv[:, l], rope_tabs, attn_sink[l], _attn_ctx(geom, p, attn_sink[l]))
        g_norm = gla_norm_g[l].reshape(1, GLA_DV)
        yc, s_ctx = _gla(geom, p, False, w2[l], b2[l], g_norm)
        yc, _ = _gla(geom, p, True, w2[l], b2[l], g_norm, s0t[:, l], yc)
        new_s.append(jnp.transpose(s_ctx.reshape(n_ctx, 2, GLA_DV, GLA_HEADS, GLA_DK), (0, 1, 3, 4, 2)))
        x1, h2_tiles, top_idx, top_gate, rank, tile_counts = _merge(geom, x, p, yb, yc, mod, lw, dn_alpha)
        dest, blk_e, n_used, pad_start, pad_n = _route(top_idx, rank, tile_counts, MOE_ROWS, TOK_TILE)
        xs = _dispatch(h2_tiles, dest, pad_start, pad_n, n_used, blk_e.shape[0])
        y_rows = _experts(xs, l, blk_e, n_used, w_gate_up, b_gate_up, w_down, b_down)
        x = _combine(geom, y_rows, dest, x1, top_gate, mod, ln2_g[l].reshape(1, -1), ln2_b[l].reshape(1, -1),
                     dn_alpha)
    y_prompt = x[0].reshape(n_ctx, t_ctx, D_MODEL)
    y_sample = x[1].reshape(n_lat, t_lat, D_MODEL)
    return (y_prompt, y_sample, jnp.stack(new_k, axis=1), jnp.stack(new_v, axis=1), jnp.stack(new_s, axis=1))
```

```python
import functools

import jax
import jax.numpy as jnp
from jax import lax
from jax.experimental import pallas as pl
from jax.experimental.pallas import tpu as pltpu

F32 = jnp.float32
BF16 = jnp.bfloat16
HIGHEST = lax.Precision.HIGHEST

D_MODEL = 1024
MIX_W = D_MODEL // 2
CONV_W = MIX_W
HEAD_DIM = 64
ATT_HEADS = MIX_W // HEAD_DIM
ATT_KV_HEADS = 2
ATT_GROUP = ATT_HEADS // ATT_KV_HEADS
ATT_Q_W = ATT_HEADS * HEAD_DIM
ATT_KV_W = ATT_KV_HEADS * HEAD_DIM
WINDOW = 128
ATT_BLOCK = 128
ATT_SCALE = HEAD_DIM ** -0.5
ROPE_THETA = 10000.0
NEG_INF = -1e30
GRID_W = 64
GLA_HEADS = 4
GLA_DK = MIX_W // 2 // GLA_HEADS
GLA_DV = MIX_W // GLA_HEADS
GLA_QK_W = GLA_HEADS * GLA_DK
GLA_V_W = GLA_HEADS * GLA_DV
GLA_GATE_RANK = 16
GLA_GATE_NORM = 16.0
GLA_CHUNK = 64
N_EXPERTS = 32
TOP_K = 4
D_FF = D_MODEL
SWIGLU_LIMIT = 7.0
SWIGLU_ALPHA = 1.702
LN_EPS = 1e-5

LANES = 128
SUBLANES = 8
BF16_SUBLANES = 16
VMEM_LIMIT_BYTES = 56 * 1024 * 1024

COL_GATES = 0
COL_CONV = 3 * D_MODEL
COL_QB = COL_CONV + 3 * CONV_W
COL_VC = COL_QB + ATT_Q_W
COL_RC = COL_VC + GLA_V_W
COL_QC = COL_RC + GLA_V_W
COL_KC = COL_QC + GLA_QK_W
COL_KB = COL_KC + GLA_QK_W
COL_VB = COL_KB + ATT_KV_W
COL_ZA = COL_VB + ATT_KV_W
P_COLS = COL_ZA + LANES

TOK_TILE = 256
IN_TILE = 512
MOE_ROWS = 512
COMB_TILE = 256
DISPATCH_CHUNK = 2048
PAD_GROUP = 16
FF_CHUNK = 1024

ROW_TILE = D_MODEL // LANES


def _params(semantics, **kw):
    return pltpu.CompilerParams(dimension_semantics=semantics, vmem_limit_bytes=VMEM_LIMIT_BYTES, **kw)


def _sigmoid(x):
    return 1.0 / (1.0 + jnp.exp(-x))


def _layernorm_rows(z, g, b):
    mu = jnp.mean(z, axis=-1, keepdims=True)
    d = z - mu
    var = jnp.mean(d * d, axis=-1, keepdims=True)
    return d * lax.rsqrt(var + LN_EPS) * g + b


def _store_row_tiles(ref, x):
    rows = x.shape[0]
    for j in range(ROW_TILE):
        ref[pl.ds(j, rows, stride=ROW_TILE), :] = x[:, j * LANES:(j + 1) * LANES]


def _load_row_tiles(ref, rows):
    return jnp.concatenate([ref[pl.ds(j, rows, stride=ROW_TILE), :] for j in range(ROW_TILE)], axis=1)


def _ada_body(c_ref, w_ref, b_ref, o_ref):
    c = c_ref[...]
    s = c * _sigmoid(c)
    o_ref[...] = jnp.dot(s, w_ref[...], precision=HIGHEST, preferred_element_type=F32) + b_ref[...]


def _modulation(cond, ada_w, ada_b):
    depth, d, n6 = ada_w.shape
    rows = cond.shape[0]
    tn = 1536
    return pl.pallas_call(
        _ada_body,
        grid=(depth, n6 // tn),
        in_specs=[
            pl.BlockSpec((rows, d), lambda l, j: (0, 0)),
            pl.BlockSpec((None, d, tn), lambda l, j: (l, 0, j)),
            pl.BlockSpec((None, 1, tn), lambda l, j: (l, 0, j)),
        ],
        out_specs=pl.BlockSpec((None, rows, tn), lambda l, j: (l, 0, j)),
        out_shape=jax.ShapeDtypeStruct((depth, rows, n6), F32),
        compiler_params=_params(("arbitrary", "arbitrary")),
        name="modulation",
    )(cond, ada_w, ada_b.reshape(depth, 1, n6))


class _Geom:
    def __init__(self, n_ctx, t_ctx, n_lat, t_lat):
        self.n_ctx, self.t_ctx, self.n_lat, self.t_lat = n_ctx, t_ctx, n_lat, t_lat
        self.tok_ctx = n_ctx * t_ctx
        self.tok_lat = n_lat * t_lat
        self.n_tok = self.tok_ctx + self.tok_lat
        self.ctx_row = n_lat
        assert t_ctx % TOK_TILE == 0 and t_lat % TOK_TILE == 0 and self.tok_ctx % t_lat == 0

    def seq_of_tile(self, i, tile):
        ctx_tiles = self.tok_ctx // tile
        return jnp.where(i < ctx_tiles, self.ctx_row, (i - ctx_tiles) // (self.t_lat // tile))


def _mod_spec(geom, tile, which):
    return pl.BlockSpec((None, 1, D_MODEL), lambda i, *_: (geom.seq_of_tile(i, tile), 0, which))


IN_CHUNK = 1408


def _token_pair_specs(geom, tile):
    ctx_tiles = geom.tok_ctx // tile
    return [pl.BlockSpec((tile, D_MODEL), lambda i, *_: (jnp.minimum(i, ctx_tiles - 1), 0)),
            pl.BlockSpec((tile, D_MODEL), lambda i, *_: (jnp.maximum(i - ctx_tiles, 0), 0))]


def _token_tile(geom, x_ctx_ref, x_lat_ref):
    tile = x_ctx_ref.shape[0]
    is_ctx = pl.program_id(0) < geom.tok_ctx // tile
    return jnp.where(is_ctx, x_ctx_ref[...], x_lat_ref[...])


def _inproj_body(geom, xc_ref, xl_ref, sc_ref, sh_ref, w_ref, o_ref):
    h = (_token_tile(geom, xc_ref, xl_ref) * (1.0 + sc_ref[...]) + sh_ref[...]).astype(BF16)
    for c0 in range(0, P_COLS, IN_CHUNK):
        o_ref[:, c0:c0 + IN_CHUNK] = jnp.dot(h, w_ref[:, c0:c0 + IN_CHUNK],
                                             preferred_element_type=F32).astype(o_ref.dtype)


def _in_proj(geom, x_pair, mod, w_in_bf, layer):
    tm = IN_TILE
    return pl.pallas_call(
        functools.partial(_inproj_body, geom),
        grid=(geom.n_tok // tm,),
        in_specs=_token_pair_specs(geom, tm) + [
            _mod_spec(geom, tm, 1),
            _mod_spec(geom, tm, 0),
            pl.BlockSpec((None, D_MODEL, P_COLS), lambda i: (layer, 0, 0), pipeline_mode=pl.Buffered(1)),
        ],
        out_specs=pl.BlockSpec((tm, P_COLS), lambda i: (i, 0)),
        out_shape=jax.ShapeDtypeStruct((geom.n_tok, P_COLS), BF16),
        compiler_params=_params(("arbitrary",)),
        name="in_proj",
    )(x_pair[0], x_pair[1], mod, mod, w_in_bf)


def _nt_dot(a, b):
    return lax.dot_general(a, b, (((1,), (1,)), ((), ())), preferred_element_type=F32)


def _fold_sublanes(x, op):
    out = x[0:SUBLANES]
    for r0 in range(SUBLANES, x.shape[0], SUBLANES):
        out = op(out, x[r0:r0 + SUBLANES])
    return out


def _split_kv(k_blocks, v_blocks):
    keys, vts = [], []
    for g in range(ATT_KV_HEADS):
        hs = slice(g * HEAD_DIM, (g + 1) * HEAD_DIM)
        keys.append([kb[:, hs].astype(BF16) for kb in k_blocks])
        vts.append([vb[:, hs].astype(F32).T.astype(BF16) for vb in v_blocks])
    return keys, vts


def _attend(q, keys, vts, biases, sink_ref):
    rows = q.shape[0]
    qb = q.astype(BF16)
    pair_outs = []
    for g in range(ATT_KV_HEADS):
        heads = range(g * ATT_GROUP, (g + 1) * ATT_GROUP)
        q_group = jnp.concatenate([qb[:, h * HEAD_DIM:(h + 1) * HEAD_DIM] for h in heads], axis=0)
        sink = jnp.concatenate([jnp.full((SUBLANES, rows), sink_ref[h], F32) for h in heads], axis=1)
        bias_group = [None if b is None else jnp.concatenate([b] * ATT_GROUP, axis=1) for b in biases]
        scores = []
        m_part = sink
        for kb, bias in zip(keys[g], bias_group):
            s = _nt_dot(kb, q_group)
            if bias is not None:
                s = s + bias
            scores.append(s)
            m_part = jnp.maximum(m_part, _fold_sublanes(s, jnp.maximum))
        m = jnp.max(m_part, axis=0, keepdims=True)
        d_part = jnp.zeros_like(m_part)
        acc = jnp.zeros((HEAD_DIM, ATT_GROUP * rows), F32)
        for s, vt in zip(scores, vts[g]):
            p = jnp.exp(s - m)
            d_part = d_part + _fold_sublanes(p, jnp.add)
            acc = acc + jnp.dot(vt, p.astype(BF16), preferred_element_type=F32)
        den = jnp.sum(d_part, axis=0, keepdims=True) + jnp.exp(sink[0:1] - m)
        out_t = acc / den
        for a in range(0, ATT_GROUP, 2):
            pair_outs.append(jnp.concatenate([out_t[:, a * rows:(a + 1) * rows],
                                              out_t[:, (a + 1) * rows:(a + 2) * rows]], axis=0).T)
    return jnp.concatenate(pair_outs, axis=1)


def _attn_ctx_body(q_ref, k_ref, v_ref, sink_ref, o_ref, *, n_seq):
    t = q_ref.shape[0]

    @pl.when(pl.program_id(0) < n_seq)
    def _():
        blk = ATT_BLOCK
        q = q_ref[...].astype(F32) * ATT_SCALE
        k = k_ref[...]
        v = v_ref[...]
        blocks = [slice(r0, r0 + blk) for r0 in range(0, t, blk)]
        keys, vts = _split_kv([k[b] for b in blocks], [v[b] for b in blocks])
        o_ref[...] = _attend(q, keys, vts, [None] * len(blocks), sink_ref)

    @pl.when(pl.program_id(0) >= n_seq)
    def _():
        o_ref[...] = jnp.zeros_like(o_ref)


def _attn_ctx(geom, p, sink):
    t = geom.t_ctx
    last = geom.n_ctx - 1
    return pl.pallas_call(
        functools.partial(_attn_ctx_body, n_seq=geom.n_ctx),
        grid=(geom.n_tok // t,),
        in_specs=[
            pl.BlockSpec((t, ATT_Q_W), lambda b: (jnp.minimum(b, last), COL_QB // ATT_Q_W)),
            pl.BlockSpec((t, ATT_KV_W), lambda b: (jnp.minimum(b, last), COL_KB // ATT_KV_W)),
            pl.BlockSpec((t, ATT_KV_W), lambda b: (jnp.minimum(b, last), COL_VB // ATT_KV_W)),
            pl.BlockSpec(memory_space=pltpu.SMEM),
        ],
        out_specs=pl.BlockSpec((t, ATT_Q_W), lambda b: (b, 0)),
        out_shape=jax.ShapeDtypeStruct((geom.n_tok, ATT_Q_W), F32),
        compiler_params=_params(("arbitrary",)),
        name="attn_ctx",
    )(p, p, p, sink)


def _rope(x, cos, sin_lo, sin_hi):
    nf = HEAD_DIM // 4
    outs = []
    for j in range(x.shape[1] // LANES):
        xj = x[:, j * LANES:(j + 1) * LANES]
        outs.append(xj * cos + pltpu.roll(xj, LANES - nf, 1) * sin_lo + pltpu.roll(xj, nf, 1) * sin_hi)
    return outs[0] if len(outs) == 1 else jnp.concatenate(outs, axis=1)


def _attn_lat_body(q_ref, k_ref, v_ref, ck_ref, cv_ref, qcos_ref, qslo_ref, qshi_ref,
                   kcos_ref, kslo_ref, kshi_ref, sink_ref, yb_hbm, o_ref):
    del yb_hbm
    i = pl.program_id(1)
    nb = k_ref.shape[0] // ATT_BLOCK
    blk = ATT_BLOCK
    q = _rope(q_ref[...].astype(F32), qcos_ref[...], qslo_ref[...], qshi_ref[...]) * ATT_SCALE
    k_in_blk = lax.broadcasted_iota(jnp.int32, (blk, blk), 0)
    q_in_blk = lax.broadcasted_iota(jnp.int32, (blk, blk), 1)
    k_blocks, v_blocks, biases = [], [], []
    for j in (-1, 0, 1):
        kb = i + j
        kbc = jnp.clip(kb, 0, nb - 1)
        rows = pl.ds(pl.multiple_of(kbc * blk, blk), blk)
        k_blocks.append(_rope(k_ref[rows, :].astype(F32), kcos_ref[rows, :], kslo_ref[rows, :], kshi_ref[rows, :]))
        v_blocks.append(v_ref[rows, :])
        outside = jnp.where((kb >= 0) & (kb < nb), 0, 4 * WINDOW)
        dist = (j * blk + k_in_blk) - q_in_blk
        biases.append(jnp.where(jnp.abs(dist) + outside <= WINDOW, 0.0, NEG_INF))
    for r0 in range(0, ck_ref.shape[0], blk):
        k_blocks.append(ck_ref[r0:r0 + blk, :])
        v_blocks.append(cv_ref[r0:r0 + blk, :])
        biases.append(None)
    keys, vts = _split_kv(k_blocks, v_blocks)
    o_ref[...] = _attend(q, keys, vts, biases, sink_ref)


def _attn_lat(geom, p, cache_k, cache_v, rope_tabs, sink, yb):
    t, blk = geom.t_lat, ATT_BLOCK
    nb = t // blk
    past = cache_k.shape[1]
    q_row0 = geom.tok_ctx // blk
    s_row0 = geom.tok_ctx // t
    cos, slo, shi = rope_tabs
    q_tab = pl.BlockSpec((blk, LANES), lambda b, i: (i, 0))
    k_tab = pl.BlockSpec((t, LANES), lambda b, i: (0, 0))
    return pl.pallas_call(
        _attn_lat_body,
        grid=(geom.n_lat, nb),
        in_specs=[
            pl.BlockSpec((blk, ATT_Q_W), lambda b, i: (q_row0 + b * nb + i, COL_QB // ATT_Q_W)),
            pl.BlockSpec((t, ATT_KV_W), lambda b, i: (s_row0 + b, COL_KB // ATT_KV_W)),
            pl.BlockSpec((t, ATT_KV_W), lambda b, i: (s_row0 + b, COL_VB // ATT_KV_W)),
            pl.BlockSpec((None, past, ATT_KV_W), lambda b, i: (b, 0, 0)),
            pl.BlockSpec((None, past, ATT_KV_W), lambda b, i: (b, 0, 0)),
            q_tab, q_tab, q_tab, k_tab, k_tab, k_tab,
            pl.BlockSpec(memory_space=pltpu.SMEM),
            pl.BlockSpec(memory_space=pl.ANY),
        ],
        out_specs=pl.BlockSpec((blk, ATT_Q_W), lambda b, i: (q_row0 + b * nb + i, 0)),
        out_shape=jax.ShapeDtypeStruct((geom.n_tok, ATT_Q_W), F32),
        input_output_aliases={12: 0},
        compiler_params=_params(("arbitrary", "arbitrary")),
        name="attn_lat",
    )(p, p, p, cache_k, cache_v, cos, slo, shi, cos, slo, shi, sink, yb)


def _rope_tables(t_lat):
    half = HEAD_DIM // 2
    nf = half // 2
    tok = jnp.arange(t_lat)
    lane = jnp.arange(LANES) % HEAD_DIM
    pos = jnp.where(lane[None, :] < half, (tok // GRID_W)[:, None], (tok % GRID_W)[:, None]).astype(F32)
    inv = ROPE_THETA ** (-(lane % nf).astype(F32) / nf)
    ang = pos * inv[None, :]
    first = (lane % half) < nf
    cos = jnp.cos(ang)
    sin = jnp.sin(ang)
    sin_lo = jnp.where(first[None, :], -sin, 0.0)
    sin_hi = jnp.where(first[None, :], 0.0, sin)
    return cos, sin_lo, sin_hi


def _gla_body(*refs, has_s0, n_seq):
    y_ref = refs[10 if has_s0 else 8]

    @pl.when(pl.program_id(0) < n_seq)
    def _():
        _gla_sequence(*refs, has_s0=has_s0)

    @pl.when(pl.program_id(0) >= n_seq)
    def _():
        y_ref[...] = jnp.zeros_like(y_ref)


def _gla_sequence(*refs, has_s0):
    if has_s0:
        (q_ref, k_ref, v_ref, r_ref, za_ref, w2_ref, b2_ref, g_ref, s0_ref, _yc_hbm,
         y_ref, sfin_ref, bc_scr, o_scr, qe_scr, upd_scr, st_scr, last_scr) = refs
    else:
        (q_ref, k_ref, v_ref, r_ref, za_ref, w2_ref, b2_ref, g_ref,
         y_ref, sfin_ref, bc_scr, o_scr, qe_scr, upd_scr, st_scr, last_scr) = refs
    t = q_ref.shape[0]
    c, nh = GLA_CHUNK, GLA_HEADS
    n = t // c
    rb = nh * c

    def iota(shape, axis):
        return lax.broadcasted_iota(jnp.int32, shape, axis)

    same_chunk = (iota((rb, rb), 0) // c) == (iota((rb, rb), 1) // c)
    cum = (jnp.where(same_chunk & (iota((rb, rb), 0) >= iota((rb, rb), 1)), 1.0, 0.0).astype(BF16),
           jnp.where(same_chunk & (iota((rb, rb), 0) <= iota((rb, rb), 1)), 1.0, 0.0).astype(BF16))
    for r0 in range(0, t, rb):
        z = jnp.dot(za_ref[r0:r0 + rb, :].astype(BF16), w2_ref[...], preferred_element_type=F32) + b2_ref[...]
        la = (jnp.minimum(z, 0.0) - jnp.log1p(jnp.exp(-jnp.abs(z)))) * (1.0 / GLA_GATE_NORM)
        for d in (0, 1):
            rest = la[:, d * GLA_QK_W:(d + 1) * GLA_QK_W]
            bc = None
            for _ in range(3):
                term = rest.astype(BF16)
                rest = rest - term.astype(F32)
                part = jnp.dot(cum[d], term, preferred_element_type=F32)
                bc = part if bc is None else bc + part
            bc_scr[r0:r0 + rb, d * GLA_QK_W:(d + 1) * GLA_QK_W] = bc

    in_chunk = iota((c, nh * c), 1) % c
    keeps = (iota((c, nh * c), 0) >= in_chunk, iota((c, nh * c), 0) <= in_chunk)
    own_k = (iota((nh * c, GLA_QK_W), 0) // c) == (iota((nh * c, GLA_QK_W), 1) // GLA_DK)
    own_v = (iota((nh * c, GLA_V_W), 0) // c) == (iota((nh * c, GLA_V_W), 1) // GLA_DV)
    own_s = (iota((nh * GLA_DV, GLA_QK_W), 0) // GLA_DV) == (iota((nh * GLA_DV, GLA_QK_W), 1) // GLA_DK)
    lane_head = iota((GLA_DV, GLA_QK_W), 1) // GLA_DK

    def stack(x):
        return jnp.concatenate([x] * nh, axis=0)

    def chunk_rows(ch):
        return pl.ds(pl.multiple_of(ch * c, c), c)

    def local(ch, carry):
        rows = chunk_rows(ch)
        q = q_ref[rows, :].astype(F32) * (GLA_DK ** -0.5)
        k = k_ref[rows, :].astype(F32)
        v = v_ref[rows, :].astype(F32)
        v_bd = jnp.where(own_v, stack(v), 0.0).astype(BF16)
        vt = v.T.astype(BF16)
        o_sum = None
        for d in (0, 1):
            bc = bc_scr[rows, d * GLA_QK_W:(d + 1) * GLA_QK_W]
            last = bc[c - 1:c, :] if d == 0 else bc[0:1, :]
            qe = (q * jnp.exp(bc)).astype(BF16)
            kd = (k * jnp.exp(last - bc)).astype(BF16)
            k_bd = jnp.where(own_k, stack(k * jnp.exp(-bc)), 0.0).astype(BF16)
            qe_scr[d, rows, :] = qe
            last_scr[d, ch] = jnp.broadcast_to(last, (SUBLANES, GLA_QK_W))
            att = jnp.where(keeps[d], _nt_dot(qe, k_bd), 0.0)
            o_d = jnp.dot(att.astype(BF16), v_bd, preferred_element_type=F32)
            u = jnp.dot(vt, kd, preferred_element_type=F32)
            upd = jnp.where(lane_head == 0, u[:GLA_DV], 0.0)
            for h in range(1, nh):
                upd = jnp.where(lane_head == h, u[h * GLA_DV:(h + 1) * GLA_DV], upd)
            upd_scr[d, ch] = upd
            o_sum = o_d if o_sum is None else o_sum + o_d
        o_scr[rows, :] = o_sum
        return carry

    lax.fori_loop(0, n, local, 0, unroll=2)

    for d in (0, 1):
        def scan(step, st, d=d):
            ch = step if d == 0 else n - 1 - step
            st_scr[d, ch] = st
            return st * jnp.exp(last_scr[d, ch][0:1, :]) + upd_scr[d, ch]

        st0 = s0_ref[d] if has_s0 else jnp.zeros((GLA_DV, GLA_QK_W), F32)
        sfin_ref[d] = lax.fori_loop(0, n, scan, st0)

    def cross(ch, carry):
        rows = chunk_rows(ch)
        acc = o_scr[rows, :]
        for d in (0, 1):
            s_bd = jnp.where(own_s, stack(st_scr[d, ch]), 0.0).astype(BF16)
            acc = acc + _nt_dot(qe_scr[d, rows, :], s_bd)
        o_scr[rows, :] = acc
        return carry

    lax.fori_loop(0, n, cross, 0, unroll=4)

    g = g_ref[...]

    def finish(ch, carry):
        rows = chunk_rows(ch)
        of = o_scr[rows, :]
        normed = []
        for h in range(nh):
            oh = of[:, h * GLA_DV:(h + 1) * GLA_DV]
            mu = jnp.mean(oh, axis=-1, keepdims=True)
            dev = oh - mu
            var = jnp.mean(dev * dev, axis=-1, keepdims=True)
            normed.append(dev * lax.rsqrt(var + LN_EPS) * g)
        r = r_ref[rows, :].astype(F32)
        y_ref[rows, :] = jnp.concatenate(normed, axis=1) * (r * _sigmoid(r))
        return carry

    lax.fori_loop(0, n, finish, 0, unroll=2)


def _gla(geom, p, latent, w2, b2, norm_g, s0t=None, yc=None):
    if latent:
        t, n_seq, row0 = geom.t_lat, geom.n_lat, geom.tok_ctx // geom.t_lat
    else:
        t, n_seq, row0 = geom.t_ctx, geom.n_ctx, 0
    has_s0 = s0t is not None
    steps = n_seq if latent else geom.n_tok // t
    last = n_seq - 1

    def col(width, off):
        return pl.BlockSpec((t, width), lambda b: (row0 + jnp.minimum(b, last), off // width))

    in_specs = [
        col(GLA_QK_W, COL_QC), col(GLA_QK_W, COL_KC), col(GLA_V_W, COL_VC), col(GLA_V_W, COL_RC),
        col(LANES, COL_ZA),
        pl.BlockSpec((LANES, 2 * GLA_QK_W), lambda b: (0, 0)),
        pl.BlockSpec((1, 2 * GLA_QK_W), lambda b: (0, 0)),
        pl.BlockSpec((1, GLA_DV), lambda b: (0, 0)),
    ]
    args = [p, p, p, p, p, w2, b2, norm_g]
    if has_s0:
        in_specs += [pl.BlockSpec((None, 2, GLA_DV, GLA_QK_W), lambda b: (b, 0, 0, 0)),
                     pl.BlockSpec(memory_space=pl.ANY)]
        args += [s0t, yc]
    return pl.pallas_call(
        functools.partial(_gla_body, has_s0=has_s0, n_seq=n_seq),
        grid=(steps,),
        in_specs=in_specs,
        out_specs=[
            pl.BlockSpec((t, GLA_V_W), lambda b: (row0 + b, 0)),
            pl.BlockSpec((None, 2, GLA_DV, GLA_QK_W), lambda b: (jnp.minimum(b, last), 0, 0, 0)),
        ],
        out_shape=[
            jax.ShapeDtypeStruct((geom.n_tok, GLA_V_W), F32),
            jax.ShapeDtypeStruct((n_seq, 2, GLA_DV, GLA_QK_W), F32),
        ],
        scratch_shapes=[
            pltpu.VMEM((t, 2 * GLA_QK_W), F32),
            pltpu.VMEM((t, GLA_V_W), F32),
            pltpu.VMEM((2, t, GLA_QK_W), BF16),
            pltpu.VMEM((2, t // GLA_CHUNK, GLA_DV, GLA_QK_W), F32),
            pltpu.VMEM((2, t // GLA_CHUNK, GLA_DV, GLA_QK_W), F32),
            pltpu.VMEM((2, t // GLA_CHUNK, SUBLANES, GLA_QK_W), F32),
        ],
        input_output_aliases={len(args) - 1: 0} if has_s0 else {},
        compiler_params=_params(("arbitrary",)),
        name="gla_lat" if latent else "gla_ctx",
    )(*args)


def _merge_body(geom, xc_ref, xl_ref, gt_ref, cv_ref, cvp_ref, cvn_ref, yb_ref, yc_ref, wbr_ref, wout_ref, cw_ref,
                g1_ref, sc2_ref, sh2_ref, lng_ref, lnb_ref, rwh_ref, rwl_ref, rb_ref,
                x1_ref, h2_ref, tidx_ref, tgate_ref, trank_ref, tcount_ref, *, dn_alpha):
    i = pl.program_id(0)
    tm = xc_ref.shape[0]
    ctx_tiles = geom.tok_ctx // tm
    per_seq = jnp.where(i < ctx_tiles, geom.t_ctx // tm, geom.t_lat // tm)
    j = jnp.where(i < ctx_tiles, i, i - ctx_tiles) % per_seq
    has_prev = (j > 0).astype(F32)
    has_next = (j < per_seq - 1).astype(F32)

    cv = cv_ref[...].astype(F32)
    u, bg, cg = cv[:, :CONV_W], cv[:, CONV_W:2 * CONV_W], cv[:, 2 * CONV_W:]
    v = cg * u
    halo = cvp_ref.shape[0]
    pv = cvp_ref[...].astype(F32)[halo - 1:halo, :]
    nv = cvn_ref[...].astype(F32)[0:1, :]
    v_before = has_prev * (pv[:, 2 * CONV_W:] * pv[:, :CONV_W])
    v_after = has_next * (nv[:, 2 * CONV_W:] * nv[:, :CONV_W])
    row = lax.broadcasted_iota(jnp.int32, (tm, 1), 0)
    v_prev = jnp.where(row == 0, v_before, pltpu.roll(v, 1, 0))
    v_next = jnp.where(row == tm - 1, v_after, pltpu.roll(v, tm - 1, 0))
    cw = cw_ref[...]
    y_a = bg * (cw[0:1, :] * v_prev + cw[1:2, :] * v + cw[2:3, :] * v_next)

    gt = gt_ref[...].astype(F32)
    merged = (_sigmoid(gt[:, :D_MODEL]) * jnp.dot(y_a.astype(BF16), wbr_ref[0], preferred_element_type=F32)
              + _sigmoid(gt[:, D_MODEL:2 * D_MODEL])
              * jnp.dot(yb_ref[...].astype(BF16), wbr_ref[1], preferred_element_type=F32)
              + _sigmoid(gt[:, 2 * D_MODEL:])
              * jnp.dot(yc_ref[...].astype(BF16), wbr_ref[2], preferred_element_type=F32))
    mix = jnp.dot(merged.astype(BF16), wout_ref[...], preferred_element_type=F32)
    x1 = _layernorm_rows(dn_alpha * _token_tile(geom, xc_ref, xl_ref) + g1_ref[...] * mix,
                         lng_ref[...], lnb_ref[...])
    x1_ref[...] = x1
    h2 = x1 * (1.0 + sc2_ref[...]) + sh2_ref[...]
    _store_row_tiles(h2_ref, h2)

    h_hi = h2.astype(BF16)
    h_lo = (h2 - h_hi.astype(F32)).astype(BF16)
    logits = (jnp.dot(h_hi, rwh_ref[...], preferred_element_type=F32)
              + jnp.dot(h_lo, rwh_ref[...], preferred_element_type=F32)
              + jnp.dot(h_hi, rwl_ref[...], preferred_element_type=F32)) + rb_ref[...]
    lane = lax.broadcasted_iota(jnp.int32, logits.shape, 1).astype(F32)
    vals, top_v, top_i = logits, [], []
    for _ in range(TOP_K):
        m = jnp.max(vals, axis=-1, keepdims=True)
        am = jnp.min(jnp.where(vals == m, lane, float(N_EXPERTS)), axis=-1, keepdims=True)
        top_v.append(m)
        top_i.append(am)
        vals = jnp.where(lane == am, -jnp.inf, vals)
    ex = [jnp.exp(tv - top_v[0]) for tv in top_v]
    den = ex[0] + ex[1] + ex[2] + ex[3]
    slot = lax.broadcasted_iota(jnp.int32, (tm, TOP_K), 1)
    gates, idxs = ex[TOP_K - 1], top_i[TOP_K - 1]
    for k in range(TOP_K - 2, -1, -1):
        gates = jnp.where(slot == k, ex[k], gates)
        idxs = jnp.where(slot == k, top_i[k], idxs)
    tgate_ref[...] = gates / den
    tidx_ref[...] = idxs.astype(jnp.int32)

    onehots = [(lane == ti).astype(F32) for ti in top_i]
    member = onehots[0] + onehots[1] + onehots[2] + onehots[3]
    earlier = (lax.broadcasted_iota(jnp.int32, (tm, tm), 0) > lax.broadcasted_iota(jnp.int32, (tm, tm), 1))
    before = jnp.dot(earlier.astype(BF16), member.astype(BF16), preferred_element_type=F32)
    ranks = jnp.sum(onehots[TOP_K - 1] * before, axis=-1, keepdims=True)
    for k in range(TOP_K - 2, -1, -1):
        ranks = jnp.where(slot == k, jnp.sum(onehots[k] * before, axis=-1, keepdims=True), ranks)
    trank_ref[...] = ranks.astype(jnp.int32)
    tcount_ref[...] = (before[tm - 1:tm, :] + member[tm - 1:tm, :]).astype(jnp.int32)


def _merge(geom, x_pair, p, yb, yc, mod, lw, dn_alpha):
    tm = TOK_TILE
    n = geom.n_tok
    halo = BF16_SUBLANES
    halo_per_tile = tm // halo
    n_halo = n // halo
    conv_blk = COL_CONV // (3 * CONV_W)
    const2 = lambda i: (0, 0)
    return pl.pallas_call(
        functools.partial(_merge_body, geom, dn_alpha=dn_alpha),
        grid=(n // tm,),
        in_specs=_token_pair_specs(geom, tm) + [
            pl.BlockSpec((tm, 3 * D_MODEL), lambda i: (i, 0)),
            pl.BlockSpec((tm, 3 * CONV_W), lambda i: (i, conv_blk)),
            pl.BlockSpec((halo, 3 * CONV_W), lambda i: (jnp.maximum(i * halo_per_tile - 1, 0), conv_blk)),
            pl.BlockSpec((halo, 3 * CONV_W),
                         lambda i: (jnp.minimum((i + 1) * halo_per_tile, n_halo - 1), conv_blk)),
            pl.BlockSpec((tm, MIX_W), lambda i: (i, 0)),
            pl.BlockSpec((tm, MIX_W), lambda i: (i, 0)),
            pl.BlockSpec((3, MIX_W, D_MODEL), lambda i: (0, 0, 0)),
            pl.BlockSpec((D_MODEL, D_MODEL), const2),
            pl.BlockSpec((SUBLANES, CONV_W), const2),
            _mod_spec(geom, tm, 2), _mod_spec(geom, tm, 4), _mod_spec(geom, tm, 3),
            pl.BlockSpec((1, D_MODEL), const2), pl.BlockSpec((1, D_MODEL), const2),
            pl.BlockSpec((D_MODEL, N_EXPERTS), const2), pl.BlockSpec((D_MODEL, N_EXPERTS), const2),
            pl.BlockSpec((1, N_EXPERTS), const2),
        ],
        out_specs=[
            pl.BlockSpec((tm, D_MODEL), lambda i: (i, 0)),
            pl.BlockSpec((tm * ROW_TILE, LANES), lambda i: (i, 0)),
            pl.BlockSpec((tm, TOP_K), lambda i: (i, 0)),
            pl.BlockSpec((tm, TOP_K), lambda i: (i, 0)),
            pl.BlockSpec((tm, TOP_K), lambda i: (i, 0)),
            pl.BlockSpec((None, 1, N_EXPERTS), lambda i: (i, 0, 0)),
        ],
        out_shape=[
            jax.ShapeDtypeStruct((n, D_MODEL), F32),
            jax.ShapeDtypeStruct((n * ROW_TILE, LANES), F32),
            jax.ShapeDtypeStruct((n, TOP_K), jnp.int32),
            jax.ShapeDtypeStruct((n, TOP_K), F32),
            jax.ShapeDtypeStruct((n, TOP_K), jnp.int32),
            jax.ShapeDtypeStruct((n // tm, 1, N_EXPERTS), jnp.int32),
        ],
        compiler_params=_params(("arbitrary",)),
        name="merge",
    )(x_pair[0], x_pair[1], p, p, p, p, yb, yc, lw["w_branch"], lw["w_out"], lw["conv_w"], mod, mod, mod,
      lw["ln1_g"], lw["ln1_b"], lw["rw_hi"], lw["rw_lo"], lw["r_b"])


def _route(top_idx, rank, tile_counts, bm, tile):
    n_tok = top_idx.shape[0]
    n_tiles = n_tok // tile
    tcnt = tile_counts.reshape(n_tiles, N_EXPERTS)
    counts = jnp.sum(tcnt, axis=0)
    padded = (counts + bm - 1) // bm * bm
    pend = jnp.cumsum(padded)
    pstart = pend - padded
    tile_base = pstart[None, :] + jnp.cumsum(tcnt, axis=0) - tcnt
    tok_base = jnp.broadcast_to(tile_base[:, None, :], (n_tiles, tile, N_EXPERTS)).reshape(n_tok, 1, N_EXPERTS)
    onehot = top_idx[:, :, None] == jnp.arange(N_EXPERTS, dtype=jnp.int32)[None, None, :]
    dest = (jnp.sum(jnp.where(onehot, tok_base, 0), axis=-1) + rank).reshape(-1).astype(jnp.int32)
    n_blocks = n_tok * TOP_K // bm + N_EXPERTS
    blk_start = jnp.arange(n_blocks, dtype=jnp.int32) * bm
    blk_e = jnp.minimum(jnp.sum((pend[None, :] <= blk_start[:, None]).astype(jnp.int32), axis=1), N_EXPERTS - 1)
    n_used = (pend[-1] // bm).astype(jnp.int32).reshape(1)
    return dest, blk_e.astype(jnp.int32), n_used, (pstart + counts).astype(jnp.int32), (padded - counts).astype(
        jnp.int32)


def _row_tile(ref, row):
    return ref.at[pl.ds(pl.multiple_of(row * ROW_TILE, ROW_TILE), ROW_TILE), :]


def _dispatch_body(dest_ref, pad_start_ref, pad_n_ref, n_used_ref, h_ref, xs_hbm, zbuf, sem, zsem, gsem, bsem,
                   *, chunk, bm):
    i = pl.program_id(0)
    n_blocks = xs_hbm.shape[0] // (bm * ROW_TILE)

    @pl.when(i == 0)
    def _():
        zbuf[...] = jnp.zeros_like(zbuf)
        zrow = zbuf.at[pl.ds(0, ROW_TILE), :]
        zgroup = zbuf.at[pl.ds(0, PAD_GROUP * ROW_TILE), :]

        def group_copy(row):
            return pltpu.make_async_copy(
                zgroup, xs_hbm.at[pl.ds(pl.multiple_of(row * ROW_TILE, ROW_TILE), PAD_GROUP * ROW_TILE), :], gsem)

        def per_expert(e, totals):
            start, cnt = pad_start_ref[e], pad_n_ref[e]
            groups = cnt // PAD_GROUP

            def many(j, carry):
                group_copy(start + j * PAD_GROUP).start()
                return carry

            def one(j, carry):
                pltpu.make_async_copy(zrow, _row_tile(xs_hbm, start + j), zsem).start()
                return carry

            lax.fori_loop(0, groups, many, 0)
            lax.fori_loop(groups * PAD_GROUP, cnt, one, 0)
            return totals[0] + groups, totals[1] + cnt - groups * PAD_GROUP

        total_groups, total = lax.fori_loop(0, N_EXPERTS, per_expert, (0, 0))

        def block_copy(b):
            return pltpu.make_async_copy(zbuf, xs_hbm.at[pl.ds(pl.multiple_of(b * bm * ROW_TILE, bm * ROW_TILE),
                                                                bm * ROW_TILE), :], bsem)

        def start_block(b, carry):
            block_copy(b).start()
            return carry

        def wait_block(b, carry):
            block_copy(b).wait()
            return carry

        lax.fori_loop(n_used_ref[0], n_blocks, start_block, 0)

        def wait_one(j, carry):
            pltpu.make_async_copy(zrow, _row_tile(xs_hbm, 0), zsem).wait()
            return carry

        def wait_group(j, carry):
            group_copy(0).wait()
            return carry

        lax.fori_loop(0, total_groups, wait_group, 0)
        lax.fori_loop(0, total, wait_one, 0)
        lax.fori_loop(n_used_ref[0], n_blocks, wait_block, 0)

    def one_token(r, carry):
        t = i * chunk + r
        for k in range(TOP_K):
            pltpu.make_async_copy(_row_tile(h_ref, r), _row_tile(xs_hbm, dest_ref[t * TOP_K + k]),
                                  sem).start(priority=k % 2)
        return carry

    lax.fori_loop(0, chunk, one_token, 0, unroll=4)
    rows = chunk * TOP_K * ROW_TILE
    pltpu.make_async_copy(xs_hbm.at[pl.ds(0, rows), :], xs_hbm.at[pl.ds(0, rows), :], sem).wait()


def _dispatch(h2_tiles, dest, pad_start, pad_n, n_used, n_blocks):
    chunk, bm = DISPATCH_CHUNK, MOE_ROWS
    n_tok = h2_tiles.shape[0] // ROW_TILE
    grid_spec = pltpu.PrefetchScalarGridSpec(
        num_scalar_prefetch=4,
        grid=(n_tok // chunk,),
        in_specs=[pl.BlockSpec((chunk * ROW_TILE, LANES), lambda i, *_: (i, 0))],
        out_specs=pl.BlockSpec(memory_space=pl.ANY),
        scratch_shapes=[
            pltpu.VMEM((bm * ROW_TILE, LANES), F32),
            pltpu.SemaphoreType.DMA,
            pltpu.SemaphoreType.DMA,
            pltpu.SemaphoreType.DMA,
            pltpu.SemaphoreType.DMA,
        ],
    )
    return pl.pallas_call(
        functools.partial(_dispatch_body, chunk=chunk, bm=bm),
        grid_spec=grid_spec,
        out_shape=jax.ShapeDtypeStruct((n_blocks * bm * ROW_TILE, LANES), F32),
        compiler_params=_params(("arbitrary",), disable_bounds_checks=True),
        name="dispatch",
    )(dest, pad_start, pad_n, n_used, h2_tiles)


def _expert_body(blk_e_ref, n_used_ref, first_ref, next_e_ref, parity_ref, xs_ref, wgu_hbm, bgu_ref, wdn_hbm,
                 bdn_ref, y_ref, wgu_f32, wdn_f32, wsem, wgu_bf, wdn_bf, *, layer):
    i = pl.program_id(0)
    bm = y_ref.shape[0] // ROW_TILE
    n_used = n_used_ref[0]

    def weight_copies(e, s):
        return (pltpu.make_async_copy(wgu_hbm.at[layer, e], wgu_f32.at[s], wsem.at[s]),
                pltpu.make_async_copy(wdn_hbm.at[layer, e], wdn_f32.at[s], wsem.at[s]))

    @pl.when(i == 0)
    def _():
        for cp in weight_copies(blk_e_ref[0], 0):
            cp.start()

    @pl.when(i < n_used)
    def _():
        @pl.when(first_ref[i] == 1)
        def _():
            s = parity_ref[i]
            for cp in weight_copies(blk_e_ref[i], s):
                cp.wait()

            @pl.when(next_e_ref[i] >= 0)
            def _():
                for cp in weight_copies(next_e_ref[i], 1 - s):
                    cp.start()

            wgu_bf[...] = wgu_f32[s].astype(BF16)
            wdn_bf[...] = wdn_f32[s].astype(BF16)

        x = _load_row_tiles(xs_ref, bm).astype(BF16)
        acc = jnp.zeros((bm, D_MODEL), F32)
        for f0 in range(0, D_FF, FF_CHUNK):
            g = jnp.dot(x, wgu_bf[:, f0:f0 + FF_CHUNK], preferred_element_type=F32) + bgu_ref[:, f0:f0 + FF_CHUNK]
            u = (jnp.dot(x, wgu_bf[:, D_FF + f0:D_FF + f0 + FF_CHUNK], preferred_element_type=F32)
                 + bgu_ref[:, D_FF + f0:D_FF + f0 + FF_CHUNK])
            g = jnp.minimum(g, SWIGLU_LIMIT)
            u = jnp.clip(u, -SWIGLU_LIMIT, SWIGLU_LIMIT)
            act = (u + 1.0) * (g * _sigmoid(SWIGLU_ALPHA * g))
            acc = acc + jnp.dot(act.astype(BF16), wdn_bf[f0:f0 + FF_CHUNK, :], preferred_element_type=F32)
        _store_row_tiles(y_ref, acc + bdn_ref[...])

    @pl.when(i >= n_used)
    def _():
        y_ref[...] = jnp.zeros_like(y_ref)


def _expert_runs(blk_e, n_used):
    n_blocks = blk_e.shape[0]
    idx = jnp.arange(n_blocks, dtype=jnp.int32)
    before = jnp.concatenate([jnp.full((1,), -1, jnp.int32), blk_e[:-1]])
    first = (blk_e != before) & (idx < n_used[0])
    parity = (jnp.cumsum(first.astype(jnp.int32)) - 1) % 2
    starts = jnp.where(first, idx, n_blocks)
    later = jnp.concatenate([lax.cummin(starts, reverse=True)[1:], jnp.full((1,), n_blocks, jnp.int32)])
    next_e = jnp.where(later < n_blocks, blk_e[jnp.minimum(later, n_blocks - 1)], -1)
    return first.astype(jnp.int32), next_e.astype(jnp.int32), parity.astype(jnp.int32)


def _experts(xs, layer, blk_e, n_used, w_gate_up, b_gate_up, w_down, b_down):
    bm = MOE_ROWS
    n_blocks = blk_e.shape[0]
    first, next_e, parity = _expert_runs(blk_e, n_used)
    grid_spec = pltpu.PrefetchScalarGridSpec(
        num_scalar_prefetch=5,
        grid=(n_blocks,),
        in_specs=[
            pl.BlockSpec((bm * ROW_TILE, LANES), lambda i, be, nu, *_: (jnp.minimum(i, nu[0] - 1), 0)),
            pl.BlockSpec(memory_space=pl.ANY),
            pl.BlockSpec((None, None, 1, 2 * D_FF), lambda i, be, *_: (layer, be[i], 0, 0)),
            pl.BlockSpec(memory_space=pl.ANY),
            pl.BlockSpec((None, None, 1, D_MODEL), lambda i, be, *_: (layer, be[i], 0, 0)),
        ],
        out_specs=pl.BlockSpec((bm * ROW_TILE, LANES), lambda i, *_: (i, 0)),
        scratch_shapes=[
            pltpu.VMEM((2, D_MODEL, 2 * D_FF), F32),
            pltpu.VMEM((2, D_FF, D_MODEL), F32),
            pltpu.SemaphoreType.DMA((2,)),
            pltpu.VMEM((D_MODEL, 2 * D_FF), BF16),
            pltpu.VMEM((D_FF, D_MODEL), BF16),
        ],
    )
    depth = w_gate_up.shape[0]
    return pl.pallas_call(
        functools.partial(_expert_body, layer=layer),
        grid_spec=grid_spec,
        out_shape=jax.ShapeDtypeStruct((n_blocks * bm * ROW_TILE, LANES), F32),
        compiler_params=_params(("arbitrary",)),
        name="experts",
    )(blk_e, n_used, first, next_e, parity, xs, w_gate_up, b_gate_up.reshape(depth, N_EXPERTS, 1, 2 * D_FF),
      w_down, b_down.reshape(depth, N_EXPERTS, 1, D_MODEL))


def _combine_body(geom, dest_ref, y_hbm, x1_ref, gate_ref, g2_ref, lng_ref, lnb_ref, o_ctx_ref, o_lat_ref,
                  buf, sem, *, dn_alpha):
    i = pl.program_id(0)
    n_steps = pl.num_programs(0)
    tc = o_ctx_ref.shape[0]

    def issue(step, slot):
        base = step * tc * TOP_K

        def one(r, carry):
            for k in range(TOP_K):
                pltpu.make_async_copy(_row_tile(y_hbm, dest_ref[base + r * TOP_K + k]),
                                      _row_tile(buf.at[slot, k], r), sem.at[slot]).start(priority=k % 2)
            return carry

        lax.fori_loop(0, tc, one, 0, unroll=4)

    @pl.when(i == 0)
    def _():
        issue(0, 0)

    @pl.when(i + 1 < n_steps)
    def _():
        issue(i + 1, (i + 1) % 2)

    slot = i % 2
    for k in range(TOP_K):
        pltpu.make_async_copy(y_hbm.at[pl.ds(0, tc * ROW_TILE), :], buf.at[slot, k], sem.at[slot]).wait()
    gate = gate_ref[...]
    moe = gate[:, 0:1] * _load_row_tiles(buf.at[slot, 0], tc)
    for k in range(1, TOP_K):
        moe = moe + gate[:, k:k + 1] * _load_row_tiles(buf.at[slot, k], tc)
    z = dn_alpha * x1_ref[...] + g2_ref[...] * moe
    out = _layernorm_rows(z, lng_ref[...], lnb_ref[...])
    ctx_steps = geom.tok_ctx // tc

    @pl.when(i < ctx_steps)
    def _():
        o_ctx_ref[...] = out

    @pl.when(i >= ctx_steps)
    def _():
        o_lat_ref[...] = out


def _combine(geom, y_rows, dest, x1, top_gate, mod, ln_g, ln_b, dn_alpha):
    tc = COMB_TILE
    n = geom.n_tok
    out_specs = _token_pair_specs(geom, tc)
    out_shape = [jax.ShapeDtypeStruct((geom.tok_ctx, D_MODEL), F32),
                 jax.ShapeDtypeStruct((geom.tok_lat, D_MODEL), F32)]
    grid_spec = pltpu.PrefetchScalarGridSpec(
        num_scalar_prefetch=1,
        grid=(n // tc,),
        in_specs=[
            pl.BlockSpec(memory_space=pl.ANY),
            pl.BlockSpec((tc, D_MODEL), lambda i, d: (i, 0)),
            pl.BlockSpec((tc, TOP_K), lambda i, d: (i, 0)),
            _mod_spec(geom, tc, 5),
            pl.BlockSpec((1, D_MODEL), lambda i, d: (0, 0)),
            pl.BlockSpec((1, D_MODEL), lambda i, d: (0, 0)),
        ],
        out_specs=out_specs,
        scratch_shapes=[
            pltpu.VMEM((2, TOP_K, tc * ROW_TILE, LANES), F32),
            pltpu.SemaphoreType.DMA((2,)),
        ],
    )
    return pl.pallas_call(
        functools.partial(_combine_body, geom, dn_alpha=dn_alpha),
        grid_spec=grid_spec,
        out_shape=out_shape,
        compiler_params=_params(("arbitrary",), disable_bounds_checks=True),
        name="combine",
    )(dest, y_rows, x1, top_gate, mod, ln_g, ln_b)


IN_SPLITS = (CONV_W, CONV_W, CONV_W, ATT_Q_W, ATT_KV_W, ATT_KV_W, GLA_QK_W, GLA_QK_W, GLA_V_W, GLA_V_W,
             GLA_GATE_RANK, GLA_GATE_RANK, D_MODEL, D_MODEL, D_MODEL)


def _w_in_body(w_ref, o_ref):
    w = w_ref[...]
    parts, o = [], 0
    for width in IN_SPLITS:
        parts.append(w[:, o:o + width])
        o += width
    u_a, b_a, c_a, q_b, k_b, v_b, q_c, k_c, v_c, r_c, za_f, za_b, gt_a, gt_b, gt_c = parts
    pad = jnp.zeros((w.shape[0], LANES - 2 * GLA_GATE_RANK), w.dtype)
    o_ref[...] = jnp.concatenate(
        [gt_a, gt_b, gt_c, u_a, b_a, c_a, q_b, v_c, r_c, q_c, k_c, k_b, v_b, za_f, za_b, pad], axis=1).astype(BF16)


def _rearrange_w_in(w_in):
    depth, d, in_cols = w_in.shape
    rows = 256
    return pl.pallas_call(
        _w_in_body,
        grid=(depth, d // rows),
        in_specs=[pl.BlockSpec((None, rows, in_cols), lambda l, i: (l, i, 0))],
        out_specs=pl.BlockSpec((None, rows, P_COLS), lambda l, i: (l, i, 0)),
        out_shape=jax.ShapeDtypeStruct((depth, d, P_COLS), BF16),
        compiler_params=_params(("arbitrary", "arbitrary")),
        name="w_in_layout",
    )(w_in.astype(BF16))


def kernel(x_prompt, x_sample, cache_k, cache_v, state_gla, c, c_ctx, ada_w, ada_b, w_in, conv_w, attn_sink,
           gla_wa2, gla_ba2, gla_norm_g, w_branch, w_out, ln1_g, ln1_b, ln2_g, ln2_b, router_w, router_b,
           w_gate_up, b_gate_up, w_down, b_down):
    depth = w_in.shape[0]
    n_ctx, t_ctx, _ = x_prompt.shape
    n_lat, t_lat, _ = x_sample.shape
    geom = _Geom(n_ctx, t_ctx, n_lat, t_lat)
    dn_alpha = (2 * depth) ** 0.25

    mod_rows = -(-(n_lat + 1) // SUBLANES) * SUBLANES
    cond = jnp.zeros((mod_rows, D_MODEL), F32).at[:n_lat].set(c).at[n_lat].set(c_ctx)
    mods = _modulation(cond, ada_w, ada_b).reshape(depth, mod_rows, 1, 6 * D_MODEL)

    w_in_bf = _rearrange_w_in(w_in)
    w_branch_bf = w_branch.astype(BF16)
    w_out_bf = w_out.astype(BF16)
    conv_w_pad = jnp.zeros((depth, SUBLANES, CONV_W), F32).at[:, :conv_w.shape[1]].set(conv_w)
    rw_hi = router_w.astype(BF16)
    rw_lo = (router_w - rw_hi.astype(F32)).astype(BF16)
    w2 = jnp.zeros((depth, LANES, 2 * GLA_QK_W), F32)
    w2 = w2.at[:, :GLA_GATE_RANK, :GLA_QK_W].set(gla_wa2[:, 0])
    w2 = w2.at[:, GLA_GATE_RANK:2 * GLA_GATE_RANK, GLA_QK_W:].set(gla_wa2[:, 1]).astype(BF16)
    b2 = gla_ba2.reshape(depth, 1, 2 * GLA_QK_W)
    rope_tabs = _rope_tables(t_lat)
    s0t = jnp.transpose(state_gla, (0, 1, 2, 5, 3, 4)).reshape(n_lat, depth, 2, GLA_DV, GLA_QK_W)
    past = cache_k.shape[2]
    ck = cache_k.reshape(n_lat, depth, past, ATT_KV_W)
    cv = cache_v.reshape(n_lat, depth, past, ATT_KV_W)

    x = (x_prompt.reshape(-1, D_MODEL), x_sample.reshape(-1, D_MODEL))
    new_k, new_v, new_s = [], [], []
    for l in range(depth):
        mod = mods[l]
        lw = dict(w_branch=w_branch_bf[l], w_out=w_out_bf[l], conv_w=conv_w_pad[l],
                  ln1_g=ln1_g[l].reshape(1, -1), ln1_b=ln1_b[l].reshape(1, -1),
                  rw_hi=rw_hi[l], rw_lo=rw_lo[l], r_b=router_b[l].reshape(1, -1))
        p = _in_proj(geom, x, mod, w_in_bf, l)
        kv_shape = (n_ctx, t_ctx, ATT_KV_HEADS, HEAD_DIM)
        new_k.append(p[:geom.tok_ctx, COL_KB:COL_KB + ATT_KV_W].astype(F32).reshape(kv_shape))
        new_v.append(p[:geom.tok_ctx, COL_VB:COL_VB + ATT_KV_W].astype(F32).reshape(kv_shape))
        yb = _attn_lat(geom, p, ck[:, l], cv[:, l], rope_tabs, attn_sink[l], _attn_ctx(geom, p, attn_sink[l]))
        g_norm = gla_norm_g[l].reshape(1, GLA_DV)
        yc, s_ctx = _gla(geom, p, False, w2[l], b2[l], g_norm)
        yc, _ = _gla(geom, p, True, w2[l], b2[l], g_norm, s0t[:, l], yc)
        new_s.append(jnp.transpose(s_ctx.reshape(n_ctx, 2, GLA_DV, GLA_HEADS, GLA_DK), (0, 1, 3, 4, 2)))
        x1, h2_tiles, top_idx, top_gate, rank, tile_counts = _merge(geom, x, p, yb, yc, mod, lw, dn_alpha)
        dest, blk_e, n_used, pad_start, pad_n = _route(top_idx, rank, tile_counts, MOE_ROWS, TOK_TILE)
        xs = _dispatch(h2_tiles, dest, pad_start, pad_n, n_used, blk_e.shape[0])
        y_rows = _experts(xs, l, blk_e, n_used, w_gate_up, b_gate_up, w_down, b_down)
        x = _combine(geom, y_rows, dest, x1, top_gate, mod, ln2_g[l].reshape(1, -1), ln2_b[l].reshape(1, -1),
                     dn_alpha)
    y_prompt = x[0].reshape(n_ctx, t_ctx, D_MODEL)
    y_sample = x[1].reshape(n_lat, t_lat, D_MODEL)
    return (y_prompt, y_sample, jnp.stack(new_k, axis=1), jnp.stack(new_v, axis=1), jnp.stack(new_s, axis=1))
```

```python
import functools

import jax
import jax.numpy as jnp
from jax import lax
from jax.experimental import pallas as pl
from jax.experimental.pallas import tpu as pltpu

F32 = jnp.float32
BF16 = jnp.bfloat16
HIGHEST = lax.Precision.HIGHEST

D_MODEL = 1024
MIX_W = D_MODEL // 2
CONV_W = MIX_W
HEAD_DIM = 64
ATT_HEADS = MIX_W // HEAD_DIM
ATT_KV_HEADS = 2
ATT_GROUP = ATT_HEADS // ATT_KV_HEADS
ATT_Q_W = ATT_HEADS * HEAD_DIM
ATT_KV_W = ATT_KV_HEADS * HEAD_DIM
WINDOW = 128
ATT_BLOCK = 128
ATT_SCALE = HEAD_DIM ** -0.5
ROPE_THETA = 10000.0
NEG_INF = -1e30
GRID_W = 64
GLA_HEADS = 4
GLA_DK = MIX_W // 2 // GLA_HEADS
GLA_DV = MIX_W // GLA_HEADS
GLA_QK_W = GLA_HEADS * GLA_DK
GLA_V_W = GLA_HEADS * GLA_DV
GLA_GATE_RANK = 16
GLA_GATE_NORM = 16.0
GLA_CHUNK = 64
N_EXPERTS = 32
TOP_K = 4
D_FF = D_MODEL
SWIGLU_LIMIT = 7.0
SWIGLU_ALPHA = 1.702
LN_EPS = 1e-5

LANES = 128
SUBLANES = 8
BF16_SUBLANES = 16
VMEM_LIMIT_BYTES = 56 * 1024 * 1024

COL_GATES = 0
COL_CONV = 3 * D_MODEL
COL_QB = COL_CONV + 3 * CONV_W
COL_VC = COL_QB + ATT_Q_W
COL_RC = COL_VC + GLA_V_W
COL_QC = COL_RC + GLA_V_W
COL_KC = COL_QC + GLA_QK_W
COL_KB = COL_KC + GLA_QK_W
COL_VB = COL_KB + ATT_KV_W
COL_ZA = COL_VB + ATT_KV_W
P_COLS = COL_ZA + LANES

TOK_TILE = 256
IN_TILE = 512
MOE_ROWS = 512
COMB_TILE = 256
DISPATCH_CHUNK = 2048
PAD_GROUP = 16
FF_CHUNK = 1024

ROW_TILE = D_MODEL // LANES


def _params(semantics, **kw):
    return pltpu.CompilerParams(dimension_semantics=semantics, vmem_limit_bytes=VMEM_LIMIT_BYTES, **kw)


def _sigmoid(x):
    return 1.0 / (1.0 + jnp.exp(-x))


def _layernorm_rows(z, g, b):
    mu = jnp.mean(z, axis=-1, keepdims=True)
    d = z - mu
    var = jnp.mean(d * d, axis=-1, keepdims=True)
    return d * lax.rsqrt(var + LN_EPS) * g + b


def _store_row_tiles(ref, x):
    rows = x.shape[0]
    for j in range(ROW_TILE):
        ref[pl.ds(j, rows, stride=ROW_TILE), :] = x[:, j * LANES:(j + 1) * LANES]


def _load_row_tiles(ref, rows):
    return jnp.concatenate([ref[pl.ds(j, rows, stride=ROW_TILE), :] for j in range(ROW_TILE)], axis=1)


def _ada_body(c_ref, w_ref, b_ref, o_ref):
    c = c_ref[...]
    s = c * _sigmoid(c)
    o_ref[...] = jnp.dot(s, w_ref[...], precision=HIGHEST, preferred_element_type=F32) + b_ref[...]


def _modulation(cond, ada_w, ada_b):
    depth, d, n6 = ada_w.shape
    rows = cond.shape[0]
    tn = 1536
    return pl.pallas_call(
        _ada_body,
        grid=(depth, n6 // tn),
        in_specs=[
            pl.BlockSpec((rows, d), lambda l, j: (0, 0)),
            pl.BlockSpec((None, d, tn), lambda l, j: (l, 0, j)),
            pl.BlockSpec((None, 1, tn), lambda l, j: (l, 0, j)),
        ],
        out_specs=pl.BlockSpec((None, rows, tn), lambda l, j: (l, 0, j)),
        out_shape=jax.ShapeDtypeStruct((depth, rows, n6), F32),
        compiler_params=_params(("arbitrary", "arbitrary")),
        name="modulation",
    )(cond, ada_w, ada_b.reshape(depth, 1, n6))


class _Geom:
    def __init__(self, n_ctx, t_ctx, n_lat, t_lat):
        self.n_ctx, self.t_ctx, self.n_lat, self.t_lat = n_ctx, t_ctx, n_lat, t_lat
        self.tok_ctx = n_ctx * t_ctx
        self.tok_lat = n_lat * t_lat
        self.n_tok = self.tok_ctx + self.tok_lat
        self.ctx_row = n_lat
        assert t_ctx % TOK_TILE == 0 and t_lat % TOK_TILE == 0 and self.tok_ctx % t_lat == 0

    def seq_of_tile(self, i, tile):
        ctx_tiles = self.tok_ctx // tile
        return jnp.where(i < ctx_tiles, self.ctx_row, (i - ctx_tiles) // (self.t_lat // tile))


def _mod_spec(geom, tile, which):
    return pl.BlockSpec((None, 1, D_MODEL), lambda i, *_: (geom.seq_of_tile(i, tile), 0, which))


IN_CHUNK = 1408


def _token_pair_specs(geom, tile):
    ctx_tiles = geom.tok_ctx // tile
    return [pl.BlockSpec((tile, D_MODEL), lambda i, *_: (jnp.minimum(i, ctx_tiles - 1), 0)),
            pl.BlockSpec((tile, D_MODEL), lambda i, *_: (jnp.maximum(i - ctx_tiles, 0), 0))]


def _token_tile(geom, x_ctx_ref, x_lat_ref):
    tile = x_ctx_ref.shape[0]
    is_ctx = pl.program_id(0) < geom.tok_ctx // tile
    return jnp.where(is_ctx, x_ctx_ref[...], x_lat_ref[...])


def _inproj_body(geom, xc_ref, xl_ref, sc_ref, sh_ref, w_ref, o_ref):
    h = (_token_tile(geom, xc_ref, xl_ref) * (1.0 + sc_ref[...]) + sh_ref[...]).astype(BF16)
    for c0 in range(0, P_COLS, IN_CHUNK):
        o_ref[:, c0:c0 + IN_CHUNK] = jnp.dot(h, w_ref[:, c0:c0 + IN_CHUNK],
                                             preferred_element_type=F32).astype(o_ref.dtype)


def _in_proj(geom, x_pair, mod, w_in_bf, layer):
    tm = IN_TILE
    return pl.pallas_call(
        functools.partial(_inproj_body, geom),
        grid=(geom.n_tok // tm,),
        in_specs=_token_pair_specs(geom, tm) + [
            _mod_spec(geom, tm, 1),
            _mod_spec(geom, tm, 0),
            pl.BlockSpec((None, D_MODEL, P_COLS), lambda i: (layer, 0, 0), pipeline_mode=pl.Buffered(1)),
        ],
        out_specs=pl.BlockSpec((tm, P_COLS), lambda i: (i, 0)),
        out_shape=jax.ShapeDtypeStruct((geom.n_tok, P_COLS), BF16),
        compiler_params=_params(("arbitrary",)),
        name="in_proj",
    )(x_pair[0], x_pair[1], mod, mod, w_in_bf)


def _nt_dot(a, b):
    return lax.dot_general(a, b, (((1,), (1,)), ((), ())), preferred_element_type=F32)


def _fold_sublanes(x, op):
    out = x[0:SUBLANES]
    for r0 in range(SUBLANES, x.shape[0], SUBLANES):
        out = op(out, x[r0:r0 + SUBLANES])
    return out


def _split_kv(k_blocks, v_blocks):
    keys, vts = [], []
    for g in range(ATT_KV_HEADS):
        hs = slice(g * HEAD_DIM, (g + 1) * HEAD_DIM)
        keys.append([kb[:, hs].astype(BF16) for kb in k_blocks])
        vts.append([vb[:, hs].astype(F32).T.astype(BF16) for vb in v_blocks])
    return keys, vts


def _attend(q, keys, vts, biases, sink_ref):
    rows = q.shape[0]
    qb = q.astype(BF16)
    pair_outs = []
    for g in range(ATT_KV_HEADS):
        heads = range(g * ATT_GROUP, (g + 1) * ATT_GROUP)
        q_group = jnp.concatenate([qb[:, h * HEAD_DIM:(h + 1) * HEAD_DIM] for h in heads], axis=0)
        sink = jnp.concatenate([jnp.full((SUBLANES, rows), sink_ref[h], F32) for h in heads], axis=1)
        bias_group = [None if b is None else jnp.concatenate([b] * ATT_GROUP, axis=1) for b in biases]
        scores = []
        m_part = sink
        for kb, bias in zip(keys[g], bias_group):
            s = _nt_dot(kb, q_group)
            if bias is not None:
                s = s + bias
            scores.append(s)
            m_part = jnp.maximum(m_part, _fold_sublanes(s, jnp.maximum))
        m = jnp.max(m_part, axis=0, keepdims=True)
        d_part = jnp.zeros_like(m_part)
        acc = jnp.zeros((HEAD_DIM, ATT_GROUP * rows), F32)
        for s, vt in zip(scores, vts[g]):
            p = jnp.exp(s - m)
            d_part = d_part + _fold_sublanes(p, jnp.add)
            acc = acc + jnp.dot(vt, p.astype(BF16), preferred_element_type=F32)
        den = jnp.sum(d_part, axis=0, keepdims=True) + jnp.exp(sink[0:1] - m)
        out_t = acc / den
        for a in range(0, ATT_GROUP, 2):
            pair_outs.append(jnp.concatenate([out_t[:, a * rows:(a + 1) * rows],
                                              out_t[:, (a + 1) * rows:(a + 2) * rows]], axis=0).T)
    return jnp.concatenate(pair_outs, axis=1)


def _attn_ctx_body(q_ref, k_ref, v_ref, sink_ref, o_ref, *, n_seq):
    t = q_ref.shape[0]

    @pl.when(pl.program_id(0) < n_seq)
    def _():
        blk = ATT_BLOCK
        q = q_ref[...].astype(F32) * ATT_SCALE
        k = k_ref[...]
        v = v_ref[...]
        blocks = [slice(r0, r0 + blk) for r0 in range(0, t, blk)]
        keys, vts = _split_kv([k[b] for b in blocks], [v[b] for b in blocks])
        o_ref[...] = _attend(q, keys, vts, [None] * len(blocks), sink_ref)

    @pl.when(pl.program_id(0) >= n_seq)
    def _():
        o_ref[...] = jnp.zeros_like(o_ref)


def _attn_ctx(geom, p, sink):
    t = geom.t_ctx
    last = geom.n_ctx - 1
    return pl.pallas_call(
        functools.partial(_attn_ctx_body, n_seq=geom.n_ctx),
        grid=(geom.n_tok // t,),
        in_specs=[
            pl.BlockSpec((t, ATT_Q_W), lambda b: (jnp.minimum(b, last), COL_QB // ATT_Q_W)),
            pl.BlockSpec((t, ATT_KV_W), lambda b: (jnp.minimum(b, last), COL_KB // ATT_KV_W)),
            pl.BlockSpec((t, ATT_KV_W), lambda b: (jnp.minimum(b, last), COL_VB // ATT_KV_W)),
            pl.BlockSpec(memory_space=pltpu.SMEM),
        ],
        out_specs=pl.BlockSpec((t, ATT_Q_W), lambda b: (b, 0)),
        out_shape=jax.ShapeDtypeStruct((geom.n_tok, ATT_Q_W), F32),
        compiler_params=_params(("arbitrary",)),
        name="attn_ctx",
    )(p, p, p, sink)


def _rope(x, cos, sin_lo, sin_hi):
    nf = HEAD_DIM // 4
    outs = []
    for j in range(x.shape[1] // LANES):
        xj = x[:, j * LANES:(j + 1) * LANES]
        outs.append(xj * cos + pltpu.roll(xj, LANES - nf, 1) * sin_lo + pltpu.roll(xj, nf, 1) * sin_hi)
    return outs[0] if len(outs) == 1 else jnp.concatenate(outs, axis=1)


def _attn_lat_body(q_ref, k_ref, v_ref, ck_ref, cv_ref, qcos_ref, qslo_ref, qshi_ref,
                   kcos_ref, kslo_ref, kshi_ref, sink_ref, yb_hbm, o_ref):
    del yb_hbm
    i = pl.program_id(1)
    nb = k_ref.shape[0] // ATT_BLOCK
    blk = ATT_BLOCK
    q = _rope(q_ref[...].astype(F32), qcos_ref[...], qslo_ref[...], qshi_ref[...]) * ATT_SCALE
    k_in_blk = lax.broadcasted_iota(jnp.int32, (blk, blk), 0)
    q_in_blk = lax.broadcasted_iota(jnp.int32, (blk, blk), 1)
    k_blocks, v_blocks, biases = [], [], []
    for j in (-1, 0, 1):
        kb = i + j
        kbc = jnp.clip(kb, 0, nb - 1)
        rows = pl.ds(pl.multiple_of(kbc * blk, blk), blk)
        k_blocks.append(_rope(k_ref[rows, :].astype(F32), kcos_ref[rows, :], kslo_ref[rows, :], kshi_ref[rows, :]))
        v_blocks.append(v_ref[rows, :])
        outside = jnp.where((kb >= 0) & (kb < nb), 0, 4 * WINDOW)
        dist = (j * blk + k_in_blk) - q_in_blk
        biases.append(jnp.where(jnp.abs(dist) + outside <= WINDOW, 0.0, NEG_INF))
    for r0 in range(0, ck_ref.shape[0], blk):
        k_blocks.append(ck_ref[r0:r0 + blk, :])
        v_blocks.append(cv_ref[r0:r0 + blk, :])
        biases.append(None)
    keys, vts = _split_kv(k_blocks, v_blocks)
    o_ref[...] = _attend(q, keys, vts, biases, sink_ref)


def _attn_lat(geom, p, cache_k, cache_v, rope_tabs, sink, yb):
    t, blk = geom.t_lat, ATT_BLOCK
    nb = t // blk
    past = cache_k.shape[1]
    q_row0 = geom.tok_ctx // blk
    s_row0 = geom.tok_ctx // t
    cos, slo, shi = rope_tabs
    q_tab = pl.BlockSpec((blk, LANES), lambda b, i: (i, 0))
    k_tab = pl.BlockSpec((t, LANES), lambda b, i: (0, 0))
    return pl.pallas_call(
        _attn_lat_body,
        grid=(geom.n_lat, nb),
        in_specs=[
            pl.BlockSpec((blk, ATT_Q_W), lambda b, i: (q_row0 + b * nb + i, COL_QB // ATT_Q_W)),
            pl.BlockSpec((t, ATT_KV_W), lambda b, i: (s_row0 + b, COL_KB // ATT_KV_W)),
            pl.BlockSpec((t, ATT_KV_W), lambda b, i: (s_row0 + b, COL_VB // ATT_KV_W)),
            pl.BlockSpec((None, past, ATT_KV_W), lambda b, i: (b, 0, 0)),
            pl.BlockSpec((None, past, ATT_KV_W), lambda b, i: (b, 0, 0)),
            q_tab, q_tab, q_tab, k_tab, k_tab, k_tab,
            pl.BlockSpec(memory_space=pltpu.SMEM),
            pl.BlockSpec(memory_space=pl.ANY),
        ],
        out_specs=pl.BlockSpec((blk, ATT_Q_W), lambda b, i: (q_row0 + b * nb + i, 0)),
        out_shape=jax.ShapeDtypeStruct((geom.n_tok, ATT_Q_W), F32),
        input_output_aliases={12: 0},
        compiler_params=_params(("arbitrary", "arbitrary")),
        name="attn_lat",
    )(p, p, p, cache_k, cache_v, cos, slo, shi, cos, slo, shi, sink, yb)


def _rope_tables(t_lat):
    half = HEAD_DIM // 2
    nf = half // 2
    tok = jnp.arange(t_lat)
    lane = jnp.arange(LANES) % HEAD_DIM
    pos = jnp.where(lane[None, :] < half, (tok // GRID_W)[:, None], (tok % GRID_W)[:, None]).astype(F32)
    inv = ROPE_THETA ** (-(lane % nf).astype(F32) / nf)
    ang = pos * inv[None, :]
    first = (lane % half) < nf
    cos = jnp.cos(ang)
    sin = jnp.sin(ang)
    sin_lo = jnp.where(first[None, :], -sin, 0.0)
    sin_hi = jnp.where(first[None, :], 0.0, sin)
    return cos, sin_lo, sin_hi


def _gla_body(*refs, has_s0, n_seq):
    y_ref = refs[10 if has_s0 else 8]

    @pl.when(pl.program_id(0) < n_seq)
    def _():
        _gla_sequence(*refs, has_s0=has_s0)

    @pl.when(pl.program_id(0) >= n_seq)
    def _():
        y_ref[...] = jnp.zeros_like(y_ref)


def _gla_sequence(*refs, has_s0):
    if has_s0:
        (q_ref, k_ref, v_ref, r_ref, za_ref, w2_ref, b2_ref, g_ref, s0_ref, _yc_hbm,
         y_ref, sfin_ref, bc_scr, o_scr, qe_scr, upd_scr, st_scr, last_scr) = refs
    else:
        (q_ref, k_ref, v_ref, r_ref, za_ref, w2_ref, b2_ref, g_ref,
         y_ref, sfin_ref, bc_scr, o_scr, qe_scr, upd_scr, st_scr, last_scr) = refs
    t = q_ref.shape[0]
    c, nh = GLA_CHUNK, GLA_HEADS
    n = t // c
    rb = nh * c

    def iota(shape, axis):
        return lax.broadcasted_iota(jnp.int32, shape, axis)

    same_chunk = (iota((rb, rb), 0) // c) == (iota((rb, rb), 1) // c)
    cum = (jnp.where(same_chunk & (iota((rb, rb), 0) >= iota((rb, rb), 1)), 1.0, 0.0).astype(BF16),
           jnp.where(same_chunk & (iota((rb, rb), 0) <= iota((rb, rb), 1)), 1.0, 0.0).astype(BF16))
    for r0 in range(0, t, rb):
        z = jnp.dot(za_ref[r0:r0 + rb, :].astype(BF16), w2_ref[...], preferred_element_type=F32) + b2_ref[...]
        la = (jnp.minimum(z, 0.0) - jnp.log1p(jnp.exp(-jnp.abs(z)))) * (1.0 / GLA_GATE_NORM)
        for d in (0, 1):
            rest = la[:, d * GLA_QK_W:(d + 1) * GLA_QK_W]
            bc = None
            for _ in range(3):
                term = rest.astype(BF16)
                rest = rest - term.astype(F32)
                part = jnp.dot(cum[d], term, preferred_element_type=F32)
                bc = part if bc is None else bc + part
            bc_scr[r0:r0 + rb, d * GLA_QK_W:(d + 1) * GLA_QK_W] = bc

    in_chunk = iota((c, nh * c), 1) % c
    keeps = (iota((c, nh * c), 0) >= in_chunk, iota((c, nh * c), 0) <= in_chunk)
    own_k = (iota((nh * c, GLA_QK_W), 0) // c) == (iota((nh * c, GLA_QK_W), 1) // GLA_DK)
    own_v = (iota((nh * c, GLA_V_W), 0) // c) == (iota((nh * c, GLA_V_W), 1) // GLA_DV)
    own_s = (iota((nh * GLA_DV, GLA_QK_W), 0) // GLA_DV) == (iota((nh * GLA_DV, GLA_QK_W), 1) // GLA_DK)
    lane_head = iota((GLA_DV, GLA_QK_W), 1) // GLA_DK

    def stack(x):
        return jnp.concatenate([x] * nh, axis=0)

    def chunk_rows(ch):
        return pl.ds(pl.multiple_of(ch * c, c), c)

    def local(ch, carry):
        rows = chunk_rows(ch)
        q = q_ref[rows, :].astype(F32) * (GLA_DK ** -0.5)
        k = k_ref[rows, :].astype(F32)
        v = v_ref[rows, :].astype(F32)
        v_bd = jnp.where(own_v, stack(v), 0.0).astype(BF16)
        vt = v.T.astype(BF16)
        o_sum = None
        for d in (0, 1):
            bc = bc_scr[rows, d * GLA_QK_W:(d + 1) * GLA_QK_W]
            last = bc[c - 1:c, :] if d == 0 else bc[0:1, :]
            qe = (q * jnp.exp(bc)).astype(BF16)
            kd = (k * jnp.exp(last - bc)).astype(BF16)
            k_bd = jnp.where(own_k, stack(k * jnp.exp(-bc)), 0.0).astype(BF16)
            qe_scr[d, rows, :] = qe
            last_scr[d, ch] = jnp.broadcast_to(last, (SUBLANES, GLA_QK_W))
            att = jnp.where(keeps[d], _nt_dot(qe, k_bd), 0.0)
            o_d = jnp.dot(att.astype(BF16), v_bd, preferred_element_type=F32)
            u = jnp.dot(vt, kd, preferred_element_type=F32)
            upd = jnp.where(lane_head == 0, u[:GLA_DV], 0.0)
            for h in range(1, nh):
                upd = jnp.where(lane_head == h, u[h * GLA_DV:(h + 1) * GLA_DV], upd)
            upd_scr[d, ch] = upd
            o_sum = o_d if o_sum is None else o_sum + o_d
        o_scr[rows, :] = o_sum
        return carry

    lax.fori_loop(0, n, local, 0, unroll=2)

    for d in (0, 1):
        def scan(step, st, d=d):
            ch = step if d == 0 else n - 1 - step
            st_scr[d, ch] = st
            return st * jnp.exp(last_scr[d, ch][0:1, :]) + upd_scr[d, ch]

        st0 = s0_ref[d] if has_s0 else jnp.zeros((GLA_DV, GLA_QK_W), F32)
        sfin_ref[d] = lax.fori_loop(0, n, scan, st0)

    def cross(ch, carry):
        rows = chunk_rows(ch)
        acc = o_scr[rows, :]
        for d in (0, 1):
            s_bd = jnp.where(own_s, stack(st_scr[d, ch]), 0.0).astype(BF16)
            acc = acc + _nt_dot(qe_scr[d, rows, :], s_bd)
        o_scr[rows, :] = acc
        return carry

    lax.fori_loop(0, n, cross, 0, unroll=2)

    g = g_ref[...]

    def finish(ch, carry):
        rows = chunk_rows(ch)
        of = o_scr[rows, :]
        normed = []
        for h in range(nh):
            oh = of[:, h * GLA_DV:(h + 1) * GLA_DV]
            mu = jnp.mean(oh, axis=-1, keepdims=True)
            dev = oh - mu
            var = jnp.mean(dev * dev, axis=-1, keepdims=True)
            normed.append(dev * lax.rsqrt(var + LN_EPS) * g)
        r = r_ref[rows, :].astype(F32)
        y_ref[rows, :] = jnp.concatenate(normed, axis=1) * (r * _sigmoid(r))
        return carry

    lax.fori_loop(0, n, finish, 0, unroll=2)


def _gla(geom, p, latent, w2, b2, norm_g, s0t=None, yc=None):
    if latent:
        t, n_seq, row0 = geom.t_lat, geom.n_lat, geom.tok_ctx // geom.t_lat
    else:
        t, n_seq, row0 = geom.t_ctx, geom.n_ctx, 0
    has_s0 = s0t is not None
    steps = n_seq if latent else geom.n_tok // t
    last = n_seq - 1

    def col(width, off):
        return pl.BlockSpec((t, width), lambda b: (row0 + jnp.minimum(b, last), off // width))

    in_specs = [
        col(GLA_QK_W, COL_QC), col(GLA_QK_W, COL_KC), col(GLA_V_W, COL_VC), col(GLA_V_W, COL_RC),
        col(LANES, COL_ZA),
        pl.BlockSpec((LANES, 2 * GLA_QK_W), lambda b: (0, 0)),
        pl.BlockSpec((1, 2 * GLA_QK_W), lambda b: (0, 0)),
        pl.BlockSpec((1, GLA_DV), lambda b: (0, 0)),
    ]
    args = [p, p, p, p, p, w2, b2, norm_g]
    if has_s0:
        in_specs += [pl.BlockSpec((None, 2, GLA_DV, GLA_QK_W), lambda b: (b, 0, 0, 0)),
                     pl.BlockSpec(memory_space=pl.ANY)]
        args += [s0t, yc]
    return pl.pallas_call(
        functools.partial(_gla_body, has_s0=has_s0, n_seq=n_seq),
        grid=(steps,),
        in_specs=in_specs,
        out_specs=[
            pl.BlockSpec((t, GLA_V_W), lambda b: (row0 + b, 0)),
            pl.BlockSpec((None, 2, GLA_DV, GLA_QK_W), lambda b: (jnp.minimum(b, last), 0, 0, 0)),
        ],
        out_shape=[
            jax.ShapeDtypeStruct((geom.n_tok, GLA_V_W), F32),
            jax.ShapeDtypeStruct((n_seq, 2, GLA_DV, GLA_QK_W), F32),
        ],
        scratch_shapes=[
            pltpu.VMEM((t, 2 * GLA_QK_W), F32),
            pltpu.VMEM((t, GLA_V_W), F32),
            pltpu.VMEM((2, t, GLA_QK_W), BF16),
            pltpu.VMEM((2, t // GLA_CHUNK, GLA_DV, GLA_QK_W), F32),
            pltpu.VMEM((2, t // GLA_CHUNK, GLA_DV, GLA_QK_W), F32),
            pltpu.VMEM((2, t // GLA_CHUNK, SUBLANES, GLA_QK_W), F32),
        ],
        input_output_aliases={len(args) - 1: 0} if has_s0 else {},
        compiler_params=_params(("arbitrary",)),
        name="gla_lat" if latent else "gla_ctx",
    )(*args)


def _merge_body(geom, xc_ref, xl_ref, gt_ref, cv_ref, cvp_ref, cvn_ref, yb_ref, yc_ref, wbr_ref, wout_ref, cw_ref,
                g1_ref, sc2_ref, sh2_ref, lng_ref, lnb_ref, rwh_ref, rwl_ref, rb_ref,
                x1_ref, h2_ref, tidx_ref, tgate_ref, trank_ref, tcount_ref, *, dn_alpha):
    i = pl.program_id(0)
    tm = xc_ref.shape[0]
    ctx_tiles = geom.tok_ctx // tm
    per_seq = jnp.where(i < ctx_tiles, geom.t_ctx // tm, geom.t_lat // tm)
    j = jnp.where(i < ctx_tiles, i, i - ctx_tiles) % per_seq
    has_prev = (j > 0).astype(F32)
    has_next = (j < per_seq - 1).astype(F32)

    cv = cv_ref[...].astype(F32)
    u, bg, cg = cv[:, :CONV_W], cv[:, CONV_W:2 * CONV_W], cv[:, 2 * CONV_W:]
    v = cg * u
    halo = cvp_ref.shape[0]
    pv = cvp_ref[...].astype(F32)[halo - 1:halo, :]
    nv = cvn_ref[...].astype(F32)[0:1, :]
    v_before = has_prev * (pv[:, 2 * CONV_W:] * pv[:, :CONV_W])
    v_after = has_next * (nv[:, 2 * CONV_W:] * nv[:, :CONV_W])
    row = lax.broadcasted_iota(jnp.int32, (tm, 1), 0)
    v_prev = jnp.where(row == 0, v_before, pltpu.roll(v, 1, 0))
    v_next = jnp.where(row == tm - 1, v_after, pltpu.roll(v, tm - 1, 0))
    cw = cw_ref[...]
    y_a = bg * (cw[0:1, :] * v_prev + cw[1:2, :] * v + cw[2:3, :] * v_next)

    gt = gt_ref[...].astype(F32)
    merged = (_sigmoid(gt[:, :D_MODEL]) * jnp.dot(y_a.astype(BF16), wbr_ref[0], preferred_element_type=F32)
              + _sigmoid(gt[:, D_MODEL:2 * D_MODEL])
              * jnp.dot(yb_ref[...].astype(BF16), wbr_ref[1], preferred_element_type=F32)
              + _sigmoid(gt[:, 2 * D_MODEL:])
              * jnp.dot(yc_ref[...].astype(BF16), wbr_ref[2], preferred_element_type=F32))
    mix = jnp.dot(merged.astype(BF16), wout_ref[...], preferred_element_type=F32)
    x1 = _layernorm_rows(dn_alpha * _token_tile(geom, xc_ref, xl_ref) + g1_ref[...] * mix,
                         lng_ref[...], lnb_ref[...])
    x1_ref[...] = x1
    h2 = x1 * (1.0 + sc2_ref[...]) + sh2_ref[...]
    _store_row_tiles(h2_ref, h2)

    h_hi = h2.astype(BF16)
    h_lo = (h2 - h_hi.astype(F32)).astype(BF16)
    logits = (jnp.dot(h_hi, rwh_ref[...], preferred_element_type=F32)
              + jnp.dot(h_lo, rwh_ref[...], preferred_element_type=F32)
              + jnp.dot(h_hi, rwl_ref[...], preferred_element_type=F32)) + rb_ref[...]
    lane = lax.broadcasted_iota(jnp.int32, logits.shape, 1).astype(F32)
    vals, top_v, top_i = logits, [], []
    for _ in range(TOP_K):
        m = jnp.max(vals, axis=-1, keepdims=True)
        am = jnp.min(jnp.where(vals == m, lane, float(N_EXPERTS)), axis=-1, keepdims=True)
        top_v.append(m)
        top_i.append(am)
        vals = jnp.where(lane == am, -jnp.inf, vals)
    ex = [jnp.exp(tv - top_v[0]) for tv in top_v]
    den = ex[0] + ex[1] + ex[2] + ex[3]
    slot = lax.broadcasted_iota(jnp.int32, (tm, TOP_K), 1)
    gates, idxs = ex[TOP_K - 1], top_i[TOP_K - 1]
    for k in range(TOP_K - 2, -1, -1):
        gates = jnp.where(slot == k, ex[k], gates)
        idxs = jnp.where(slot == k, top_i[k], idxs)
    tgate_ref[...] = gates / den
    tidx_ref[...] = idxs.astype(jnp.int32)

    onehots = [(lane == ti).astype(F32) for ti in top_i]
    member = onehots[0] + onehots[1] + onehots[2] + onehots[3]
    earlier = (lax.broadcasted_iota(jnp.int32, (tm, tm), 0) > lax.broadcasted_iota(jnp.int32, (tm, tm), 1))
    before = jnp.dot(earlier.astype(BF16), member.astype(BF16), preferred_element_type=F32)
    ranks = jnp.sum(onehots[TOP_K - 1] * before, axis=-1, keepdims=True)
    for k in range(TOP_K - 2, -1, -1):
        ranks = jnp.where(slot == k, jnp.sum(onehots[k] * before, axis=-1, keepdims=True), ranks)
    trank_ref[...] = ranks.astype(jnp.int32)
    tcount_ref[...] = (before[tm - 1:tm, :] + member[tm - 1:tm, :]).astype(jnp.int32)


def _merge(geom, x_pair, p, yb, yc, mod, lw, dn_alpha):
    tm = TOK_TILE
    n = geom.n_tok
    halo = BF16_SUBLANES
    halo_per_tile = tm // halo
    n_halo = n // halo
    conv_blk = COL_CONV // (3 * CONV_W)
    const2 = lambda i: (0, 0)
    return pl.pallas_call(
        functools.partial(_merge_body, geom, dn_alpha=dn_alpha),
        grid=(n // tm,),
        in_specs=_token_pair_specs(geom, tm) + [
            pl.BlockSpec((tm, 3 * D_MODEL), lambda i: (i, 0)),
            pl.BlockSpec((tm, 3 * CONV_W), lambda i: (i, conv_blk)),
            pl.BlockSpec((halo, 3 * CONV_W), lambda i: (jnp.maximum(i * halo_per_tile - 1, 0), conv_blk)),
            pl.BlockSpec((halo, 3 * CONV_W),
                         lambda i: (jnp.minimum((i + 1) * halo_per_tile, n_halo - 1), conv_blk)),
            pl.BlockSpec((tm, MIX_W), lambda i: (i, 0)),
            pl.BlockSpec((tm, MIX_W), lambda i: (i, 0)),
            pl.BlockSpec((3, MIX_W, D_MODEL), lambda i: (0, 0, 0)),
            pl.BlockSpec((D_MODEL, D_MODEL), const2),
            pl.BlockSpec((SUBLANES, CONV_W), const2),
            _mod_spec(geom, tm, 2), _mod_spec(geom, tm, 4), _mod_spec(geom, tm, 3),
            pl.BlockSpec((1, D_MODEL), const2), pl.BlockSpec((1, D_MODEL), const2),
            pl.BlockSpec((D_MODEL, N_EXPERTS), const2), pl.BlockSpec((D_MODEL, N_EXPERTS), const2),
            pl.BlockSpec((1, N_EXPERTS), const2),
        ],
        out_specs=[
            pl.BlockSpec((tm, D_MODEL), lambda i: (i, 0)),
            pl.BlockSpec((tm * ROW_TILE, LANES), lambda i: (i, 0)),
            pl.BlockSpec((tm, TOP_K), lambda i: (i, 0)),
            pl.BlockSpec((tm, TOP_K), lambda i: (i, 0)),
            pl.BlockSpec((tm, TOP_K), lambda i: (i, 0)),
            pl.BlockSpec((None, 1, N_EXPERTS), lambda i: (i, 0, 0)),
        ],
        out_shape=[
            jax.ShapeDtypeStruct((n, D_MODEL), F32),
            jax.ShapeDtypeStruct((n * ROW_TILE, LANES), F32),
            jax.ShapeDtypeStruct((n, TOP_K), jnp.int32),
            jax.ShapeDtypeStruct((n, TOP_K), F32),
            jax.ShapeDtypeStruct((n, TOP_K), jnp.int32),
            jax.ShapeDtypeStruct((n // tm, 1, N_EXPERTS), jnp.int32),
        ],
        compiler_params=_params(("arbitrary",)),
        name="merge",
    )(x_pair[0], x_pair[1], p, p, p, p, yb, yc, lw["w_branch"], lw["w_out"], lw["conv_w"], mod, mod, mod,
      lw["ln1_g"], lw["ln1_b"], lw["rw_hi"], lw["rw_lo"], lw["r_b"])


def _route(top_idx, rank, tile_counts, bm, tile):
    n_tok = top_idx.shape[0]
    n_tiles = n_tok // tile
    tcnt = tile_counts.reshape(n_tiles, N_EXPERTS)
    counts = jnp.sum(tcnt, axis=0)
    padded = (counts + bm - 1) // bm * bm
    pend = jnp.cumsum(padded)
    pstart = pend - padded
    tile_base = pstart[None, :] + jnp.cumsum(tcnt, axis=0) - tcnt
    tok_base = jnp.broadcast_to(tile_base[:, None, :], (n_tiles, tile, N_EXPERTS)).reshape(n_tok, 1, N_EXPERTS)
    onehot = top_idx[:, :, None] == jnp.arange(N_EXPERTS, dtype=jnp.int32)[None, None, :]
    dest = (jnp.sum(jnp.where(onehot, tok_base, 0), axis=-1) + rank).reshape(-1).astype(jnp.int32)
    n_blocks = n_tok * TOP_K // bm + N_EXPERTS
    blk_start = jnp.arange(n_blocks, dtype=jnp.int32) * bm
    blk_e = jnp.minimum(jnp.sum((pend[None, :] <= blk_start[:, None]).astype(jnp.int32), axis=1), N_EXPERTS - 1)
    n_used = (pend[-1] // bm).astype(jnp.int32).reshape(1)
    return dest, blk_e.astype(jnp.int32), n_used, (pstart + counts).astype(jnp.int32), (padded - counts).astype(
        jnp.int32)


def _row_tile(ref, row):
    return ref.at[pl.ds(pl.multiple_of(row * ROW_TILE, ROW_TILE), ROW_TILE), :]


def _dispatch_body(dest_ref, pad_start_ref, pad_n_ref, n_used_ref, h_ref, xs_hbm, zbuf, sem, zsem, gsem, bsem,
                   *, chunk, bm):
    i = pl.program_id(0)
    n_blocks = xs_hbm.shape[0] // (bm * ROW_TILE)

    @pl.when(i == 0)
    def _():
        zbuf[...] = jnp.zeros_like(zbuf)
        zrow = zbuf.at[pl.ds(0, ROW_TILE), :]
        zgroup = zbuf.at[pl.ds(0, PAD_GROUP * ROW_TILE), :]

        def group_copy(row):
            return pltpu.make_async_copy(
                zgroup, xs_hbm.at[pl.ds(pl.multiple_of(row * ROW_TILE, ROW_TILE), PAD_GROUP * ROW_TILE), :], gsem)

        def per_expert(e, totals):
            start, cnt = pad_start_ref[e], pad_n_ref[e]
            groups = cnt // PAD_GROUP

            def many(j, carry):
                group_copy(start + j * PAD_GROUP).start()
                return carry

            def one(j, carry):
                pltpu.make_async_copy(zrow, _row_tile(xs_hbm, start + j), zsem).start()
                return carry

            lax.fori_loop(0, groups, many, 0)
            lax.fori_loop(groups * PAD_GROUP, cnt, one, 0)
            return totals[0] + groups, totals[1] + cnt - groups * PAD_GROUP

        total_groups, total = lax.fori_loop(0, N_EXPERTS, per_expert, (0, 0))

        def block_copy(b):
            return pltpu.make_async_copy(zbuf, xs_hbm.at[pl.ds(pl.multiple_of(b * bm * ROW_TILE, bm * ROW_TILE),
                                                                bm * ROW_TILE), :], bsem)

        def start_block(b, carry):
            block_copy(b).start()
            return carry

        lax.fori_loop(n_used_ref[0], n_blocks, start_block, 0)
        del total_groups, total

    def one_token(r, carry):
        t = i * chunk + r
        for k in range(TOP_K):
            pltpu.make_async_copy(_row_tile(h_ref, r), _row_tile(xs_hbm, dest_ref[t * TOP_K + k]),
                                  sem).start(priority=k % 2)
        return carry

    lax.fori_loop(0, chunk, one_token, 0, unroll=4)
    rows = chunk * TOP_K * ROW_TILE
    pltpu.make_async_copy(xs_hbm.at[pl.ds(0, rows), :], xs_hbm.at[pl.ds(0, rows), :], sem).wait()

    @pl.when(i == 0)
    def _():
        def count(e, totals):
            cnt = pad_n_ref[e]
            return totals[0] + cnt // PAD_GROUP, totals[1] + cnt % PAD_GROUP

        total_groups, total = lax.fori_loop(0, N_EXPERTS, count, (0, 0))

        def wait_group(j, carry):
            pltpu.make_async_copy(zbuf.at[pl.ds(0, PAD_GROUP * ROW_TILE), :],
                                  xs_hbm.at[pl.ds(0, PAD_GROUP * ROW_TILE), :], gsem).wait()
            return carry

        def wait_one(j, carry):
            pltpu.make_async_copy(zbuf.at[pl.ds(0, ROW_TILE), :], _row_tile(xs_hbm, 0), zsem).wait()
            return carry

        def wait_block(b, carry):
            pltpu.make_async_copy(zbuf, xs_hbm.at[pl.ds(0, bm * ROW_TILE), :], bsem).wait()
            return carry

        lax.fori_loop(0, total_groups, wait_group, 0)
        lax.fori_loop(0, total, wait_one, 0)
        lax.fori_loop(n_used_ref[0], n_blocks, wait_block, 0)


def _dispatch(h2_tiles, dest, pad_start, pad_n, n_used, n_blocks):
    chunk, bm = DISPATCH_CHUNK, MOE_ROWS
    n_tok = h2_tiles.shape[0] // ROW_TILE
    grid_spec = pltpu.PrefetchScalarGridSpec(
        num_scalar_prefetch=4,
        grid=(n_tok // chunk,),
        in_specs=[pl.BlockSpec((chunk * ROW_TILE, LANES), lambda i, *_: (i, 0))],
        out_specs=pl.BlockSpec(memory_space=pl.ANY),
        scratch_shapes=[
            pltpu.VMEM((bm * ROW_TILE, LANES), F32),
            pltpu.SemaphoreType.DMA,
            pltpu.SemaphoreType.DMA,
            pltpu.SemaphoreType.DMA,
            pltpu.SemaphoreType.DMA,
        ],
    )
    return pl.pallas_call(
        functools.partial(_dispatch_body, chunk=chunk, bm=bm),
        grid_spec=grid_spec,
        out_shape=jax.ShapeDtypeStruct((n_blocks * bm * ROW_TILE, LANES), F32),
        compiler_params=_params(("arbitrary",), disable_bounds_checks=True),
        name="dispatch",
    )(dest, pad_start, pad_n, n_used, h2_tiles)


def _expert_body(blk_e_ref, n_used_ref, first_ref, next_e_ref, parity_ref, xs_ref, wgu_hbm, bgu_ref, wdn_hbm,
                 bdn_ref, y_ref, wgu_f32, wdn_f32, wsem, wgu_bf, wdn_bf, *, layer):
    i = pl.program_id(0)
    bm = y_ref.shape[0] // ROW_TILE
    n_used = n_used_ref[0]

    def weight_copies(e, s):
        return (pltpu.make_async_copy(wgu_hbm.at[layer, e], wgu_f32.at[s], wsem.at[s]),
                pltpu.make_async_copy(wdn_hbm.at[layer, e], wdn_f32.at[s], wsem.at[s]))

    @pl.when(i == 0)
    def _():
        for cp in weight_copies(blk_e_ref[0], 0):
            cp.start()

    @pl.when(i < n_used)
    def _():
        @pl.when(first_ref[i] == 1)
        def _():
            s = parity_ref[i]
            for cp in weight_copies(blk_e_ref[i], s):
                cp.wait()

            @pl.when(next_e_ref[i] >= 0)
            def _():
                for cp in weight_copies(next_e_ref[i], 1 - s):
                    cp.start()

            wgu_bf[...] = wgu_f32[s].astype(BF16)
            wdn_bf[...] = wdn_f32[s].astype(BF16)

        x = _load_row_tiles(xs_ref, bm).astype(BF16)
        acc = jnp.zeros((bm, D_MODEL), F32)
        for f0 in range(0, D_FF, FF_CHUNK):
            g = jnp.dot(x, wgu_bf[:, f0:f0 + FF_CHUNK], preferred_element_type=F32) + bgu_ref[:, f0:f0 + FF_CHUNK]
            u = (jnp.dot(x, wgu_bf[:, D_FF + f0:D_FF + f0 + FF_CHUNK], preferred_element_type=F32)
                 + bgu_ref[:, D_FF + f0:D_FF + f0 + FF_CHUNK])
            g = jnp.minimum(g, SWIGLU_LIMIT)
            u = jnp.clip(u, -SWIGLU_LIMIT, SWIGLU_LIMIT)
            act = (u + 1.0) * (g * _sigmoid(SWIGLU_ALPHA * g))
            acc = acc + jnp.dot(act.astype(BF16), wdn_bf[f0:f0 + FF_CHUNK, :], preferred_element_type=F32)
        _store_row_tiles(y_ref, acc + bdn_ref[...])

    @pl.when(i >= n_used)
    def _():
        y_ref[...] = jnp.zeros_like(y_ref)


def _expert_runs(blk_e, n_used):
    n_blocks = blk_e.shape[0]
    idx = jnp.arange(n_blocks, dtype=jnp.int32)
    before = jnp.concatenate([jnp.full((1,), -1, jnp.int32), blk_e[:-1]])
    first = (blk_e != before) & (idx < n_used[0])
    parity = (jnp.cumsum(first.astype(jnp.int32)) - 1) % 2
    starts = jnp.where(first, idx, n_blocks)
    later = jnp.concatenate([lax.cummin(starts, reverse=True)[1:], jnp.full((1,), n_blocks, jnp.int32)])
    next_e = jnp.where(later < n_blocks, blk_e[jnp.minimum(later, n_blocks - 1)], -1)
    return first.astype(jnp.int32), next_e.astype(jnp.int32), parity.astype(jnp.int32)


def _experts(xs, layer, blk_e, n_used, w_gate_up, b_gate_up, w_down, b_down):
    bm = MOE_ROWS
    n_blocks = blk_e.shape[0]
    first, next_e, parity = _expert_runs(blk_e, n_used)
    grid_spec = pltpu.PrefetchScalarGridSpec(
        num_scalar_prefetch=5,
        grid=(n_blocks,),
        in_specs=[
            pl.BlockSpec((bm * ROW_TILE, LANES), lambda i, be, nu, *_: (jnp.minimum(i, nu[0] - 1), 0)),
            pl.BlockSpec(memory_space=pl.ANY),
            pl.BlockSpec((None, None, 1, 2 * D_FF), lambda i, be, *_: (layer, be[i], 0, 0)),
            pl.BlockSpec(memory_space=pl.ANY),
            pl.BlockSpec((None, None, 1, D_MODEL), lambda i, be, *_: (layer, be[i], 0, 0)),
        ],
        out_specs=pl.BlockSpec((bm * ROW_TILE, LANES), lambda i, *_: (i, 0)),
        scratch_shapes=[
            pltpu.VMEM((2, D_MODEL, 2 * D_FF), F32),
            pltpu.VMEM((2, D_FF, D_MODEL), F32),
            pltpu.SemaphoreType.DMA((2,)),
            pltpu.VMEM((D_MODEL, 2 * D_FF), BF16),
            pltpu.VMEM((D_FF, D_MODEL), BF16),
        ],
    )
    depth = w_gate_up.shape[0]
    return pl.pallas_call(
        functools.partial(_expert_body, layer=layer),
        grid_spec=grid_spec,
        out_shape=jax.ShapeDtypeStruct((n_blocks * bm * ROW_TILE, LANES), F32),
        compiler_params=_params(("arbitrary",)),
        name="experts",
    )(blk_e, n_used, first, next_e, parity, xs, w_gate_up, b_gate_up.reshape(depth, N_EXPERTS, 1, 2 * D_FF),
      w_down, b_down.reshape(depth, N_EXPERTS, 1, D_MODEL))


def _combine_body(geom, dest_ref, y_hbm, x1_ref, gate_ref, g2_ref, lng_ref, lnb_ref, o_ctx_ref, o_lat_ref,
                  buf, sem, *, dn_alpha):
    i = pl.program_id(0)
    n_steps = pl.num_programs(0)
    tc = o_ctx_ref.shape[0]

    def issue(step, slot):
        base = step * tc * TOP_K

        def one(r, carry):
            for k in range(TOP_K):
                pltpu.make_async_copy(_row_tile(y_hbm, dest_ref[base + r * TOP_K + k]),
                                      _row_tile(buf.at[slot, k], r), sem.at[slot]).start(priority=k % 2)
            return carry

        lax.fori_loop(0, tc, one, 0, unroll=4)

    @pl.when(i == 0)
    def _():
        issue(0, 0)

    @pl.when(i + 1 < n_steps)
    def _():
        issue(i + 1, (i + 1) % 2)

    slot = i % 2
    for k in range(TOP_K):
        pltpu.make_async_copy(y_hbm.at[pl.ds(0, tc * ROW_TILE), :], buf.at[slot, k], sem.at[slot]).wait()
    gate = gate_ref[...]
    moe = gate[:, 0:1] * _load_row_tiles(buf.at[slot, 0], tc)
    for k in range(1, TOP_K):
        moe = moe + gate[:, k:k + 1] * _load_row_tiles(buf.at[slot, k], tc)
    z = dn_alpha * x1_ref[...] + g2_ref[...] * moe
    out = _layernorm_rows(z, lng_ref[...], lnb_ref[...])
    ctx_steps = geom.tok_ctx // tc

    @pl.when(i < ctx_steps)
    def _():
        o_ctx_ref[...] = out

    @pl.when(i >= ctx_steps)
    def _():
        o_lat_ref[...] = out


def _combine(geom, y_rows, dest, x1, top_gate, mod, ln_g, ln_b, dn_alpha):
    tc = COMB_TILE
    n = geom.n_tok
    out_specs = _token_pair_specs(geom, tc)
    out_shape = [jax.ShapeDtypeStruct((geom.tok_ctx, D_MODEL), F32),
                 jax.ShapeDtypeStruct((geom.tok_lat, D_MODEL), F32)]
    grid_spec = pltpu.PrefetchScalarGridSpec(
        num_scalar_prefetch=1,
        grid=(n // tc,),
        in_specs=[
            pl.BlockSpec(memory_space=pl.ANY),
            pl.BlockSpec((tc, D_MODEL), lambda i, d: (i, 0)),
            pl.BlockSpec((tc, TOP_K), lambda i, d: (i, 0)),
            _mod_spec(geom, tc, 5),
            pl.BlockSpec((1, D_MODEL), lambda i, d: (0, 0)),
            pl.BlockSpec((1, D_MODEL), lambda i, d: (0, 0)),
        ],
        out_specs=out_specs,
        scratch_shapes=[
            pltpu.VMEM((2, TOP_K, tc * ROW_TILE, LANES), F32),
            pltpu.SemaphoreType.DMA((2,)),
        ],
    )
    return pl.pallas_call(
        functools.partial(_combine_body, geom, dn_alpha=dn_alpha),
        grid_spec=grid_spec,
        out_shape=out_shape,
        compiler_params=_params(("arbitrary",), disable_bounds_checks=True),
        name="combine",
    )(dest, y_rows, x1, top_gate, mod, ln_g, ln_b)


IN_SPLITS = (CONV_W, CONV_W, CONV_W, ATT_Q_W, ATT_KV_W, ATT_KV_W, GLA_QK_W, GLA_QK_W, GLA_V_W, GLA_V_W,
             GLA_GATE_RANK, GLA_GATE_RANK, D_MODEL, D_MODEL, D_MODEL)


def _w_in_body(w_ref, o_ref):
    w = w_ref[...]
    parts, o = [], 0
    for width in IN_SPLITS:
        parts.append(w[:, o:o + width])
        o += width
    u_a, b_a, c_a, q_b, k_b, v_b, q_c, k_c, v_c, r_c, za_f, za_b, gt_a, gt_b, gt_c = parts
    pad = jnp.zeros((w.shape[0], LANES - 2 * GLA_GATE_RANK), w.dtype)
    o_ref[...] = jnp.concatenate(
        [gt_a, gt_b, gt_c, u_a, b_a, c_a, q_b, v_c, r_c, q_c, k_c, k_b, v_b, za_f, za_b, pad], axis=1).astype(BF16)


def _rearrange_w_in(w_in):
    depth, d, in_cols = w_in.shape
    rows = 256
    return pl.pallas_call(
        _w_in_body,
        grid=(depth, d // rows),
        in_specs=[pl.BlockSpec((None, rows, in_cols), lambda l, i: (l, i, 0))],
        out_specs=pl.BlockSpec((None, rows, P_COLS), lambda l, i: (l, i, 0)),
        out_shape=jax.ShapeDtypeStruct((depth, d, P_COLS), BF16),
        compiler_params=_params(("arbitrary", "arbitrary")),
        name="w_in_layout",
    )(w_in)


def kernel(x_prompt, x_sample, cache_k, cache_v, state_gla, c, c_ctx, ada_w, ada_b, w_in, conv_w, attn_sink,
           gla_wa2, gla_ba2, gla_norm_g, w_branch, w_out, ln1_g, ln1_b, ln2_g, ln2_b, router_w, router_b,
           w_gate_up, b_gate_up, w_down, b_down):
    depth = w_in.shape[0]
    n_ctx, t_ctx, _ = x_prompt.shape
    n_lat, t_lat, _ = x_sample.shape
    geom = _Geom(n_ctx, t_ctx, n_lat, t_lat)
    dn_alpha = (2 * depth) ** 0.25

    mod_rows = -(-(n_lat + 1) // SUBLANES) * SUBLANES
    cond = jnp.zeros((mod_rows, D_MODEL), F32).at[:n_lat].set(c).at[n_lat].set(c_ctx)
    mods = _modulation(cond, ada_w, ada_b).reshape(depth, mod_rows, 1, 6 * D_MODEL)

    w_in_bf = _rearrange_w_in(w_in)
    w_branch_bf = w_branch.astype(BF16)
    w_out_bf = w_out.astype(BF16)
    conv_w_pad = jnp.zeros((depth, SUBLANES, CONV_W), F32).at[:, :conv_w.shape[1]].set(conv_w)
    rw_hi = router_w.astype(BF16)
    rw_lo = (router_w - rw_hi.astype(F32)).astype(BF16)
    w2 = jnp.zeros((depth, LANES, 2 * GLA_QK_W), F32)
    w2 = w2.at[:, :GLA_GATE_RANK, :GLA_QK_W].set(gla_wa2[:, 0])
    w2 = w2.at[:, GLA_GATE_RANK:2 * GLA_GATE_RANK, GLA_QK_W:].set(gla_wa2[:, 1]).astype(BF16)
    b2 = gla_ba2.reshape(depth, 1, 2 * GLA_QK_W)
    rope_tabs = _rope_tables(t_lat)
    s0t = jnp.transpose(state_gla, (0, 1, 2, 5, 3, 4)).reshape(n_lat, depth, 2, GLA_DV, GLA_QK_W)
    past = cache_k.shape[2]
    ck = cache_k.reshape(n_lat, depth, past, ATT_KV_W)
    cv = cache_v.reshape(n_lat, depth, past, ATT_KV_W)

    x = (x_prompt.reshape(-1, D_MODEL), x_sample.reshape(-1, D_MODEL))
    new_k, new_v, new_s = [], [], []
    for l in range(depth):
        mod = mods[l]
        lw = dict(w_branch=w_branch_bf[l], w_out=w_out_bf[l], conv_w=conv_w_pad[l],
                  ln1_g=ln1_g[l].reshape(1, -1), ln1_b=ln1_b[l].reshape(1, -1),
                  rw_hi=rw_hi[l], rw_lo=rw_lo[l], r_b=router_b[l].reshape(1, -1))
        p = _in_proj(geom, x, mod, w_in_bf, l)
        kv_shape = (n_ctx, t_ctx, ATT_KV_HEADS, HEAD_DIM)
        new_k.append(p[:geom.tok_ctx, COL_KB:COL_KB + ATT_KV_W].astype(F32).reshape(kv_shape))
        new_v.append(p[:geom.tok_ctx, COL_VB:COL_VB + ATT_KV_W].astype(F32).reshape(kv_shape))
        yb = _attn_lat(geom, p, ck[:, l], cv[:, l], rope_tabs, attn_sink[l], _attn_ctx(geom, p, attn_sink[l]))
        g_norm = gla_norm_g[l].reshape(1, GLA_DV)
        yc, s_ctx = _gla(geom, p, False, w2[l], b2[l], g_norm)
        yc, _ = _gla(geom, p, True, w2[l], b2[l], g_norm, s0t[:, l], yc)
        new_s.append(jnp.transpose(s_ctx.reshape(n_ctx, 2, GLA_DV, GLA_HEADS, GLA_DK), (0, 1, 3, 4, 2)))
        x1, h2_tiles, top_idx, top_gate, rank, tile_counts = _merge(geom, x, p, yb, yc, mod, lw, dn_alpha)
        dest, blk_e, n_used, pad_start, pad_n = _route(top_idx, rank, tile_counts, MOE_ROWS, TOK_TILE)
        xs = _dispatch(h2_tiles, dest, pad_start, pad_n, n_used, blk_e.shape[0])
        y_rows = _experts(xs, l, blk_e, n_used, w_gate_up, b_gate_up, w_down, b_down)
        x = _combine(geom, y_rows, dest, x1, top_gate, mod, ln2_g[l].reshape(1, -1), ln2_b[l].reshape(1, -1),
                     dn_alpha)
    y_prompt = x[0].reshape(n_ctx, t_ctx, D_MODEL)
    y_sample = x[1].reshape(n_lat, t_lat, D_MODEL)
    return (y_prompt, y_sample, jnp.stack(new_k, axis=1), jnp.stack(new_v, axis=1), jnp.stack(new_s, axis=1))
```
